```python
import math
import jax
import jax.numpy as jnp
from jax import lax
import numpy as np

D_MODEL = 4096
BATCH = 16
SEQ = 256
DEPTH = 2
DEC_BATCH = 4
DEC_SEQ = 1024
PAST_LEN = 256

GRID_W = 64
HEAD_DIM = 128
MIX_W = D_MODEL
GROUP_W = MIX_W // 4
A_HEADS = GROUP_W // HEAD_DIM
A_DH = HEAD_DIM // 2
B_HEADS = GROUP_W // HEAD_DIM
B_KV = B_HEADS // 4
C_HEADS = GROUP_W // HEAD_DIM
C_KV = C_HEADS // 4
WINDOW = 128
SSM_P = 64
SSM_INNER = GROUP_W
SSM_HEADS = SSM_INNER // SSM_P
SSM_GROUPS = 2
SSM_N = 128
SSM_CHUNK = 128
D_CONV = 5
CONV_DIM = SSM_INNER + 2 * SSM_GROUPS * SSM_N
D_FF = 4 * D_MODEL
BLOCK_Q = 128
ROPE_BASE = 10000.0
EPS = 1e-6
N_MOD = 6
IN_SPLITS = (A_HEADS * 2 * A_DH, A_HEADS * 2 * A_DH, A_HEADS * 2 * A_DH,
             B_HEADS * HEAD_DIM, B_KV * HEAD_DIM, B_KV * HEAD_DIM,
             C_HEADS * HEAD_DIM, C_KV * HEAD_DIM, C_KV * HEAD_DIM,
             SSM_INNER, CONV_DIM, 2 * SSM_HEADS)
IN_W = sum(IN_SPLITS)
SPLIT_IDX = [int(v) for v in np.cumsum(IN_SPLITS)[:-1]]

kernel_name = 'hybrid_prefix_flow_trunk_step'


def rms_norm(x, g):
    xf = x.astype(jnp.float32)
    y = xf * lax.rsqrt(jnp.mean(xf * xf, axis=-1, keepdims=True) + EPS)
    return (y * g.astype(jnp.float32)).astype(x.dtype)


def axial_rope(x):
    n_tok, d = x.shape[1], x.shape[-1]
    n_rows = n_tok // GRID_W
    row = jnp.repeat(jnp.arange(n_rows), GRID_W, total_repeat_length=n_tok)
    col = jnp.tile(jnp.arange(GRID_W), n_rows)
    da = d // 2
    inv_freq = 1.0 / (ROPE_BASE ** (jnp.arange(0, da, 2, dtype=jnp.float32) / da))
    xf = x.astype(jnp.float32)

    def rotate(xa, pos):
        ang = pos.astype(jnp.float32)[:, None] * inv_freq
        cos = jnp.cos(ang)[None, :, None, :]
        sin = jnp.sin(ang)[None, :, None, :]
        x1, x2 = xa[..., : da // 2], xa[..., da // 2:]
        return jnp.concatenate([x1 * cos - x2 * sin, x2 * cos + x1 * sin], axis=-1)

    out = jnp.concatenate([rotate(xf[..., :da], row), rotate(xf[..., da:], col)], axis=-1)
    return out.astype(x.dtype)


def rope_pairs(u):
    s = u.shape
    return axial_rope(u.reshape(s[0], s[1], s[2] * s[3], s[4])).reshape(s)


def dense_gqa(q, k, v, sink):
    bsz, n_q, hq, d = q.shape
    hkv = k.shape[2]
    g = hq // hkv
    nb = n_q // BLOCK_Q
    qb = jnp.moveaxis(q.reshape(bsz, nb, BLOCK_Q, hkv, g, d), 1, 0)
    scale = d ** -0.5

    def block(qblk):
        s = jnp.einsum('bqhgd,bkhd->bhgqk', qblk, k).astype(jnp.float32) * scale
        if sink is None:
            p = jax.nn.softmax(s, axis=-1)
        else:
            sk = jnp.broadcast_to(sink.astype(jnp.float32).reshape(hkv, g)[None, :, :, None, None], s.shape[:-1] + (1,))
            p = jax.nn.softmax(jnp.concatenate([s, sk], axis=-1), axis=-1)[..., :-1]
        return jnp.einsum('bhgqk,bkhd->bqhgd', p.astype(v.dtype), v)

    o = lax.map(block, qb)
    return jnp.moveaxis(o, 0, 1).reshape(bsz, n_q, hq, d)


def diff_attention(q, k, v, lam):
    bsz, n_q, nh, _, dh = q.shape
    nb = n_q // BLOCK_Q
    qb = jnp.moveaxis(q.reshape(bsz, nb, BLOCK_Q, nh, 2, dh), 1, 0)
    scale = dh ** -0.5

    def block(qblk):
        s = jnp.einsum('bqhcd,bkhcd->bhcqk', qblk, k).astype(jnp.float32) * scale
        p = jax.nn.softmax(s, axis=-1)
        p = p[:, :, 0] - lam * p[:, :, 1]
        return jnp.einsum('bhqk,bkhd->bqhd', p.astype(v.dtype), v)

    o = lax.map(block, qb)
    return jnp.moveaxis(o, 0, 1).reshape(bsz, n_q, nh, v.shape[-1])


def window_attention(q, k, v, k_ctx, v_ctx, sink):
    bsz, n_tok, hq, d = q.shape
    hkv = k.shape[2]
    g = hq // hkv
    nb = n_tok // BLOCK_Q
    n_ctx = k_ctx.shape[1]
    pad = ((0, 0), (BLOCK_Q, BLOCK_Q), (0, 0), (0, 0))
    kp = jnp.pad(k, pad).reshape(bsz, nb + 2, BLOCK_Q, hkv, d)
    vp = jnp.pad(v, pad).reshape(bsz, nb + 2, BLOCK_Q, hkv, d)
    kw = jnp.concatenate([kp[:, :-2], kp[:, 1:-1], kp[:, 2:]], axis=2)
    vw = jnp.concatenate([vp[:, :-2], vp[:, 1:-1], vp[:, 2:]], axis=2)
    qb = q.reshape(bsz, nb, BLOCK_Q, hkv, g, d)
    scale = d ** -0.5
    s_loc = jnp.einsum('bnqhgd,bnkhd->bnhgqk', qb, kw).astype(jnp.float32) * scale
    qpos = jnp.arange(nb)[:, None] * BLOCK_Q + jnp.arange(BLOCK_Q)[None, :]
    kpos = jnp.arange(nb)[:, None] * BLOCK_Q - BLOCK_Q + jnp.arange(3 * BLOCK_Q)[None, :]
    valid = (kpos >= 0) & (kpos < n_tok)
    mask = (jnp.abs(qpos[:, :, None] - kpos[:, None, :]) <= WINDOW) & valid[:, None, :]
    s_loc = jnp.where(mask[None, :, None, None], s_loc, -jnp.inf)
    s_ctx = jnp.einsum('bnqhgd,bkhd->bnhgqk', qb, k_ctx).astype(jnp.float32) * scale
    s_sink = jnp.broadcast_to(sink.astype(jnp.float32).reshape(hkv, g)[None, None, :, :, None, None], s_ctx.shape[:-1] + (1,))
    p = jax.nn.softmax(jnp.concatenate([s_ctx, s_loc, s_sink], axis=-1), axis=-1)
    p_ctx = p[..., :n_ctx].astype(v.dtype)
    p_loc = p[..., n_ctx:n_ctx + 3 * BLOCK_Q].astype(v.dtype)
    o = jnp.einsum('bnhgqk,bkhd->bnqhgd', p_ctx, v_ctx) + jnp.einsum('bnhgqk,bnkhd->bnqhgd', p_loc, vw)
    return o.reshape(bsz, n_tok, hq, d)


def centred_conv(u, w, b):
    out = lax.conv_general_dilated(
        u, w[:, None, :].astype(u.dtype), window_strides=(1,),
        padding=[(D_CONV // 2, D_CONV // 2)], dimension_numbers=('NWC', 'WIO', 'NWC'),
        feature_group_count=u.shape[-1])
    return out + b.astype(u.dtype)


def ssd_scan(x, dt, a, bm, cm, h0):
    bsz, n_tok, nh, hp = x.shape
    ng, ns = bm.shape[2], bm.shape[3]
    hg = nh // ng
    nc = n_tok // SSM_CHUNK
    xc = x.reshape(bsz, nc, SSM_CHUNK, ng, hg, hp)
    dtc = dt.reshape(bsz, nc, SSM_CHUNK, ng, hg)
    bc = bm.reshape(bsz, nc, SSM_CHUNK, ng, ns)
    cc = cm.reshape(bsz, nc, SSM_CHUNK, ng, ns)
    acum = jnp.cumsum(dtc * a.reshape(ng, hg), axis=2)
    lower = jnp.tril(jnp.ones((SSM_CHUNK, SSM_CHUNK), dtype=bool))[:, :, None, None]
    seg = acum[:, :, :, None] - acum[:, :, None, :]
    decay = jnp.exp(jnp.where(lower, seg, -jnp.inf))
    cb = jnp.einsum('bcign,bcjgn->bcijg', cc, bc)
    xdt = xc * dtc[..., None]
    y_diag = jnp.einsum('bcijgh,bcjghp->bcighp', cb[..., None] * decay, xdt)
    to_end = jnp.exp(acum[:, :, -1:] - acum)
    states = jnp.einsum('bcjgn,bcjghp->bcghpn', bc, xdt * to_end[..., None])
    chunk_decay = jnp.exp(acum[:, :, -1])

    def step(h, inp):
        st, dec = inp
        return h * dec[..., None, None] + st, h

    h_last, h_in = lax.scan(step, h0.reshape(bsz, ng, hg, hp, ns),
                            (jnp.moveaxis(states, 1, 0), jnp.moveaxis(chunk_decay, 1, 0)))
    h_in = jnp.moveaxis(h_in, 0, 1)
    y_off = jnp.einsum('bcign,bcghpn->bcighp', cc, h_in) * jnp.exp(acum)[..., None]
    return (y_diag + y_off).reshape(bsz, n_tok, nh, hp), h_last.reshape(bsz, nh, hp, ns)


def flip_seq(u):
    return jnp.flip(u, axis=1)


def bidir_ssd(z, xbc, dt_raw, conv_w, conv_b, dt_bias, a_log, d_skip, g_norm, h0):
    bsz, n_tok, _ = z.shape
    xbc = jax.nn.silu(centred_conv(xbc, conv_w, conv_b)).astype(jnp.float32)
    xs, bm, cm = jnp.split(xbc, [SSM_INNER, SSM_INNER + SSM_GROUPS * SSM_N], axis=-1)
    xs = xs.reshape(bsz, n_tok, SSM_HEADS, SSM_P)
    bm = bm.reshape(bsz, n_tok, SSM_GROUPS, SSM_N)
    cm = cm.reshape(bsz, n_tok, SSM_GROUPS, SSM_N)
    dt = jax.nn.softplus(dt_raw.astype(jnp.float32).reshape(bsz, n_tok, 2, SSM_HEADS) + dt_bias.astype(jnp.float32))
    a = -jnp.exp(a_log.astype(jnp.float32))
    h0 = h0.astype(jnp.float32)
    y_f, h_f = ssd_scan(xs, dt[:, :, 0], a[0], bm, cm, h0[:, 0])
    y_b, h_b = ssd_scan(flip_seq(xs), flip_seq(dt[:, :, 1]), a[1], flip_seq(bm), flip_seq(cm), h0[:, 1])
    d_tot = (d_skip[0] + d_skip[1]).astype(jnp.float32)
    y = y_f + flip_seq(y_b) + d_tot[:, None] * xs
    y = y.reshape(bsz, n_tok, SSM_INNER) * jax.nn.silu(z.astype(jnp.float32))
    yg = y.reshape(bsz, n_tok, SSM_GROUPS, SSM_INNER // SSM_GROUPS)
    yg = yg * lax.rsqrt(jnp.mean(yg * yg, axis=-1, keepdims=True) + EPS)
    y = yg.reshape(bsz, n_tok, SSM_INNER) * g_norm.astype(jnp.float32)
    return y.astype(z.dtype), jnp.stack([h_f, h_b], axis=1).astype(z.dtype)


def trunk_layer(x, cvec, lp, layer_idx, ctx_cache):
    (w_mod, b_mod, g_pre_mix, g_post_mix, g_pre_ffn, g_post_ffn, w_in, w_out,
     lam_q1, lam_k1, lam_q2, lam_k2, g_subln, g_qnorm, g_knorm, sink,
     conv_w, conv_b, dt_bias, a_log, d_skip, g_ssm_norm, w_up, w_down) = lp
    bsz, n_tok, _ = x.shape
    lambda_init = 0.8 - 0.6 * math.exp(-0.3 * layer_idx)
    sh1, sc1, gt1, sh2, sc2, gt2 = jnp.split(jax.nn.silu(cvec) @ w_mod + b_mod, N_MOD, axis=-1)

    h = rms_norm(x, g_pre_mix) * (1.0 + sc1) + sh1
    qa, ka, va, qb, kb, vb, qc, kc, vc, z, xbc, dt_raw = jnp.split(h @ w_in, SPLIT_IDX, axis=-1)
    qa = qa.reshape(bsz, n_tok, A_HEADS, 2, A_DH)
    ka = ka.reshape(bsz, n_tok, A_HEADS, 2, A_DH)
    va = va.reshape(bsz, n_tok, A_HEADS, 2 * A_DH)
    qb = rms_norm(qb.reshape(bsz, n_tok, B_HEADS, HEAD_DIM), g_qnorm)
    kb = rms_norm(kb.reshape(bsz, n_tok, B_KV, HEAD_DIM), g_knorm)
    vb = vb.reshape(bsz, n_tok, B_KV, HEAD_DIM)
    qc = qc.reshape(bsz, n_tok, C_HEADS, HEAD_DIM)
    kc = kc.reshape(bsz, n_tok, C_KV, HEAD_DIM)
    vc = vc.reshape(bsz, n_tok, C_KV, HEAD_DIM)
    lam = (jnp.exp(jnp.sum(lam_q1.astype(jnp.float32) * lam_k1.astype(jnp.float32)))
           - jnp.exp(jnp.sum(lam_q2.astype(jnp.float32) * lam_k2.astype(jnp.float32))) + lambda_init)

    if ctx_cache is None:
        oa = diff_attention(qa, ka, va, lam)
        ob = dense_gqa(qb, kb, vb, None)
        oc = dense_gqa(qc, kc, vc, sink)
        h0 = jnp.zeros((bsz, 2, SSM_HEADS, SSM_P, SSM_N), jnp.float32)
    else:
        ck_a, cv_a, ck_b, cv_b, ck_c, cv_c, h0 = ctx_cache
        n_ctx = ck_a.shape[1]
        ka_all = jnp.concatenate([ck_a.reshape(bsz, n_ctx, A_HEADS, 2, A_DH), rope_pairs(ka)], axis=1)
        oa = diff_attention(rope_pairs(qa), ka_all, jnp.concatenate([cv_a, va], axis=1), lam)
        ob = dense_gqa(axial_rope(qb), jnp.concatenate([ck_b, axial_rope(kb)], axis=1),
                       jnp.concatenate([cv_b, vb], axis=1), None)
        oc = window_attention(axial_rope(qc), axial_rope(kc), vc, ck_c, cv_c, sink)
    oa = rms_norm(oa, g_subln) * (1.0 - lambda_init)
    od, h_end = bidir_ssd(z, xbc, dt_raw, conv_w, conv_b, dt_bias, a_log, d_skip, g_ssm_norm, h0)

    mix = jnp.concatenate([oa.reshape(bsz, n_tok, GROUP_W), ob.reshape(bsz, n_tok, GROUP_W),
                           oc.reshape(bsz, n_tok, GROUP_W), od], axis=-1) @ w_out
    x = x + gt1 * rms_norm(mix, g_post_mix)

    h2 = rms_norm(x, g_pre_ffn) * (1.0 + sc2) + sh2
    u = jax.nn.relu(h2 @ w_up)
    x = x + gt2 * rms_norm((u * u) @ w_down, g_post_ffn)

    if ctx_cache is None:
        ctx_out = (ka.reshape(bsz, n_tok, A_HEADS, 2 * A_DH), va, kb, vb, kc, vc, h_end)
    else:
        ctx_out = None
    return x, ctx_out


def setup_inputs(seed: int = 0) -> dict:
    key = jax.random.key(seed)
    keys = jax.random.split(key, 40)
    counter = iter(range(40))
    f32 = jnp.float32

    def nrm(shape, scale):
        return jax.random.normal(keys[next(counter)], shape, f32) * scale

    def gain(shape):
        return 1.0 + nrm(shape, 0.02)

    def unif(shape, lo, hi):
        return jax.random.uniform(keys[next(counter)], shape, f32, lo, hi)

    dt0 = jnp.exp(unif((DEPTH, 2, SSM_HEADS), math.log(1e-3), math.log(1e-1)))
    dt_bias = dt0 + jnp.log(-jnp.expm1(-dt0))
    a_log = jnp.log(unif((DEPTH, 2, SSM_HEADS), 1.0, 16.0))
    return {
        'x_prompt': nrm((BATCH, SEQ, D_MODEL), 1.0),
        'x_sample': nrm((DEC_BATCH, DEC_SEQ, D_MODEL), 1.0),
        'c': nrm((DEC_BATCH, D_MODEL), 1.0),
        'cache_a_k': nrm((DEC_BATCH, DEPTH, PAST_LEN, A_HEADS, 2 * A_DH), 1.0),
        'cache_a_v': nrm((DEC_BATCH, DEPTH, PAST_LEN, A_HEADS, 2 * A_DH), 1.0),
        'cache_b_k': nrm((DEC_BATCH, DEPTH, PAST_LEN, B_KV, HEAD_DIM), 1.0),
        'cache_b_v': nrm((DEC_BATCH, DEPTH, PAST_LEN, B_KV, HEAD_DIM), 1.0),
        'cache_c_k': nrm((DEC_BATCH, DEPTH, PAST_LEN, C_KV, HEAD_DIM), 1.0),
        'cache_c_v': nrm((DEC_BATCH, DEPTH, PAST_LEN, C_KV, HEAD_DIM), 1.0),
        'state_ssm': nrm((DEC_BATCH, DEPTH, 2, SSM_HEADS, SSM_P, SSM_N), 0.1),
        'c_ctx': nrm((D_MODEL,), 1.0),
        'w_mod': nrm((DEPTH, D_MODEL, N_MOD * D_MODEL), 0.5 * D_MODEL ** -0.5),
        'b_mod': nrm((DEPTH, N_MOD * D_MODEL), 0.02),
        'g_pre_mix': gain((DEPTH, D_MODEL)),
        'g_post_mix': gain((DEPTH, D_MODEL)),
        'g_pre_ffn': gain((DEPTH, D_MODEL)),
        'g_post_ffn': gain((DEPTH, D_MODEL)),
        'w_in': nrm((DEPTH, D_MODEL, IN_W), D_MODEL ** -0.5),
        'w_out': nrm((DEPTH, MIX_W, D_MODEL), MIX_W ** -0.5),
        'lam_q1': nrm((DEPTH, A_DH), 0.1),
        'lam_k1': nrm((DEPTH, A_DH), 0.1),
        'lam_q2': nrm((DEPTH, A_DH), 0.1),
        'lam_k2': nrm((DEPTH, A_DH), 0.1),
        'g_subln': gain((DEPTH, 2 * A_DH)),
        'g_qnorm': gain((DEPTH, HEAD_DIM)),
        'g_knorm': gain((DEPTH, HEAD_DIM)),
        'sink': nrm((DEPTH, C_HEADS), 0.5),
        'conv_w': nrm((DEPTH, D_CONV, CONV_DIM), D_CONV ** -0.5),
        'conv_b': nrm((DEPTH, CONV_DIM), 0.02),
        'dt_bias': dt_bias,
        'a_log': a_log,
        'd_skip': 1.0 + nrm((DEPTH, 2, SSM_HEADS), 0.1),
        'g_ssm_norm': gain((DEPTH, SSM_INNER)),
        'w_up': nrm((DEPTH, D_MODEL, D_FF), D_MODEL ** -0.5),
        'w_down': nrm((DEPTH, D_FF, D_MODEL), D_FF ** -0.5),
    }


def reference(x_prompt, x_sample, c, cache_a_k, cache_a_v, cache_b_k, cache_b_v, cache_c_k, cache_c_v,
              state_ssm, c_ctx, w_mod, b_mod, g_pre_mix, g_post_mix, g_pre_ffn, g_post_ffn, w_in, w_out,
              lam_q1, lam_k1, lam_q2, lam_k2, g_subln, g_qnorm, g_knorm, sink, conv_w, conv_b,
              dt_bias, a_log, d_skip, g_ssm_norm, w_up, w_down):
    layer_weights = (w_mod, b_mod, g_pre_mix, g_post_mix, g_pre_ffn, g_post_ffn, w_in, w_out,
                     lam_q1, lam_k1, lam_q2, lam_k2, g_subln, g_qnorm, g_knorm, sink,
                     conv_w, conv_b, dt_bias, a_log, d_skip, g_ssm_norm, w_up, w_down)
    ctx_vec = c_ctx[None, None, :]
    lat_vec = c[:, None, :]
    y_prompt = x_prompt
    y_sample = x_sample
    ctx_layers = []
    for layer in range(DEPTH):
        lp = [w[layer] for w in layer_weights]
        y_prompt, ctx_out = trunk_layer(y_prompt, ctx_vec, lp, layer, None)
        ctx_layers.append(ctx_out)
        cached = (cache_a_k[:, layer], cache_a_v[:, layer], cache_b_k[:, layer], cache_b_v[:, layer],
                  cache_c_k[:, layer], cache_c_v[:, layer], state_ssm[:, layer])
        y_sample, _ = trunk_layer(y_sample, lat_vec, lp, layer, cached)
    new_a_k, new_a_v, new_b_k, new_b_v, new_c_k, new_c_v, new_ssm = [
        jnp.stack([t[i] for t in ctx_layers], axis=1) for i in range(7)]
    return (y_prompt, y_sample, new_a_k, new_a_v, new_b_k, new_b_v, new_c_k, new_c_v, new_ssm)
```

```python
import functools
import math

import numpy as np
import jax
import jax.numpy as jnp
from jax import lax
from jax.experimental import pallas as pl
from jax.experimental.pallas import tpu as pltpu

F32 = jnp.float32
BF16 = jnp.bfloat16

D = 4096
BATCH, SEQ = 16, 256
DEC_BATCH, DEC_SEQ = 4, 1024
PAST = 256
DEPTH = 2
GRID_W = 64
HD = 128
N_HEADS = 8
N_KV = 2
GQ = N_HEADS // N_KV
WINDOW = 128
QB = 128
SSM_H, SSM_P, SSM_N, SSM_G = 16, 64, 128, 2
SSM_INNER = SSM_H * SSM_P
CHUNK = 128
D_CONV = 5
CONV_DIM = SSM_INNER + 2 * SSM_G * SSM_N
D_FF = 4 * D
EPS = 1e-6
ROPE_BASE = 10000.0
N_MOD = 6

R_CTX = BATCH * SEQ
R_LAT = DEC_BATCH * DEC_SEQ
R = R_CTX + R_LAT

C_QA, C_KA, C_VA = 0, 1024, 2048
C_QB, C_KB, C_VB = 3072, 4096, 4352
C_QC, C_KC, C_VC = 4608, 5632, 5888
C_Z, C_XBC, C_DT = 6144, 7168, 8704
IN_W = C_DT + 2 * SSM_H
IN_WP = 9216

VMEM_LIMIT = 56 * 1024 * 1024


def _sds(shape, dtype):
    return jax.ShapeDtypeStruct(shape, dtype)


def _params(n_grid, vmem=VMEM_LIMIT):
    return pltpu.CompilerParams(dimension_semantics=("arbitrary",) * n_grid, vmem_limit_bytes=vmem)


def _rms(x, g):
    return x * lax.rsqrt(jnp.mean(x * x, axis=-1, keepdims=True) + EPS) * g


def _sigmoid(x):
    return 1.0 / (1.0 + jnp.exp(-x))


def _mod_row(i, tm):
    r = i * tm
    return jnp.where(r >= R_CTX, 1 + (r - R_CTX) // DEC_SEQ, 0)


def _mod_kernel(c_ref, w_ref, b_ref, o_ref):
    c = c_ref[...]
    s = (c * _sigmoid(c)).astype(BF16)
    o_ref[...] = jnp.dot(s, w_ref[...].astype(BF16), preferred_element_type=F32) + b_ref[...]


def _mod_call(cvec, w_mod, b_mod):
    tn = 512
    return pl.pallas_call(
        _mod_kernel,
        grid=(DEPTH, N_MOD * D // tn),
        in_specs=[pl.BlockSpec((8, D), lambda l, j: (0, 0)),
                  pl.BlockSpec((None, D, tn), lambda l, j: (l, 0, j)),
                  pl.BlockSpec((None, 1, tn), lambda l, j: (l, 0, j))],
        out_specs=pl.BlockSpec((None, 8, tn), lambda l, j: (l, 0, j)),
        out_shape=_sds((DEPTH, 8, N_MOD * D), F32),
        compiler_params=_params(2),
        name="modulation",
    )(cvec, w_mod, b_mod.reshape(DEPTH, 1, N_MOD * D))


def _pre_kernel(x_ref, g_ref, sc_ref, sh_ref, h_ref):
    h_ref[...] = (_rms(x_ref[...], g_ref[...]) * (1.0 + sc_ref[...]) + sh_ref[...]).astype(BF16)


def _mod_spec(layer, chunk, tm):
    return pl.BlockSpec((None, None, 1, D), lambda i: (layer, _mod_row(i, tm), 0, chunk))


def _gain_spec(layer):
    return pl.BlockSpec((None, 1, D), lambda i: (layer, 0, 0))


def _pre_call(x, g, mods4, layer):
    tm = 256
    return pl.pallas_call(
        _pre_kernel,
        grid=(R // tm,),
        in_specs=[pl.BlockSpec((tm, D), lambda i: (i, 0)), _gain_spec(layer),
                  _mod_spec(layer, 1, tm), _mod_spec(layer, 0, tm)],
        out_specs=pl.BlockSpec((tm, D), lambda i: (i, 0)),
        out_shape=_sds((R, D), BF16),
        compiler_params=_params(1),
        name="pre_norm",
    )(x, g.reshape(DEPTH, 1, D), mods4, mods4)


def _post_kernel(*refs, with_next):
    if with_next:
        mm_ref, x_ref, gpost_ref, gt_ref, gpre_ref, sc_ref, sh_ref, xo_ref, h_ref = refs
    else:
        mm_ref, x_ref, gpost_ref, gt_ref, xo_ref = refs
    xn = x_ref[...] + gt_ref[...] * _rms(mm_ref[...], gpost_ref[...])
    xo_ref[...] = xn
    if with_next:
        h_ref[...] = (_rms(xn, gpre_ref[...]) * (1.0 + sc_ref[...]) + sh_ref[...]).astype(BF16)


def _post_call(mm, x, g_post, mods4, layer, gate_chunk, nxt=None):
    tm = 256
    row = pl.BlockSpec((tm, D), lambda i: (i, 0))
    in_specs = [row, row, _gain_spec(layer), _mod_spec(layer, gate_chunk, tm)]
    args = [mm, x, g_post.reshape(DEPTH, 1, D), mods4]
    out_specs, out_shape = [row], [_sds((R, D), F32)]
    if nxt is not None:
        g_pre, nl, sc_chunk, sh_chunk = nxt
        in_specs += [_gain_spec(nl), _mod_spec(nl, sc_chunk, tm), _mod_spec(nl, sh_chunk, tm)]
        args += [g_pre.reshape(DEPTH, 1, D), mods4, mods4]
        out_specs.append(row)
        out_shape.append(_sds((R, D), BF16))
    return pl.pallas_call(
        functools.partial(_post_kernel, with_next=nxt is not None),
        grid=(R // tm,),
        in_specs=in_specs, out_specs=out_specs, out_shape=out_shape,
        compiler_params=_params(1),
        name="post_norm",
    )(*args)


def _mm_kernel(a_ref, w_ref, o_ref, *, nk, relu2):
    acc = jnp.dot(a_ref[...], w_ref[...], preferred_element_type=F32)
    if nk == 1:
        if relu2:
            u = jnp.maximum(acc, 0.0)
            acc = u * u
        o_ref[...] = acc.astype(o_ref.dtype)
    else:
        k = pl.program_id(2)

        @pl.when(k == 0)
        def _():
            o_ref[...] = acc

        @pl.when(k > 0)
        def _():
            o_ref[...] += acc


def _mm_call(a, w, out_dtype, relu2=False, name="proj"):
    m, kdim = a.shape
    n = w.shape[1]
    tm, tn, tk = 1024, 512, 4096
    nk = kdim // tk
    assert nk == 1 or (out_dtype == F32 and not relu2)
    return pl.pallas_call(
        functools.partial(_mm_kernel, nk=nk, relu2=relu2),
        grid=(m // tm, n // tn, nk),
        in_specs=[pl.BlockSpec((tm, tk), lambda i, j, k: (i, k)),
                  pl.BlockSpec((tk, tn), lambda i, j, k: (k, j))],
        out_specs=pl.BlockSpec((tm, tn), lambda i, j, k: (i, j)),
        out_shape=_sds((m, n), out_dtype),
        compiler_params=_params(3),
        name=name,
    )(a, w)


def _rope_tables(pair):
    t = np.arange(DEC_SEQ)
    row, col = t // GRID_W, t % GRID_W
    lane = np.arange(HD)
    da = 2 * pair
    inv_freq = 1.0 / (ROPE_BASE ** (np.arange(0, da, 2, dtype=np.float64) / da))
    pos = np.where((lane // da) % 2 == 0, row[:, None], col[:, None]).astype(np.float64)
    ang = pos * inv_freq[lane % pair][None, :]
    first = (lane % da) < pair
    cos = np.cos(ang)
    sin = np.sin(ang)
    sin_a = np.where(first[None, :], -sin, 0.0)
    sin_b = np.where(first[None, :], 0.0, sin)
    return tuple(jnp.asarray(a, F32) for a in (cos, sin_a, sin_b))


def _rope(x, cos, sin_a, sin_b, pair):
    return x * cos + pltpu.roll(x, HD - pair, 1) * sin_a + pltpu.roll(x, pair, 1) * sin_b


def _softmax(s):
    e = jnp.exp(s - jnp.max(s, axis=-1, keepdims=True))
    return e * (1.0 / jnp.sum(e, axis=-1, keepdims=True))


def _nt_dot(a, b):
    return lax.dot_general(a, b, (((1,), (1,)), ((), ())), preferred_element_type=F32)


def _diff_kernel(*refs, t_len, latent, lambda_init):
    it = iter(refs)
    q_ref, k_ref, v_ref = next(it), next(it), next(it)
    if latent:
        ck_ref, cv_ref, cos_ref, sa_ref, sb_ref = next(it), next(it), next(it), next(it), next(it)
    lq1_ref, lk1_ref, lq2_ref, lk2_ref, g_ref = next(it), next(it), next(it), next(it), next(it)
    next(it)
    o_ref, kb_scr, vb_scr = next(it), next(it), next(it)
    past = PAST if latent else 0
    pair = HD // 8

    k = k_ref[...]
    if latent:
        k = _rope(k, cos_ref[...], sa_ref[...], sb_ref[...], pair)
        kb_scr[0:past, :] = ck_ref[...].astype(BF16)
        vb_scr[0:past, :] = cv_ref[...].astype(BF16)
    kb_scr[past:past + t_len, :] = k.astype(BF16)
    vb_scr[past:past + t_len, :] = v_ref[...].astype(BF16)

    lam = (jnp.exp(jnp.sum(lq1_ref[...] * lk1_ref[...], axis=-1, keepdims=True))
           - jnp.exp(jnp.sum(lq2_ref[...] * lk2_ref[...], axis=-1, keepdims=True)) + lambda_init)
    first_map = lax.broadcasted_iota(jnp.int32, (1, HD), 1) < HD // 2
    scale = (HD // 2) ** -0.5
    g = g_ref[...]

    def q_block(qb, carry):
        rows = pl.ds(pl.multiple_of(qb * QB, QB), QB)
        q = q_ref[rows, :]
        if latent:
            q = _rope(q, cos_ref[rows, :], sa_ref[rows, :], sb_ref[rows, :], pair)
        kb = kb_scr[...]
        s1 = _nt_dot(jnp.where(first_map, q, 0.0).astype(BF16), kb) * scale
        s2 = _nt_dot(jnp.where(first_map, 0.0, q).astype(BF16), kb) * scale
        p = _softmax(s1) - lam * _softmax(s2)
        o = jnp.dot(p.astype(BF16), vb_scr[...], preferred_element_type=F32)
        o_ref[rows, :] = (_rms(o, g) * (1.0 - lambda_init)).astype(BF16)
        return carry

    lax.fori_loop(0, t_len // QB, q_block, 0)


def _diff_call(proj, mix, layer, latent, caches, rope_tabs, lam_vecs, g_subln):
    t_len, nb, rb0 = (DEC_SEQ, DEC_BATCH, R_CTX // DEC_SEQ) if latent else (SEQ, BATCH, 0)
    lambda_init = 0.8 - 0.6 * math.exp(-0.3 * layer)

    def col(c0):
        return pl.BlockSpec((t_len, HD), lambda b, h: (rb0 + b, c0 // HD + h))

    in_specs = [col(C_QA), col(C_KA), col(C_VA)]
    args = [proj, proj, proj]
    if latent:
        cache = pl.BlockSpec((None, None, PAST, HD), lambda b, h: (b, layer, 0, h))
        tab = pl.BlockSpec((t_len, HD), lambda b, h: (0, 0))
        in_specs += [cache, cache, tab, tab, tab]
        args += [c.reshape(DEC_BATCH, DEPTH, PAST, N_HEADS * HD) for c in caches] + list(rope_tabs)
    vec = pl.BlockSpec((None, 1, HD // 2), lambda b, h: (layer, 0, 0))
    in_specs += [vec] * 4 + [pl.BlockSpec((None, 1, HD), lambda b, h: (layer, 0, 0)),
                             pl.BlockSpec(memory_space=pl.ANY)]
    args += [v.reshape(DEPTH, 1, HD // 2) for v in lam_vecs] + [g_subln.reshape(DEPTH, 1, HD), mix]
    s_len = t_len + (PAST if latent else 0)
    return pl.pallas_call(
        functools.partial(_diff_kernel, t_len=t_len, latent=latent, lambda_init=lambda_init),
        grid=(nb, N_HEADS),
        in_specs=in_specs,
        out_specs=pl.BlockSpec((t_len, HD), lambda b, h: (rb0 + b, h)),
        out_shape=_sds((R, D), BF16),
        scratch_shapes=[pltpu.VMEM((s_len, HD), BF16), pltpu.VMEM((s_len, HD), BF16)],
        input_output_aliases={len(args) - 1: 0},
        compiler_params=_params(2),
        name="diff_attn_lat" if latent else "diff_attn_ctx",
    )(*args)


def _gqa_kernel(*refs, t_len, latent, qk_norm, use_sink, window, emit_k):
    it = iter(refs)
    q_ref, k_ref, v_ref = next(it), next(it), next(it)
    if latent:
        ck_ref, cv_ref, cos_ref, sa_ref, sb_ref = next(it), next(it), next(it), next(it), next(it)
    if qk_norm:
        gq_ref, gk_ref = next(it), next(it)
    if use_sink:
        sink_ref = next(it)
    next(it)
    o_ref = next(it)
    if emit_k:
        kn_ref = next(it)
    kb_scr, vb_scr = next(it), next(it)
    past = PAST if latent else 0
    pair = HD // 4
    scale = HD ** -0.5

    k = k_ref[...]
    if qk_norm:
        k = _rms(k, gk_ref[...])
    if emit_k:
        kn_ref[...] = k
    if latent:
        k = _rope(k, cos_ref[...], sa_ref[...], sb_ref[...], pair)
        kb_scr[0:past, :] = ck_ref[...].astype(BF16)
        vb_scr[0:past, :] = cv_ref[...].astype(BF16)
    kb_scr[past:past + t_len, :] = k.astype(BF16)
    vb_scr[past:past + t_len, :] = v_ref[...].astype(BF16)

    for g in range(GQ):
        lanes = slice(g * HD, (g + 1) * HD)
        sink = sink_ref[0:1, g:g + 1] if use_sink else None

        def q_block(qb, carry, lanes=lanes, sink=sink):
            rows = pl.ds(pl.multiple_of(qb * QB, QB), QB)
            q = q_ref[rows, lanes]
            if qk_norm:
                q = _rms(q, gq_ref[...])
            if latent:
                q = _rope(q, cos_ref[rows, :], sa_ref[rows, :], sb_ref[rows, :], pair)
            q = q.astype(BF16)
            if window:
                w_len = 3 * QB
                start = jnp.clip((qb - 1) * QB, 0, t_len - w_len)
                loc = pl.ds(pl.multiple_of(past + start, QB), w_len)
                s_ctx = _nt_dot(q, kb_scr[0:past, :]) * scale
                s_loc = _nt_dot(q, kb_scr[loc, :]) * scale
                qpos = qb * QB + lax.broadcasted_iota(jnp.int32, (QB, 1), 0)
                kpos = start + lax.broadcasted_iota(jnp.int32, (1, w_len), 1)
                s_loc = jnp.where(jnp.abs(qpos - kpos) <= WINDOW, s_loc, -1e30)
                m = jnp.maximum(jnp.maximum(jnp.max(s_ctx, axis=-1, keepdims=True),
                                            jnp.max(s_loc, axis=-1, keepdims=True)), sink)
                e_ctx, e_loc = jnp.exp(s_ctx - m), jnp.exp(s_loc - m)
                inv = 1.0 / (jnp.sum(e_ctx, axis=-1, keepdims=True) + jnp.sum(e_loc, axis=-1, keepdims=True)
                             + jnp.exp(sink - m))
                o = (jnp.dot((e_ctx * inv).astype(BF16), vb_scr[0:past, :], preferred_element_type=F32)
                     + jnp.dot((e_loc * inv).astype(BF16), vb_scr[loc, :], preferred_element_type=F32))
            else:
                s = _nt_dot(q, kb_scr[...]) * scale
                m = jnp.max(s, axis=-1, keepdims=True)
                if use_sink:
                    m = jnp.maximum(m, sink)
                e = jnp.exp(s - m)
                den = jnp.sum(e, axis=-1, keepdims=True)
                if use_sink:
                    den = den + jnp.exp(sink - m)
                o = jnp.dot((e * (1.0 / den)).astype(BF16), vb_scr[...], preferred_element_type=F32)
            o_ref[rows, lanes] = o.astype(BF16)
            return carry

        lax.fori_loop(0, t_len // QB, q_block, 0)


def _gqa_call(proj, mix, layer, latent, cols, mix_col, caches, rope_tabs, qk_gains, sink, window, emit_k):
    t_len, nb, rb0 = (DEC_SEQ, DEC_BATCH, R_CTX // DEC_SEQ) if latent else (SEQ, BATCH, 0)
    c_q, c_k, c_v = cols
    gw = GQ * HD
    in_specs = [pl.BlockSpec((t_len, gw), lambda b, h: (rb0 + b, c_q // gw + h)),
                pl.BlockSpec((t_len, HD), lambda b, h: (rb0 + b, c_k // HD + h)),
                pl.BlockSpec((t_len, HD), lambda b, h: (rb0 + b, c_v // HD + h))]
    args = [proj, proj, proj]
    if latent:
        cache = pl.BlockSpec((None, None, PAST, HD), lambda b, h: (b, layer, 0, h))
        tab = pl.BlockSpec((t_len, HD), lambda b, h: (0, 0))
        in_specs += [cache, cache, tab, tab, tab]
        args += [c.reshape(DEC_BATCH, DEPTH, PAST, N_KV * HD) for c in caches] + list(rope_tabs)
    if qk_gains is not None:
        gain = pl.BlockSpec((None, 1, HD), lambda b, h: (layer, 0, 0))
        in_specs += [gain, gain]
        args += [g.reshape(DEPTH, 1, HD) for g in qk_gains]
    if sink is not None:
        in_specs.append(pl.BlockSpec((None, None, 1, GQ), lambda b, h: (layer, h, 0, 0)))
        args.append(sink.reshape(DEPTH, N_KV, 1, GQ))
    in_specs.append(pl.BlockSpec(memory_space=pl.ANY))
    args.append(mix)
    out_specs = [pl.BlockSpec((t_len, gw), lambda b, h: (rb0 + b, mix_col // gw + h))]
    out_shape = [_sds((R, D), BF16)]
    if emit_k:
        out_specs.append(pl.BlockSpec((t_len, HD), lambda b, h: (b, h)))
        out_shape.append(_sds((nb * t_len, N_KV * HD), F32))
    s_len = t_len + (PAST if latent else 0)
    return pl.pallas_call(
        functools.partial(_gqa_kernel, t_len=t_len, latent=latent, qk_norm=qk_gains is not None,
                          use_sink=sink is not None, window=window, emit_k=emit_k),
        grid=(nb, N_KV),
        in_specs=in_specs, out_specs=out_specs, out_shape=out_shape,
        scratch_shapes=[pltpu.VMEM((s_len, HD), BF16), pltpu.VMEM((s_len, HD), BF16)],
        input_output_aliases={len(args) - 1: 0},
        compiler_params=_params(2),
        name=("win" if window else "gqa") + ("_lat" if latent else "_ctx"),
    )(*args)


def _conv_kernel(x_ref, w_ref, b_ref, o_ref, *, t_len):
    u = x_ref[...]
    w = w_ref[...]
    t = lax.broadcasted_iota(jnp.int32, (t_len, 1), 0)
    half = D_CONV // 2
    acc = u * w[half:half + 1, :] + b_ref[...]
    for s in range(1, half + 1):
        before = jnp.where(t >= s, pltpu.roll(u, s, 0), 0.0)
        after = jnp.where(t < t_len - s, pltpu.roll(u, t_len - s, 0), 0.0)
        acc = acc + before * w[half - s:half - s + 1, :] + after * w[half + s:half + s + 1, :]
    o_ref[...] = acc * _sigmoid(acc)


def _conv_call(proj, conv_w, conv_b, layer, latent):
    t_len, nb, rb0 = (DEC_SEQ, DEC_BATCH, R_CTX // DEC_SEQ) if latent else (SEQ, BATCH, 0)
    tc = 512
    return pl.pallas_call(
        functools.partial(_conv_kernel, t_len=t_len),
        grid=(nb, CONV_DIM // tc),
        in_specs=[pl.BlockSpec((t_len, tc), lambda b, j: (rb0 + b, C_XBC // tc + j)),
                  pl.BlockSpec((None, D_CONV, tc), lambda b, j: (layer, 0, j)),
                  pl.BlockSpec((None, 1, tc), lambda b, j: (layer, 0, j))],
        out_specs=pl.BlockSpec((t_len, tc), lambda b, j: (b, j)),
        out_shape=_sds((nb * t_len, CONV_DIM), F32),
        compiler_params=_params(2),
        name="conv_lat" if latent else "conv_ctx",
    )(proj, conv_w, conv_b.reshape(DEPTH, 1, CONV_DIM))


def _split3(v):
    hi = v.astype(BF16)
    r1 = v - hi.astype(F32)
    mid = r1.astype(BF16)
    lo = (r1 - mid.astype(F32)).astype(BF16)
    return hi, mid, lo


def _softplus(x):
    return jnp.maximum(x, 0.0) + jnp.log1p(jnp.exp(-jnp.abs(x)))


def _scan_kernel(*refs, direction, has_h0, emit_state, n_chunks):
    it = iter(refs)
    xa_ref, dt_ref, dtt_ref, bias_r_ref, bias_c_ref, alog_r_ref, alog_c_ref = (next(it) for _ in range(7))
    if has_h0:
        h0_ref = next(it)
    y_ref = next(it)
    if emit_state:
        hend_ref = next(it)
    h_scr = next(it)
    c = pl.program_id(1)
    hs = slice(direction * SSM_H, (direction + 1) * SSM_H)

    @pl.when(c == 0)
    def _():
        h_scr[...] = h0_ref[...] if has_h0 else jnp.zeros(h_scr.shape, F32)

    ii = lax.broadcasted_iota(jnp.int32, (CHUNK, CHUNK), 0)
    jj = lax.broadcasted_iota(jnp.int32, (CHUNK, CHUNK), 1)
    if direction == 0:
        keep, keep_t = jj <= ii, ii <= jj
        edge = CHUNK - 1
    else:
        keep, keep_t = jj >= ii, ii >= jj
        edge = 0
    tri_col = jnp.where(keep, 1.0, 0.0).astype(BF16)
    tri_row = jnp.where(keep_t, 1.0, 0.0).astype(BF16)

    dt_c = _softplus(dt_ref[:, hs] + bias_r_ref[:, hs])
    da_c = dt_c * (-jnp.exp(alog_r_ref[:, hs]))
    dt_r = _softplus(dtt_ref[hs, :] + bias_c_ref[hs, :])
    da_r = dt_r * (-jnp.exp(alog_c_ref[hs, :]))
    a_col = sum(jnp.dot(tri_col, p, preferred_element_type=F32) for p in _split3(da_c))
    a_row = sum(jnp.dot(p, tri_row, preferred_element_type=F32) for p in _split3(da_r))

    xa = xa_ref[...]
    hg = SSM_H // SSM_G
    for g in range(SSM_G):
        bm = xa[:, SSM_INNER + g * SSM_N:SSM_INNER + (g + 1) * SSM_N].astype(BF16)
        cm = xa[:, SSM_INNER + (SSM_G + g) * SSM_N:SSM_INNER + (SSM_G + g + 1) * SSM_N].astype(BF16)
        cb = _nt_dot(cm, bm)
        for hh in range(hg):
            h = g * hg + hh
            ac = a_col[:, h:h + 1]
            ar = a_row[h:h + 1, :]
            a_tot = a_col[edge:edge + 1, h:h + 1]
            decay = jnp.exp(jnp.where(keep, ac - ar, -1e30))
            xdt = xa[:, h * SSM_P:(h + 1) * SSM_P] * dt_c[:, h:h + 1]
            h_in = h_scr[h]
            y = (jnp.dot((cb * decay).astype(BF16), xdt.astype(BF16), preferred_element_type=F32)
                 + _nt_dot(cm, h_in.astype(BF16)) * jnp.exp(ac))
            y_ref[:, h * SSM_P:(h + 1) * SSM_P] = y
            st = lax.dot_general((xdt * jnp.exp(a_tot - ac)).astype(BF16), bm, (((0,), (0,)), ((), ())),
                                 preferred_element_type=F32)
            h_scr[h] = h_in * jnp.exp(a_tot) + st

    if emit_state:
        @pl.when(c == n_chunks - 1)
        def _():
            hend_ref[...] = h_scr[...]


def _scan_call(xact, proj, dtt, dt_bias, a_log, state, layer, latent, direction):
    t_len, nb, row0 = (DEC_SEQ, DEC_BATCH, R_CTX) if latent else (SEQ, BATCH, 0)
    nc = t_len // CHUNK
    rb0 = row0 // CHUNK

    def chunk(c):
        return c if direction == 0 else nc - 1 - c

    in_specs = [pl.BlockSpec((CHUNK, CONV_DIM), lambda b, c: (b * nc + chunk(c), 0)),
                pl.BlockSpec((CHUNK, HD), lambda b, c: (rb0 + b * nc + chunk(c), C_DT // HD)),
                pl.BlockSpec((2 * SSM_H, CHUNK), lambda b, c: (0, rb0 + b * nc + chunk(c))),
                pl.BlockSpec((None, 1, 2 * SSM_H), lambda b, c: (layer, 0, 0)),
                pl.BlockSpec((None, 2 * SSM_H, 1), lambda b, c: (layer, 0, 0)),
                pl.BlockSpec((None, 1, 2 * SSM_H), lambda b, c: (layer, 0, 0)),
                pl.BlockSpec((None, 2 * SSM_H, 1), lambda b, c: (layer, 0, 0))]
    args = [xact, proj, dtt,
            dt_bias.reshape(DEPTH, 1, 2 * SSM_H), dt_bias.reshape(DEPTH, 2 * SSM_H, 1),
            a_log.reshape(DEPTH, 1, 2 * SSM_H), a_log.reshape(DEPTH, 2 * SSM_H, 1)]
    if latent:
        in_specs.append(pl.BlockSpec((None, None, None, SSM_H, SSM_P, SSM_N),
                                     lambda b, c: (b, layer, direction, 0, 0, 0)))
        args.append(state)
    out_specs = [pl.BlockSpec((CHUNK, SSM_INNER), lambda b, c: (b * nc + chunk(c), 0))]
    out_shape = [_sds((nb * t_len, SSM_INNER), F32)]
    emit_state = not latent
    if emit_state:
        out_specs.append(pl.BlockSpec((None, SSM_H, SSM_P, SSM_N), lambda b, c: (b, 0, 0, 0)))
        out_shape.append(_sds((nb, SSM_H, SSM_P, SSM_N), F32))
    return pl.pallas_call(
        functools.partial(_scan_kernel, direction=direction, has_h0=latent, emit_state=emit_state, n_chunks=nc),
        grid=(nb, nc),
        in_specs=in_specs, out_specs=out_specs, out_shape=out_shape,
        scratch_shapes=[pltpu.VMEM((SSM_H, SSM_P, SSM_N), F32)],
        compiler_params=_params(2),
        name=f"ssd_{'lat' if latent else 'ctx'}_{'fwd' if direction == 0 else 'bwd'}",
    )(*args)


def _ssd_out_kernel(yf_ref, yb_ref, xs_ref, z_ref, dskip_ref, g_ref, mix_ref, o_ref):
    del mix_ref
    dsk = dskip_ref[0:1, :] + dskip_ref[1:2, :]
    z = z_ref[...]
    y = (yf_ref[...] + yb_ref[...] + dsk * xs_ref[...]) * (z * _sigmoid(z))
    gw = SSM_INNER // SSM_G
    g = g_ref[...]
    for k in range(SSM_G):
        lanes = slice(k * gw, (k + 1) * gw)
        yk = y[:, lanes]
        yk = yk * lax.rsqrt(jnp.mean(yk * yk, axis=-1, keepdims=True) + EPS)
        o_ref[:, lanes] = (yk * g[:, lanes]).astype(BF16)


def _ssd_out_call(yf, yb, xact, proj, d_skip, g_norm, mix, layer, latent):
    tm = 256
    rb0 = (R_CTX if latent else 0) // tm
    n_rows = yf.shape[0]
    local = pl.BlockSpec((tm, SSM_INNER), lambda i: (i, 0))
    return pl.pallas_call(
        _ssd_out_kernel,
        grid=(n_rows // tm,),
        in_specs=[local, local, local,
                  pl.BlockSpec((tm, SSM_INNER), lambda i: (rb0 + i, C_Z // SSM_INNER)),
                  pl.BlockSpec((None, 2, SSM_INNER), lambda i: (layer, 0, 0)),
                  pl.BlockSpec((None, 1, SSM_INNER), lambda i: (layer, 0, 0)),
                  pl.BlockSpec(memory_space=pl.ANY)],
        out_specs=pl.BlockSpec((tm, SSM_INNER), lambda i: (rb0 + i, 3)),
        out_shape=_sds((R, D), BF16),
        input_output_aliases={6: 0},
        compiler_params=_params(1),
        name="ssd_out_lat" if latent else "ssd_out_ctx",
    )(yf, yb, xact, proj, jnp.repeat(d_skip, SSM_P, axis=-1), g_norm.reshape(DEPTH, 1, SSM_INNER), mix)


def kernel(x_prompt, x_sample, c, cache_a_k, cache_a_v, cache_b_k, cache_b_v, cache_c_k, cache_c_v, state_ssm, c_ctx, w_mod, b_mod, g_pre_mix, g_post_mix, g_pre_ffn, g_post_ffn, w_in, w_out, lam_q1, lam_k1, lam_q2, lam_k2, g_subln, g_qnorm, g_knorm, sink, conv_w, conv_b, dt_bias, a_log, d_skip, g_ssm_norm, w_up, w_down):
    x = jnp.concatenate([x_prompt.reshape(R_CTX, D), x_sample.reshape(R_LAT, D)], axis=0)
    cvec = jnp.concatenate([c_ctx[None, :], c, jnp.zeros((8 - 1 - DEC_BATCH, D), F32)], axis=0)
    mods4 = _mod_call(cvec, w_mod, b_mod).reshape(DEPTH, 8, 1, N_MOD * D)

    w_in_b = jnp.pad(w_in, ((0, 0), (0, 0), (0, IN_WP - IN_W))).astype(BF16)
    w_out_b, w_up_b, w_down_b = w_out.astype(BF16), w_up.astype(BF16), w_down.astype(BF16)
    tabs_a = _rope_tables(HD // 8)
    tabs_bc = _rope_tables(HD // 4)
    lam_vecs = (lam_q1, lam_k1, lam_q2, lam_k2)

    h = _pre_call(x, g_pre_mix, mods4, 0)
    ctx_out = []
    for layer in range(DEPTH):
        proj = _mm_call(h, w_in_b[layer], F32, name="in_proj")
        mix = jnp.zeros((R, D), BF16)
        new_bk = None
        for latent in (False, True):
            mix = _diff_call(proj, mix, layer, latent, (cache_a_k, cache_a_v), tabs_a, lam_vecs, g_subln)
            res = _gqa_call(proj, mix, layer, latent, (C_QB, C_KB, C_VB), 1024, (cache_b_k, cache_b_v), tabs_bc,
                            (g_qnorm, g_knorm), None, False, not latent)
            if latent:
                mix = res[0]
            else:
                mix, new_bk = res
            mix = _gqa_call(proj, mix, layer, latent, (C_QC, C_KC, C_VC), 2048, (cache_c_k, cache_c_v), tabs_bc,
                            None, sink, latent, False)[0]

        dtt = proj[:, C_DT:C_DT + 2 * SSM_H].T
        h_ends = []
        for latent in (False, True):
            xact = _conv_call(proj, conv_w, conv_b, layer, latent)
            ys = []
            for direction in (0, 1):
                res = _scan_call(xact, proj, dtt, dt_bias, a_log, state_ssm, layer, latent, direction)
                ys.append(res[0])
                if not latent:
                    h_ends.append(res[1])
            mix = _ssd_out_call(ys[0], ys[1], xact, proj, d_skip, g_ssm_norm, mix, layer, latent)

        mm = _mm_call(mix, w_out_b[layer], F32, name="out_proj")
        x, h2 = _post_call(mm, x, g_post_mix, mods4, layer, 2, nxt=(g_pre_ffn, layer, 4, 3))
        u = _mm_call(h2, w_up_b[layer], BF16, relu2=True, name="ffn_up")
        mm = _mm_call(u, w_down_b[layer], F32, name="ffn_down")
        if layer + 1 < DEPTH:
            x, h = _post_call(mm, x, g_post_ffn, mods4, layer, 5, nxt=(g_pre_mix, layer + 1, 1, 0))
        else:
            (x,) = _post_call(mm, x, g_post_ffn, mods4, layer, 5)

        pc = proj[:R_CTX]
        ctx_out.append((
            pc[:, C_KA:C_KA + 1024].reshape(BATCH, SEQ, N_HEADS, HD),
            pc[:, C_VA:C_VA + 1024].reshape(BATCH, SEQ, N_HEADS, HD),
            new_bk.reshape(BATCH, SEQ, N_KV, HD),
            pc[:, C_VB:C_VB + N_KV * HD].reshape(BATCH, SEQ, N_KV, HD),
            pc[:, C_KC:C_KC + N_KV * HD].reshape(BATCH, SEQ, N_KV, HD),
            pc[:, C_VC:C_VC + N_KV * HD].reshape(BATCH, SEQ, N_KV, HD),
            jnp.stack(h_ends, axis=1),
        ))

    new = [jnp.stack([t[i] for t in ctx_out], axis=1) for i in range(7)]
    return (x[:R_CTX].reshape(BATCH, SEQ, D), x[R_CTX:].reshape(DEC_BATCH, DEC_SEQ, D), *new)
```

```python
import functools
import math

import numpy as np
import jax
import jax.numpy as jnp
from jax import lax
from jax.experimental import pallas as pl
from jax.experimental.pallas import tpu as pltpu

F32 = jnp.float32
BF16 = jnp.bfloat16

D = 4096
BATCH, SEQ = 16, 256
DEC_BATCH, DEC_SEQ = 4, 1024
PAST = 256
DEPTH = 2
GRID_W = 64
HD = 128
N_HEADS = 8
N_KV = 2
GQ = N_HEADS // N_KV
WINDOW = 128
QB = 128
SSM_H, SSM_P, SSM_N, SSM_G = 16, 64, 128, 2
SSM_INNER = SSM_H * SSM_P
CHUNK = 128
D_CONV = 5
CONV_DIM = SSM_INNER + 2 * SSM_G * SSM_N
D_FF = 4 * D
EPS = 1e-6
ROPE_BASE = 10000.0
N_MOD = 6
LOG2E = 1.4426950408889634

R_CTX = BATCH * SEQ
R_LAT = DEC_BATCH * DEC_SEQ
R = R_CTX + R_LAT

C_QA, C_KA, C_VA = 0, 1024, 2048
C_QB, C_KB, C_VB = 3072, 4096, 4352
C_QC, C_KC, C_VC = 4608, 5632, 5888
C_Z, C_XBC, C_DT = 6144, 7168, 8704
IN_W = C_DT + 2 * SSM_H
IN_WP = 9216

VMEM_LIMIT = 56 * 1024 * 1024


def _sds(shape, dtype):
    return jax.ShapeDtypeStruct(shape, dtype)


def _params(n_grid, vmem=VMEM_LIMIT):
    return pltpu.CompilerParams(dimension_semantics=("arbitrary",) * n_grid, vmem_limit_bytes=vmem)


def _rms(x, g):
    return x * lax.rsqrt(jnp.mean(x * x, axis=-1, keepdims=True) + EPS) * g


def _sigmoid(x):
    return 1.0 / (1.0 + jnp.exp(-x))


def _mod_row(i, tm):
    r = i * tm
    return jnp.where(r >= R_CTX, 1 + (r - R_CTX) // DEC_SEQ, 0)


def _mod_kernel(c_ref, w_ref, b_ref, o_ref):
    c = c_ref[...]
    s = (c * _sigmoid(c)).astype(BF16)
    o_ref[...] = jnp.dot(s, w_ref[...].astype(BF16), preferred_element_type=F32) + b_ref[...]


def _mod_call(cvec, w_mod, b_mod):
    tn = 512
    return pl.pallas_call(
        _mod_kernel,
        grid=(DEPTH, N_MOD * D // tn),
        in_specs=[pl.BlockSpec((8, D), lambda l, j: (0, 0)),
                  pl.BlockSpec((None, D, tn), lambda l, j: (l, 0, j)),
                  pl.BlockSpec((None, 1, tn), lambda l, j: (l, 0, j))],
        out_specs=pl.BlockSpec((None, 8, tn), lambda l, j: (l, 0, j)),
        out_shape=_sds((DEPTH, 8, N_MOD * D), F32),
        compiler_params=_params(2),
        name="modulation",
    )(cvec, w_mod, b_mod.reshape(DEPTH, 1, N_MOD * D))


def _pre_kernel(x_ref, g_ref, sc_ref, sh_ref, h_ref):
    h_ref[...] = (_rms(x_ref[...], g_ref[...]) * (1.0 + sc_ref[...]) + sh_ref[...]).astype(BF16)


def _mod_spec(layer, chunk, tm):
    return pl.BlockSpec((None, None, 1, D), lambda i: (layer, _mod_row(i, tm), 0, chunk))


def _gain_spec(layer):
    return pl.BlockSpec((None, 1, D), lambda i: (layer, 0, 0))


def _pre_call(x, g, mods4, layer):
    tm = 256
    return pl.pallas_call(
        _pre_kernel,
        grid=(R // tm,),
        in_specs=[pl.BlockSpec((tm, D), lambda i: (i, 0)), _gain_spec(layer),
                  _mod_spec(layer, 1, tm), _mod_spec(layer, 0, tm)],
        out_specs=pl.BlockSpec((tm, D), lambda i: (i, 0)),
        out_shape=_sds((R, D), BF16),
        compiler_params=_params(1),
        name="pre_norm",
    )(x, g.reshape(DEPTH, 1, D), mods4, mods4)


def _post_kernel(*refs, with_next):
    if with_next:
        mm_ref, x_ref, gpost_ref, gt_ref, gpre_ref, sc_ref, sh_ref, xo_ref, h_ref = refs
    else:
        mm_ref, x_ref, gpost_ref, gt_ref, xo_ref = refs
    xn = x_ref[...] + gt_ref[...] * _rms(mm_ref[...], gpost_ref[...])
    xo_ref[...] = xn
    if with_next:
        h_ref[...] = (_rms(xn, gpre_ref[...]) * (1.0 + sc_ref[...]) + sh_ref[...]).astype(BF16)


def _post_call(mm, x, g_post, mods4, layer, gate_chunk, nxt=None):
    tm = 256
    row = pl.BlockSpec((tm, D), lambda i: (i, 0))
    in_specs = [row, row, _gain_spec(layer), _mod_spec(layer, gate_chunk, tm)]
    args = [mm, x, g_post.reshape(DEPTH, 1, D), mods4]
    out_specs, out_shape = [row], [_sds((R, D), F32)]
    if nxt is not None:
        g_pre, nl, sc_chunk, sh_chunk = nxt
        in_specs += [_gain_spec(nl), _mod_spec(nl, sc_chunk, tm), _mod_spec(nl, sh_chunk, tm)]
        args += [g_pre.reshape(DEPTH, 1, D), mods4, mods4]
        out_specs.append(row)
        out_shape.append(_sds((R, D), BF16))
    return pl.pallas_call(
        functools.partial(_post_kernel, with_next=nxt is not None),
        grid=(R // tm,),
        in_specs=in_specs, out_specs=out_specs, out_shape=out_shape,
        compiler_params=_params(1),
        name="post_norm",
    )(*args)


def _mm_kernel(a_ref, w_ref, o_ref, *, nk, relu2):
    acc = jnp.dot(a_ref[...], w_ref[...], preferred_element_type=F32)
    if nk == 1:
        if relu2:
            u = jnp.maximum(acc, 0.0)
            acc = u * u
        o_ref[...] = acc.astype(o_ref.dtype)
    else:
        k = pl.program_id(2)

        @pl.when(k == 0)
        def _():
            o_ref[...] = acc

        @pl.when(k > 0)
        def _():
            o_ref[...] += acc


def _mm_call(a, w, out_dtype, relu2=False, name="proj"):
    m, kdim = a.shape
    n = w.shape[1]
    tm, tn, tk = 1024, 512, 4096
    nk = kdim // tk
    assert nk == 1 or (out_dtype == F32 and not relu2)
    return pl.pallas_call(
        functools.partial(_mm_kernel, nk=nk, relu2=relu2),
        grid=(m // tm, n // tn, nk),
        in_specs=[pl.BlockSpec((tm, tk), lambda i, j, k: (i, k)),
                  pl.BlockSpec((tk, tn), lambda i, j, k: (k, j))],
        out_specs=pl.BlockSpec((tm, tn), lambda i, j, k: (i, j)),
        out_shape=_sds((m, n), out_dtype),
        compiler_params=_params(3),
        name=name,
    )(a, w)


def _rope_tables(pair):
    t = np.arange(DEC_SEQ)
    row, col = t // GRID_W, t % GRID_W
    lane = np.arange(HD)
    da = 2 * pair
    inv_freq = 1.0 / (ROPE_BASE ** (np.arange(0, da, 2, dtype=np.float64) / da))
    pos = np.where((lane // da) % 2 == 0, row[:, None], col[:, None]).astype(np.float64)
    ang = pos * inv_freq[lane % pair][None, :]
    first = (lane % da) < pair
    cos = np.cos(ang)
    sin = np.sin(ang)
    sin_a = np.where(first[None, :], -sin, 0.0)
    sin_b = np.where(first[None, :], 0.0, sin)
    return tuple(jnp.asarray(a, F32) for a in (cos, sin_a, sin_b))


def _rope(x, cos, sin_a, sin_b, pair):
    return x * cos + pltpu.roll(x, HD - pair, 1) * sin_a + pltpu.roll(x, pair, 1) * sin_b


def _nt_dot(a, b):
    return lax.dot_general(a, b, (((1,), (1,)), ((), ())), preferred_element_type=F32)


def _exp2_rows(segs, sink_row=None):
    m = None
    for n, s in enumerate(segs):
        if n == 0 and sink_row is not None:
            mn = jnp.maximum(jnp.max(jnp.maximum(s[:, :HD], sink_row), axis=-1, keepdims=True),
                             jnp.max(s[:, HD:], axis=-1, keepdims=True))
        else:
            mn = jnp.max(s, axis=-1, keepdims=True)
        m = mn if m is None else jnp.maximum(m, mn)
    es = [jnp.exp2(s - m) for s in segs]
    den = sum(jnp.sum(e, axis=-1, keepdims=True) for e in es)
    if sink_row is not None:
        den = den + jnp.sum(jnp.exp2(sink_row - m), axis=-1, keepdims=True)
    return es, den


def _attn_kernel(*refs, kind, t_len, latent, n_kv, gq, qk_norm, use_sink, emit_kv, n_alias, lambda_init):
    it = iter(refs)
    q_ref, k_ref, v_ref = next(it), next(it), next(it)
    if latent:
        ck_ref, cv_ref, cos_ref, sa_ref, sb_ref = next(it), next(it), next(it), next(it), next(it)
    if kind == "diff":
        lq1_ref, lk1_ref, lq2_ref, lk2_ref, gsub_ref = next(it), next(it), next(it), next(it), next(it)
    if qk_norm:
        gq_ref, gk_ref = next(it), next(it)
    if use_sink:
        sink_ref = next(it)
    for _ in range(n_alias):
        next(it)
    o_ref = next(it)
    if emit_kv:
        ko_ref, vo_ref = next(it), next(it)
    kb_scr, vb_scr = next(it), next(it)

    past = PAST if latent else 0
    pair = HD // 8 if kind == "diff" else HD // 4
    dh = HD // 2 if kind == "diff" else HD
    q_scale = dh ** -0.5 * LOG2E

    for kv in range(n_kv):
        kl = slice(kv * HD, (kv + 1) * HD)
        k = k_ref[:, kl]
        v = v_ref[:, kl]
        if qk_norm:
            k = _rms(k, gk_ref[...])
        if emit_kv:
            ko_ref[:, kl] = k
            vo_ref[:, kl] = v
        if latent:
            k = _rope(k, cos_ref[...], sa_ref[...], sb_ref[...], pair)
            kb_scr[kv, 0:past, :] = ck_ref[:, kl].astype(BF16)
            vb_scr[kv, 0:past, :] = cv_ref[:, kl].astype(BF16)
        kb_scr[kv, past:past + t_len, :] = k.astype(BF16)
        vb_scr[kv, past:past + t_len, :] = v.astype(BF16)

    if kind == "diff":
        lam = (jnp.exp(jnp.sum(lq1_ref[...] * lk1_ref[...], axis=-1, keepdims=True))
               - jnp.exp(jnp.sum(lq2_ref[...] * lk2_ref[...], axis=-1, keepdims=True)) + lambda_init)
        first_map = lax.broadcasted_iota(jnp.int32, (1, HD), 1) < HD // 2

    def unit(qb, kv, g):
        rows = pl.ds(qb * QB, QB) if isinstance(qb, int) else pl.ds(pl.multiple_of(qb * QB, QB), QB)
        lanes = slice((kv * gq + g) * HD, (kv * gq + g + 1) * HD)
        q = q_ref[rows, lanes]
        if qk_norm:
            q = _rms(q, gq_ref[...])
        if latent:
            q = _rope(q, cos_ref[rows, :], sa_ref[rows, :], sb_ref[rows, :], pair)
        q = q * q_scale
        if kind == "diff":
            kb, vb = kb_scr[kv], vb_scr[kv]
            (e1,), d1 = _exp2_rows([_nt_dot(jnp.where(first_map, q, 0.0).astype(BF16), kb)])
            (e2,), d2 = _exp2_rows([_nt_dot(jnp.where(first_map, 0.0, q).astype(BF16), kb)])
            o = (jnp.dot(e1.astype(BF16), vb, preferred_element_type=F32) * (1.0 / d1)
                 - jnp.dot(e2.astype(BF16), vb, preferred_element_type=F32) * (lam / d2))
            o = _rms(o, gsub_ref[...]) * (1.0 - lambda_init)
        elif kind == "gqa":
            sink = sink_ref[kv * gq + g] * LOG2E if use_sink else None
            (e,), den = _exp2_rows([_nt_dot(q.astype(BF16), kb_scr[kv])], sink)
            o = jnp.dot(e.astype(BF16), vb_scr[kv], preferred_element_type=F32) * (1.0 / den)
        else:
            sink = sink_ref[kv * gq + g] * LOG2E
            w_len = 3 * QB
            start = jnp.clip((qb - 1) * QB, 0, t_len - w_len)
            loc = pl.ds(pl.multiple_of(past + start, QB), w_len)
            qh = q.astype(BF16)
            s_ctx = _nt_dot(qh, kb_scr[kv, 0:past, :])
            s_loc = _nt_dot(qh, kb_scr[kv, loc, :])
            qpos = qb * QB + lax.broadcasted_iota(jnp.int32, (QB, 1), 0)
            kpos = start + lax.broadcasted_iota(jnp.int32, (1, w_len), 1)
            s_loc = jnp.where(jnp.abs(qpos - kpos) <= WINDOW, s_loc, -1e30)
            (e_ctx, e_loc), den = _exp2_rows([s_ctx, s_loc], sink)
            o = (jnp.dot(e_ctx.astype(BF16), vb_scr[kv, 0:past, :], preferred_element_type=F32)
                 + jnp.dot(e_loc.astype(BF16), vb_scr[kv, loc, :], preferred_element_type=F32)) * (1.0 / den)
        o_ref[rows, lanes] = o.astype(BF16)

    n_blk = t_len // QB
    per_iter = min(n_blk, 4)
    for kv in range(n_kv):
        for g in range(gq):
            if n_blk == per_iter:
                for qb in range(n_blk):
                    unit(qb, kv, g)
            else:
                def some_blocks(i, carry, kv=kv, g=g):
                    for n in range(per_iter):
                        unit(per_iter * i + n, kv, g)
                    return carry

                lax.fori_loop(0, n_blk // per_iter, some_blocks, 0)


def _attn_call(kind, proj, mix, layer, latent, cols, mix_col, n_kv, caches=None, tabs=None, lam_vecs=None,
               g_subln=None, qk_gains=None, sink=None, kv_out=None):
    t_len, nb, rb0 = (DEC_SEQ, DEC_BATCH, R_CTX // DEC_SEQ) if latent else (SEQ, BATCH, 0)
    gq = 1 if kind == "diff" else GQ
    kv_total = N_HEADS // gq
    c_q, c_k, c_v = cols
    qw, kw = n_kv * gq * HD, n_kv * HD
    grid = (nb, kv_total // n_kv)

    def tok(width, c0):
        return pl.BlockSpec((t_len, width), lambda b, h: (rb0 + b, c0 // width + h))

    in_specs = [tok(qw, c_q), tok(kw, c_k), tok(kw, c_v)]
    args = [proj, proj, proj]
    if latent:
        cache = pl.BlockSpec((None, None, PAST, kw), lambda b, h: (b, layer, 0, h))
        tab = pl.BlockSpec((t_len, HD), lambda b, h: (0, 0))
        in_specs += [cache, cache, tab, tab, tab]
        args += [c.reshape(DEC_BATCH, DEPTH, PAST, kv_total * HD) for c in caches] + list(tabs)
    if kind == "diff":
        vec = pl.BlockSpec((None, 1, HD // 2), lambda b, h: (layer, 0, 0))
        in_specs += [vec] * 4 + [pl.BlockSpec((None, 1, HD), lambda b, h: (layer, 0, 0))]
        args += [v.reshape(DEPTH, 1, HD // 2) for v in lam_vecs] + [g_subln.reshape(DEPTH, 1, HD)]
    if qk_gains is not None:
        gain = pl.BlockSpec((None, 1, HD), lambda b, h: (layer, 0, 0))
        in_specs += [gain, gain]
        args += [g.reshape(DEPTH, 1, HD) for g in qk_gains]
    if sink is not None:
        in_specs.append(pl.BlockSpec((None, n_kv * GQ, 1, HD), lambda b, h: (layer, h, 0, 0)))
        args.append(jnp.pad(sink.reshape(DEPTH, N_HEADS, 1, 1), ((0, 0), (0, 0), (0, 0), (0, HD - 1)),
                            constant_values=-1e30))
    aliases = {len(args): 0}
    in_specs.append(pl.BlockSpec(memory_space=pl.ANY))
    args.append(mix)
    out_specs = [tok(qw, mix_col)]
    out_shape = [_sds((R, D), BF16)]
    emit_kv = kv_out is not None
    if emit_kv:
        for n, prev in enumerate(kv_out):
            aliases[len(args)] = 1 + n
            in_specs.append(pl.BlockSpec(memory_space=pl.ANY))
            args.append(prev)
        kv_spec = pl.BlockSpec((None, None, SEQ, kw), lambda b, h: (b, layer, 0, h))
        out_specs += [kv_spec, kv_spec]
        out_shape += [_sds((BATCH, DEPTH, SEQ, kv_total * HD), F32)] * 2
    s_len = t_len + (PAST if latent else 0)
    return pl.pallas_call(
        functools.partial(_attn_kernel, kind=kind, t_len=t_len, latent=latent, n_kv=n_kv, gq=gq,
                          qk_norm=qk_gains is not None, use_sink=sink is not None, emit_kv=emit_kv,
                          n_alias=len(aliases), lambda_init=0.8 - 0.6 * math.exp(-0.3 * layer)),
        grid=grid,
        in_specs=in_specs, out_specs=out_specs, out_shape=out_shape,
        scratch_shapes=[pltpu.VMEM((n_kv, s_len, HD), BF16), pltpu.VMEM((n_kv, s_len, HD), BF16)],
        input_output_aliases=aliases,
        compiler_params=_params(2),
        name=kind + ("_lat" if latent else "_ctx"),
    )(*args)


def _conv_kernel(x_ref, w_ref, b_ref, o_ref, *, t_len):
    u = x_ref[...]
    w = w_ref[...]
    t = lax.broadcasted_iota(jnp.int32, (t_len, 1), 0)
    half = D_CONV // 2
    acc = u * w[half:half + 1, :] + b_ref[...]
    for s in range(1, half + 1):
        before = jnp.where(t >= s, pltpu.roll(u, s, 0), 0.0)
        after = jnp.where(t < t_len - s, pltpu.roll(u, t_len - s, 0), 0.0)
        acc = acc + before * w[half - s:half - s + 1, :] + after * w[half + s:half + s + 1, :]
    o_ref[...] = acc * _sigmoid(acc)


def _conv_call(proj, conv_w, conv_b, layer, latent):
    t_len, nb, rb0 = (DEC_SEQ, DEC_BATCH, R_CTX // DEC_SEQ) if latent else (SEQ, BATCH, 0)
    tc = 512
    return pl.pallas_call(
        functools.partial(_conv_kernel, t_len=t_len),
        grid=(nb, CONV_DIM // tc),
        in_specs=[pl.BlockSpec((t_len, tc), lambda b, j: (rb0 + b, C_XBC // tc + j)),
                  pl.BlockSpec((None, D_CONV, tc), lambda b, j: (layer, 0, j)),
                  pl.BlockSpec((None, 1, tc), lambda b, j: (layer, 0, j))],
        out_specs=pl.BlockSpec((t_len, tc), lambda b, j: (b, j)),
        out_shape=_sds((nb * t_len, CONV_DIM), F32),
        compiler_params=_params(2),
        name="conv_lat" if latent else "conv_ctx",
    )(proj, conv_w, conv_b.reshape(DEPTH, 1, CONV_DIM))


def _split3(v):
    hi = v.astype(BF16).astype(F32)
    r1 = v - hi
    mid = r1.astype(BF16).astype(F32)
    lo = (r1 - mid).astype(BF16).astype(F32)
    return hi, mid, lo


def _exact_dot(tri, v, tri_first):
    parts = [p.astype(BF16) for p in _split3(v)]
    if tri_first:
        return sum(jnp.dot(tri, p, preferred_element_type=F32) for p in parts)
    return sum(jnp.dot(p, tri, preferred_element_type=F32) for p in parts)


def _lane_spread(v, width):
    hi, mid, lo = _split3(v)
    packed = (hi + pltpu.roll(mid, SSM_H, 1) + pltpu.roll(lo, 2 * SSM_H, 1)).astype(BF16)
    n = SSM_H * width
    src = lax.broadcasted_iota(jnp.int32, (HD, n), 0)
    dst = lax.broadcasted_iota(jnp.int32, (HD, n), 1)
    sel = jnp.where((src < 3 * SSM_H) & (src % SSM_H == dst // width), 1.0, 0.0).astype(BF16)
    return jnp.dot(packed, sel, preferred_element_type=F32)


def _softplus(x):
    return jnp.maximum(x, 0.0) + jnp.log1p(jnp.exp(-jnp.abs(x)))


def _scan_kernel(*refs, direction, has_h0, emit_state, n_chunks, n_alias):
    it = iter(refs)
    xa_ref, dt_ref, dtt_ref, bias_r_ref, bias_c_ref, alog_r_ref, alog_c_ref = (next(it) for _ in range(7))
    if has_h0:
        h0_ref = next(it)
    for _ in range(n_alias):
        next(it)
    y_ref = next(it)
    if emit_state:
        hend_ref = next(it)
    ht_scr = next(it)
    c = pl.program_id(1)
    hs = slice(direction * SSM_H, (direction + 1) * SSM_H)

    @pl.when(c == 0)
    def _():
        if has_h0:
            ht_scr[...] = h0_ref[...].reshape(SSM_INNER, SSM_N).T
        else:
            ht_scr[...] = jnp.zeros(ht_scr.shape, F32)

    ii = lax.broadcasted_iota(jnp.int32, (CHUNK, CHUNK), 0)
    jj = lax.broadcasted_iota(jnp.int32, (CHUNK, CHUNK), 1)
    if direction == 0:
        keep, keep_t = jj <= ii, ii <= jj
        edge = CHUNK - 1
    else:
        keep, keep_t = jj >= ii, ii >= jj
        edge = 0
    tri_col = jnp.where(keep, 1.0, 0.0).astype(BF16)
    tri_row = jnp.where(keep_t, 1.0, 0.0).astype(BF16)

    raw = dt_ref[...]
    if direction == 1:
        raw = pltpu.roll(raw, HD - SSM_H, 1)
    head_lane = lax.broadcasted_iota(jnp.int32, (1, HD), 1) < SSM_H
    dt_c = jnp.where(head_lane, _softplus(raw + bias_r_ref[...]), 0.0)
    a_col = _exact_dot(tri_col, dt_c * (-jnp.exp(alog_r_ref[...])), True)
    dt_r = _softplus(dtt_ref[hs, :] + bias_c_ref[hs, :])
    a_row = _exact_dot(tri_row, dt_r * (-jnp.exp(alog_c_ref[hs, :])), False)

    a_sq = _lane_spread(a_col, CHUNK)
    a_hp = _lane_spread(a_col, SSM_P)
    dt_hp = _lane_spread(dt_c, SSM_P)

    xa = xa_ref[...]
    xdt = xa[:, :SSM_INNER] * dt_hp
    a_tot = a_hp[edge:edge + 1, :]
    grow = jnp.exp(a_hp)
    w_end = (xdt * jnp.exp(a_tot - a_hp)).astype(BF16)
    carry = jnp.exp(a_tot)
    lower_half = lax.broadcasted_iota(jnp.int32, (1, 2 * SSM_P), 1) < SSM_P

    hg = SSM_H // SSM_G
    gw = hg * SSM_P
    for g in range(SSM_G):
        bm = xa[:, SSM_INNER + g * SSM_N:SSM_INNER + (g + 1) * SSM_N].astype(BF16)
        cm = xa[:, SSM_INNER + (SSM_G + g) * SSM_N:SSM_INNER + (SSM_G + g + 1) * SSM_N].astype(BF16)
        cb = _nt_dot(cm, bm)
        gl = slice(g * gw, (g + 1) * gw)
        ht = ht_scr[:, gl]
        y_off = jnp.dot(cm, ht.astype(BF16), preferred_element_type=F32) * grow[:, gl]
        for pr in range(hg // 2):
            pl_ = slice(g * gw + pr * 2 * SSM_P, g * gw + (pr + 1) * 2 * SSM_P)
            x_pair = xdt[:, pl_]
            y_pair = y_off[:, pr * 2 * SSM_P:(pr + 1) * 2 * SSM_P]
            for half in range(2):
                h = g * hg + pr * 2 + half
                decay = jnp.exp(jnp.where(keep, a_sq[:, h * CHUNK:(h + 1) * CHUNK] - a_row[h:h + 1, :], -1e30))
                rhs = jnp.where(lower_half if half == 0 else ~lower_half, x_pair, 0.0).astype(BF16)
                y_pair = y_pair + jnp.dot((cb * decay).astype(BF16), rhs, preferred_element_type=F32)
            y_ref[:, pl_] = y_pair
        st = lax.dot_general(bm, w_end[:, gl], (((0,), (0,)), ((), ())), preferred_element_type=F32)
        ht_scr[:, gl] = ht * carry[:, gl] + st

    if emit_state:
        @pl.when(c == n_chunks - 1)
        def _():
            hend_ref[...] = ht_scr[...].T.reshape(SSM_H, SSM_P, SSM_N)


def _pad_lanes(v):
    return jnp.pad(v, ((0, 0), (0, 0), (0, HD - SSM_H))).reshape(DEPTH, 2, 1, HD)


def _scan_call(xact, proj, dtt, dt_bias, a_log, state, layer, latent, direction, state_out=None):
    t_len, nb, row0 = (DEC_SEQ, DEC_BATCH, R_CTX) if latent else (SEQ, BATCH, 0)
    nc = t_len // CHUNK
    rb0 = row0 // CHUNK

    def chunk(c):
        return c if direction == 0 else nc - 1 - c

    row_vec = pl.BlockSpec((None, None, 1, HD), lambda b, c: (layer, direction, 0, 0))
    col_vec = pl.BlockSpec((None, 2 * SSM_H, 1), lambda b, c: (layer, 0, 0))
    in_specs = [pl.BlockSpec((CHUNK, CONV_DIM), lambda b, c: (b * nc + chunk(c), 0)),
                pl.BlockSpec((CHUNK, HD), lambda b, c: (rb0 + b * nc + chunk(c), C_DT // HD)),
                pl.BlockSpec((2 * SSM_H, CHUNK), lambda b, c: (0, rb0 + b * nc + chunk(c))),
                row_vec, col_vec, row_vec, col_vec]
    args = [xact, proj, dtt,
            _pad_lanes(dt_bias), dt_bias.reshape(DEPTH, 2 * SSM_H, 1),
            _pad_lanes(a_log), a_log.reshape(DEPTH, 2 * SSM_H, 1)]
    if latent:
        in_specs.append(pl.BlockSpec((None, None, None, SSM_H, SSM_P, SSM_N),
                                     lambda b, c: (b, layer, direction, 0, 0, 0)))
        args.append(state)
    out_specs = [pl.BlockSpec((CHUNK, SSM_INNER), lambda b, c: (b * nc + chunk(c), 0))]
    out_shape = [_sds((nb * t_len, SSM_INNER), F32)]
    aliases = {}
    emit_state = state_out is not None
    if emit_state:
        for prev in state_out:
            aliases[len(args)] = 1
            in_specs.append(pl.BlockSpec(memory_space=pl.ANY))
            args.append(prev)
        out_specs.append(pl.BlockSpec((None, None, None, SSM_H, SSM_P, SSM_N),
                                      lambda b, c: (b, layer, direction, 0, 0, 0)))
        out_shape.append(_sds((nb, DEPTH, 2, SSM_H, SSM_P, SSM_N), F32))
    return pl.pallas_call(
        functools.partial(_scan_kernel, direction=direction, has_h0=latent, emit_state=emit_state, n_chunks=nc,
                          n_alias=len(aliases)),
        grid=(nb, nc),
        in_specs=in_specs, out_specs=out_specs, out_shape=out_shape,
        scratch_shapes=[pltpu.VMEM((SSM_N, SSM_INNER), F32)],
        input_output_aliases=aliases,
        compiler_params=_params(2),
        name=f"ssd_{'lat' if latent else 'ctx'}_{'fwd' if direction == 0 else 'bwd'}",
    )(*args)


def _ssd_out_kernel(yf_ref, yb_ref, xs_ref, z_ref, dskip_ref, g_ref, mix_ref, o_ref):
    del mix_ref
    dsk = dskip_ref[0:1, :] + dskip_ref[1:2, :]
    z = z_ref[...]
    y = (yf_ref[...] + yb_ref[...] + dsk * xs_ref[...]) * (z * _sigmoid(z))
    gw = SSM_INNER // SSM_G
    g = g_ref[...]
    for k in range(SSM_G):
        lanes = slice(k * gw, (k + 1) * gw)
        yk = y[:, lanes]
        yk = yk * lax.rsqrt(jnp.mean(yk * yk, axis=-1, keepdims=True) + EPS)
        o_ref[:, lanes] = (yk * g[:, lanes]).astype(BF16)


def _ssd_out_call(yf, yb, xact, proj, d_skip, g_norm, mix, layer, latent):
    tm = 256
    rb0 = (R_CTX if latent else 0) // tm
    n_rows = yf.shape[0]
    local = pl.BlockSpec((tm, SSM_INNER), lambda i: (i, 0))
    return pl.pallas_call(
        _ssd_out_kernel,
        grid=(n_rows // tm,),
        in_specs=[local, local, local,
                  pl.BlockSpec((tm, SSM_INNER), lambda i: (rb0 + i, C_Z // SSM_INNER)),
                  pl.BlockSpec((None, 2, SSM_INNER), lambda i: (layer, 0, 0)),
                  pl.BlockSpec((None, 1, SSM_INNER), lambda i: (layer, 0, 0)),
                  pl.BlockSpec(memory_space=pl.ANY)],
        out_specs=pl.BlockSpec((tm, SSM_INNER), lambda i: (rb0 + i, 3)),
        out_shape=_sds((R, D), BF16),
        input_output_aliases={6: 0},
        compiler_params=_params(1),
        name="ssd_out_lat" if latent else "ssd_out_ctx",
    )(yf, yb, xact, proj, jnp.repeat(d_skip, SSM_P, axis=-1), g_norm.reshape(DEPTH, 1, SSM_INNER), mix)


def kernel(x_prompt, x_sample, c, cache_a_k, cache_a_v, cache_b_k, cache_b_v, cache_c_k, cache_c_v, state_ssm, c_ctx, w_mod, b_mod, g_pre_mix, g_post_mix, g_pre_ffn, g_post_ffn, w_in, w_out, lam_q1, lam_k1, lam_q2, lam_k2, g_subln, g_qnorm, g_knorm, sink, conv_w, conv_b, dt_bias, a_log, d_skip, g_ssm_norm, w_up, w_down):
    x = jnp.concatenate([x_prompt.reshape(R_CTX, D), x_sample.reshape(R_LAT, D)], axis=0)
    cvec = jnp.concatenate([c_ctx[None, :], c, jnp.zeros((8 - 1 - DEC_BATCH, D), F32)], axis=0)
    mods4 = _mod_call(cvec, w_mod, b_mod).reshape(DEPTH, 8, 1, N_MOD * D)

    w_in_b = jnp.pad(w_in, ((0, 0), (0, 0), (0, IN_WP - IN_W))).astype(BF16)
    w_out_b, w_up_b, w_down_b = w_out.astype(BF16), w_up.astype(BF16), w_down.astype(BF16)
    tabs_a = _rope_tables(HD // 8)
    tabs_bc = _rope_tables(HD // 4)
    lam_vecs = (lam_q1, lam_k1, lam_q2, lam_k2)

    h = _pre_call(x, g_pre_mix, mods4, 0)
    kv_a, kv_b, kv_c, ssm_new = (), (), (), ()
    for layer in range(DEPTH):
        proj = _mm_call(h, w_in_b[layer], F32, name="in_proj")
        mix = jnp.zeros((R, D), BF16)

        mix, *kv_a = _attn_call("diff", proj, mix, layer, False, (C_QA, C_KA, C_VA), 0, N_HEADS,
                                lam_vecs=lam_vecs, g_subln=g_subln, kv_out=kv_a)
        mix, *kv_b = _attn_call("gqa", proj, mix, layer, False, (C_QB, C_KB, C_VB), 1024, N_KV,
                                qk_gains=(g_qnorm, g_knorm), kv_out=kv_b)
        mix, *kv_c = _attn_call("gqa", proj, mix, layer, False, (C_QC, C_KC, C_VC), 2048, 1,
                                sink=sink, kv_out=kv_c)
        mix, = _attn_call("diff", proj, mix, layer, True, (C_QA, C_KA, C_VA), 0, 1,
                          caches=(cache_a_k, cache_a_v), tabs=tabs_a, lam_vecs=lam_vecs, g_subln=g_subln)
        mix, = _attn_call("gqa", proj, mix, layer, True, (C_QB, C_KB, C_VB), 1024, 1,
                          caches=(cache_b_k, cache_b_v), tabs=tabs_bc, qk_gains=(g_qnorm, g_knorm))
        mix, = _attn_call("win", proj, mix, layer, True, (C_QC, C_KC, C_VC), 2048, 1,
                          caches=(cache_c_k, cache_c_v), tabs=tabs_bc, sink=sink)

        dtt = proj[:, C_DT:C_DT + 2 * SSM_H].T
        for latent in (False, True):
            xact = _conv_call(proj, conv_w, conv_b, layer, latent)
            ys = []
            for direction in (0, 1):
                res = _scan_call(xact, proj, dtt, dt_bias, a_log, state_ssm, layer, latent, direction,
                                 state_out=None if latent else ssm_new)
                ys.append(res[0])
                if not latent:
                    ssm_new = (res[1],)
            mix = _ssd_out_call(ys[0], ys[1], xact, proj, d_skip, g_ssm_norm, mix, layer, latent)

        mm = _mm_call(mix, w_out_b[layer], F32, name="out_proj")
        x, h2 = _post_call(mm, x, g_post_mix, mods4, layer, 2, nxt=(g_pre_ffn, layer, 4, 3))
        u = _mm_call(h2, w_up_b[layer], BF16, relu2=True, name="ffn_up")
        mm = _mm_call(u, w_down_b[layer], F32, name="ffn_down")
        if layer + 1 < DEPTH:
            x, h = _post_call(mm, x, g_post_ffn, mods4, layer, 5, nxt=(g_pre_mix, layer + 1, 1, 0))
        else:
            (x,) = _post_call(mm, x, g_post_ffn, mods4, layer, 5)

    new_kv = [a.reshape(BATCH, DEPTH, SEQ, -1, HD) for a in (*kv_a, *kv_b, *kv_c)]
    return (x[:R_CTX].reshape(BATCH, SEQ, D), x[R_CTX:].reshape(DEC_BATCH, DEC_SEQ, D), *new_kv, ssm_new[0])
```

```python
import functools
import math

import numpy as np
import jax
import jax.numpy as jnp
from jax import lax
from jax.experimental import pallas as pl
from jax.experimental.pallas import tpu as pltpu

F32 = jnp.float32
BF16 = jnp.bfloat16

D = 4096
BATCH, SEQ = 16, 256
DEC_BATCH, DEC_SEQ = 4, 1024
PAST = 256
DEPTH = 2
GRID_W = 64
HD = 128
N_HEADS = 8
N_KV = 2
GQ = N_HEADS // N_KV
WINDOW = 128
QB = 128
SSM_H, SSM_P, SSM_N, SSM_G = 16, 64, 128, 2
SSM_INNER = SSM_H * SSM_P
CHUNK = 128
D_CONV = 5
CONV_DIM = SSM_INNER + 2 * SSM_G * SSM_N
D_FF = 4 * D
EPS = 1e-6
ROPE_BASE = 10000.0
N_MOD = 6
LOG2E = 1.4426950408889634

R_CTX = BATCH * SEQ
R_LAT = DEC_BATCH * DEC_SEQ
R = R_CTX + R_LAT

C_QA, C_KA, C_VA = 0, 1024, 2048
C_QB, C_KB, C_VB = 3072, 4096, 4352
C_QC, C_KC, C_VC = 4608, 5632, 5888
C_Z, C_XBC, C_DT = 6144, 7168, 8704
IN_W = C_DT + 2 * SSM_H
IN_WP = 9216

VMEM_LIMIT = 56 * 1024 * 1024


def _sds(shape, dtype):
    return jax.ShapeDtypeStruct(shape, dtype)


def _params(n_grid, vmem=VMEM_LIMIT):
    return pltpu.CompilerParams(dimension_semantics=("arbitrary",) * n_grid, vmem_limit_bytes=vmem)


def _rms(x, g):
    return x * lax.rsqrt(jnp.mean(x * x, axis=-1, keepdims=True) + EPS) * g


def _sigmoid(x):
    return 1.0 / (1.0 + jnp.exp(-x))


def _mod_row(i, tm):
    r = i * tm
    return jnp.where(r >= R_CTX, 1 + (r - R_CTX) // DEC_SEQ, 0)


def _mod_kernel(c_ref, w_ref, b_ref, o_ref):
    c = c_ref[...]
    s = (c * _sigmoid(c)).astype(BF16)
    o_ref[...] = jnp.dot(s, w_ref[...].astype(BF16), preferred_element_type=F32) + b_ref[...]


def _mod_call(cvec, w_mod, b_mod):
    tn = 512
    return pl.pallas_call(
        _mod_kernel,
        grid=(DEPTH, N_MOD * D // tn),
        in_specs=[pl.BlockSpec((8, D), lambda l, j: (0, 0)),
                  pl.BlockSpec((None, D, tn), lambda l, j: (l, 0, j)),
                  pl.BlockSpec((None, 1, tn), lambda l, j: (l, 0, j))],
        out_specs=pl.BlockSpec((None, 8, tn), lambda l, j: (l, 0, j)),
        out_shape=_sds((DEPTH, 8, N_MOD * D), F32),
        compiler_params=_params(2),
        name="modulation",
    )(cvec, w_mod, b_mod.reshape(DEPTH, 1, N_MOD * D))


TM_ROW = 256
N_CTX_TILES = R_CTX // TM_ROW


def _mod_spec(layer, chunk, tm):
    return pl.BlockSpec((None, None, 1, D), lambda i: (layer, _mod_row(i, tm), 0, chunk))


def _gain_spec(layer):
    return pl.BlockSpec((None, 1, D), lambda i: (layer, 0, 0))


_ROW = pl.BlockSpec((TM_ROW, D), lambda i: (i, 0))
_ROW_CTX = pl.BlockSpec((TM_ROW, D), lambda i: (jnp.minimum(i, N_CTX_TILES - 1), 0))
_ROW_LAT = pl.BlockSpec((TM_ROW, D), lambda i: (jnp.maximum(i - N_CTX_TILES, 0), 0))


def _norm_kernel(*refs, has_mm, split_in, with_next, split_out):
    it = iter(refs)
    if has_mm:
        mm_ref = next(it)
    x_refs = (next(it), next(it)) if split_in else (next(it),)
    if has_mm:
        gpost_ref, gt_ref = next(it), next(it)
    if with_next:
        gpre_ref, sc_ref, sh_ref = next(it), next(it), next(it)
    if has_mm:
        xo_refs = (next(it), next(it)) if split_out else (next(it),)
    if with_next:
        h_ref = next(it)

    def body(x_ref, xo_ref):
        x = x_ref[...]
        if has_mm:
            x = x + gt_ref[...] * _rms(mm_ref[...], gpost_ref[...])
            xo_ref[...] = x
        if with_next:
            h_ref[...] = (_rms(x, gpre_ref[...]) * (1.0 + sc_ref[...]) + sh_ref[...]).astype(BF16)

    if split_in or split_out:
        i = pl.program_id(0)

        @pl.when(i < N_CTX_TILES)
        def _():
            body(x_refs[0], xo_refs[0] if has_mm else None)

        @pl.when(i >= N_CTX_TILES)
        def _():
            body(x_refs[-1], xo_refs[-1] if has_mm else None)
    else:
        body(x_refs[0], xo_refs[0] if has_mm else None)


def _norm_call(x, mods4, mm=None, post=None, nxt=None, split_out=False):
    split_in = isinstance(x, tuple)
    in_specs, args = [], []
    if mm is not None:
        in_specs.append(_ROW)
        args.append(mm)
    in_specs += [_ROW_CTX, _ROW_LAT] if split_in else [_ROW]
    args += list(x) if split_in else [x]
    out_specs, out_shape = [], []
    if mm is not None:
        g_post, pl_layer, gate_chunk = post
        in_specs += [_gain_spec(pl_layer), _mod_spec(pl_layer, gate_chunk, TM_ROW)]
        args += [g_post.reshape(DEPTH, 1, D), mods4]
        if split_out:
            out_specs += [_ROW_CTX, _ROW_LAT]
            out_shape += [_sds((R_CTX, D), F32), _sds((R_LAT, D), F32)]
        else:
            out_specs.append(_ROW)
            out_shape.append(_sds((R, D), F32))
    if nxt is not None:
        g_pre, nl, sc_chunk, sh_chunk = nxt
        in_specs += [_gain_spec(nl), _mod_spec(nl, sc_chunk, TM_ROW), _mod_spec(nl, sh_chunk, TM_ROW)]
        args += [g_pre.reshape(DEPTH, 1, D), mods4, mods4]
        out_specs.append(_ROW)
        out_shape.append(_sds((R, D), BF16))
    return pl.pallas_call(
        functools.partial(_norm_kernel, has_mm=mm is not None, split_in=split_in, with_next=nxt is not None,
                          split_out=split_out),
        grid=(R // TM_ROW,),
        in_specs=in_specs, out_specs=out_specs, out_shape=out_shape,
        compiler_params=_params(1),
        name="post_norm" if mm is not None else "pre_norm",
    )(*args)


def _mm_kernel(*refs, nk, relu2, cast_weight):
    if cast_weight:
        a_ref, w_ref, cw_ref, o_ref, cwo_ref = refs
        n_in = cw_ref.shape[-1]
        cwo_ref[:, :n_in] = cw_ref[...].astype(BF16)
        if cwo_ref.shape[-1] > n_in:
            cwo_ref[:, n_in:] = jnp.zeros((cwo_ref.shape[0], cwo_ref.shape[-1] - n_in), BF16)
    else:
        a_ref, w_ref, o_ref = refs
    acc = jnp.dot(a_ref[...], w_ref[...], preferred_element_type=F32)
    if nk == 1:
        if relu2:
            u = jnp.maximum(acc, 0.0)
            acc = u * u
        o_ref[...] = acc.astype(o_ref.dtype)
    else:
        k = pl.program_id(2)

        @pl.when(k == 0)
        def _():
            o_ref[...] = acc

        @pl.when(k > 0)
        def _():
            o_ref[...] += acc


def _mm_call(a, w, out_dtype, relu2=False, name="proj", cast=None):
    m, kdim = a.shape
    n = w.shape[1]
    tm, tn, tk = 1024, 512, 4096
    nk = kdim // tk
    assert nk == 1 or (out_dtype == F32 and not relu2)
    nj = n // tn
    grid = (m // tm, nj, nk)
    in_specs = [pl.BlockSpec((tm, tk), lambda i, j, k: (i, k)),
                pl.BlockSpec((tk, tn), lambda i, j, k: (k, j))]
    args = [a, w]
    out_specs = [pl.BlockSpec((tm, tn), lambda i, j, k: (i, j))]
    out_shape = [_sds((m, n), out_dtype)]
    if cast is not None:
        cw, layer, width = cast
        rows, cols = cw.shape[1:]
        n_bands = min(grid[0] * nj * nk, rows // 16)
        while rows % n_bands or (rows // n_bands) % 16:
            n_bands -= 1
        rb = rows // n_bands

        def band(i, j, k):
            return jnp.minimum((i * nj + j) * nk + k, n_bands - 1)

        in_specs.append(pl.BlockSpec((None, rb, cols), lambda i, j, k: (layer, band(i, j, k), 0)))
        args.append(cw)
        out_specs.append(pl.BlockSpec((rb, width), lambda i, j, k: (band(i, j, k), 0)))
        out_shape.append(_sds((rows, width), BF16))
    res = pl.pallas_call(
        functools.partial(_mm_kernel, nk=nk, relu2=relu2, cast_weight=cast is not None),
        grid=grid,
        in_specs=in_specs, out_specs=out_specs, out_shape=out_shape,
        compiler_params=_params(3),
        name=name,
    )(*args)
    return res if cast is not None else res[0]


def _rope_tables(pair):
    t = np.arange(DEC_SEQ)
    row, col = t // GRID_W, t % GRID_W
    lane = np.arange(HD)
    da = 2 * pair
    inv_freq = 1.0 / (ROPE_BASE ** (np.arange(0, da, 2, dtype=np.float64) / da))
    pos = np.where((lane // da) % 2 == 0, row[:, None], col[:, None]).astype(np.float64)
    ang = pos * inv_freq[lane % pair][None, :]
    first = (lane % da) < pair
    cos = np.cos(ang)
    sin = np.sin(ang)
    sin_a = np.where(first[None, :], -sin, 0.0)
    sin_b = np.where(first[None, :], 0.0, sin)
    return tuple(jnp.asarray(a, F32) for a in (cos, sin_a, sin_b))


def _rope(x, cos, sin_a, sin_b, pair):
    return x * cos + pltpu.roll(x, HD - pair, 1) * sin_a + pltpu.roll(x, pair, 1) * sin_b


def _nt_dot(a, b):
    return lax.dot_general(a, b, (((1,), (1,)), ((), ())), preferred_element_type=F32)


def _exp2_rows(segs, sink_row=None):
    m = None
    for n, s in enumerate(segs):
        if n == 0 and sink_row is not None:
            mn = jnp.maximum(jnp.max(jnp.maximum(s[:, :HD], sink_row), axis=-1, keepdims=True),
                             jnp.max(s[:, HD:], axis=-1, keepdims=True))
        else:
            mn = jnp.max(s, axis=-1, keepdims=True)
        m = mn if m is None else jnp.maximum(m, mn)
    es = [jnp.exp2(s - m) for s in segs]
    den = sum(jnp.sum(e, axis=-1, keepdims=True) for e in es)
    if sink_row is not None:
        den = den + jnp.sum(jnp.exp2(sink_row - m), axis=-1, keepdims=True)
    return es, den


def _attn_kernel(*refs, kind, t_len, latent, n_kv, gq, qk_norm, use_sink, emit_kv, n_alias, lambda_init):
    it = iter(refs)
    q_ref, k_ref, v_ref = next(it), next(it), next(it)
    if latent:
        ck_ref, cv_ref, cos_ref, sa_ref, sb_ref = next(it), next(it), next(it), next(it), next(it)
    if kind == "diff":
        lq1_ref, lk1_ref, lq2_ref, lk2_ref, gsub_ref = next(it), next(it), next(it), next(it), next(it)
    if qk_norm:
        gq_ref, gk_ref = next(it), next(it)
    if use_sink:
        sink_ref = next(it)
    for _ in range(n_alias):
        next(it)
    o_ref = next(it)
    if emit_kv:
        ko_ref, vo_ref = next(it), next(it)
    kb_scr, vb_scr = next(it), next(it)

    past = PAST if latent else 0
    pair = HD // 8 if kind == "diff" else HD // 4
    dh = HD // 2 if kind == "diff" else HD
    q_scale = dh ** -0.5 * LOG2E

    for kv in range(n_kv):
        kl = slice(kv * HD, (kv + 1) * HD)
        k = k_ref[:, kl]
        v = v_ref[:, kl]
        if qk_norm:
            k = _rms(k, gk_ref[...])
        if emit_kv:
            ko_ref[:, kl] = k
            vo_ref[:, kl] = v
        if latent:
            k = _rope(k, cos_ref[...], sa_ref[...], sb_ref[...], pair)
            kb_scr[kv, 0:past, :] = ck_ref[:, kl].astype(BF16)
            vb_scr[kv, 0:past, :] = cv_ref[:, kl].astype(BF16)
        kb_scr[kv, past:past + t_len, :] = k.astype(BF16)
        vb_scr[kv, past:past + t_len, :] = v.astype(BF16)

    if kind == "diff":
        lam = (jnp.exp(jnp.sum(lq1_ref[...] * lk1_ref[...], axis=-1, keepdims=True))
               - jnp.exp(jnp.sum(lq2_ref[...] * lk2_ref[...], axis=-1, keepdims=True)) + lambda_init)
        first_map = lax.broadcasted_iota(jnp.int32, (1, HD), 1) < HD // 2

    def unit(qb, kv, g):
        rows = pl.ds(qb * QB, QB) if isinstance(qb, int) else pl.ds(pl.multiple_of(qb * QB, QB), QB)
        lanes = slice((kv * gq + g) * HD, (kv * gq + g + 1) * HD)
        q = q_ref[rows, lanes]
        if qk_norm:
            q = _rms(q, gq_ref[...])
        if latent:
            q = _rope(q, cos_ref[rows, :], sa_ref[rows, :], sb_ref[rows, :], pair)
        q = q * q_scale
        if kind == "diff":
            kb, vb = kb_scr[kv], vb_scr[kv]
            (e1,), d1 = _exp2_rows([_nt_dot(jnp.where(first_map, q, 0.0).astype(BF16), kb)])
            (e2,), d2 = _exp2_rows([_nt_dot(jnp.where(first_map, 0.0, q).astype(BF16), kb)])
            o = (jnp.dot(e1.astype(BF16), vb, preferred_element_type=F32) * (1.0 / d1)
                 - jnp.dot(e2.astype(BF16), vb, preferred_element_type=F32) * (lam / d2))
            o = _rms(o, gsub_ref[...]) * (1.0 - lambda_init)
        elif kind == "gqa":
            sink = sink_ref[kv * gq + g] * LOG2E if use_sink else None
            (e,), den = _exp2_rows([_nt_dot(q.astype(BF16), kb_scr[kv])], sink)
            o = jnp.dot(e.astype(BF16), vb_scr[kv], preferred_element_type=F32) * (1.0 / den)
        else:
            sink = sink_ref[kv * gq + g] * LOG2E
            w_len = 3 * QB
            start = jnp.clip((qb - 1) * QB, 0, t_len - w_len)
            loc = pl.ds(pl.multiple_of(past + start, QB), w_len)
            qh = q.astype(BF16)
            s_ctx = _nt_dot(qh, kb_scr[kv, 0:past, :])
            s_loc = _nt_dot(qh, kb_scr[kv, loc, :])
            qpos = qb * QB + lax.broadcasted_iota(jnp.int32, (QB, 1), 0)
            kpos = start + lax.broadcasted_iota(jnp.int32, (1, w_len), 1)
            s_loc = jnp.where(jnp.abs(qpos - kpos) <= WINDOW, s_loc, -1e30)
            (e_ctx, e_loc), den = _exp2_rows([s_ctx, s_loc], sink)
            o = (jnp.dot(e_ctx.astype(BF16), vb_scr[kv, 0:past, :], preferred_element_type=F32)
                 + jnp.dot(e_loc.astype(BF16), vb_scr[kv, loc, :], preferred_element_type=F32)) * (1.0 / den)
        o_ref[rows, lanes] = o.astype(BF16)

    n_blk = t_len // QB
    per_iter = min(n_blk, 4)
    for kv in range(n_kv):
        for g in range(gq):
            if n_blk == per_iter:
                for qb in range(n_blk):
                    unit(qb, kv, g)
            else:
                def some_blocks(i, carry, kv=kv, g=g):
                    for n in range(per_iter):
                        unit(per_iter * i + n, kv, g)
                    return carry

                lax.fori_loop(0, n_blk // per_iter, some_blocks, 0)


def _attn_call(kind, proj, mix, layer, latent, cols, mix_col, n_kv, caches=None, tabs=None, lam_vecs=None,
               g_subln=None, qk_gains=None, sink=None, kv_out=None):
    t_len, nb, rb0 = (DEC_SEQ, DEC_BATCH, R_CTX // DEC_SEQ) if latent else (SEQ, BATCH, 0)
    gq = 1 if kind == "diff" else GQ
    kv_total = N_HEADS // gq
    c_q, c_k, c_v = cols
    qw, kw = n_kv * gq * HD, n_kv * HD
    grid = (nb, kv_total // n_kv)

    def tok(width, c0):
        return pl.BlockSpec((t_len, width), lambda b, h: (rb0 + b, c0 // width + h))

    in_specs = [tok(qw, c_q), tok(kw, c_k), tok(kw, c_v)]
    args = [proj, proj, proj]
    if latent:
        cache = pl.BlockSpec((None, None, PAST, kw), lambda b, h: (b, layer, 0, h))
        tab = pl.BlockSpec((t_len, HD), lambda b, h: (0, 0))
        in_specs += [cache, cache, tab, tab, tab]
        args += [c.reshape(DEC_BATCH, DEPTH, PAST, kv_total * HD) for c in caches] + list(tabs)
    if kind == "diff":
        vec = pl.BlockSpec((None, 1, HD // 2), lambda b, h: (layer, 0, 0))
        in_specs += [vec] * 4 + [pl.BlockSpec((None, 1, HD), lambda b, h: (layer, 0, 0))]
        args += [v.reshape(DEPTH, 1, HD // 2) for v in lam_vecs] + [g_subln.reshape(DEPTH, 1, HD)]
    if qk_gains is not None:
        gain = pl.BlockSpec((None, 1, HD), lambda b, h: (layer, 0, 0))
        in_specs += [gain, gain]
        args += [g.reshape(DEPTH, 1, HD) for g in qk_gains]
    if sink is not None:
        in_specs.append(pl.BlockSpec((None, n_kv * GQ, 1, HD), lambda b, h: (layer, h, 0, 0)))
        args.append(jnp.pad(sink.reshape(DEPTH, N_HEADS, 1, 1), ((0, 0), (0, 0), (0, 0), (0, HD - 1)),
                            constant_values=-1e30))
    aliases = {}
    if mix is not None:
        aliases[len(args)] = 0
        in_specs.append(pl.BlockSpec(memory_space=pl.ANY))
        args.append(mix)
    out_specs = [tok(qw, mix_col)]
    out_shape = [_sds((R, D), BF16)]
    emit_kv = kv_out is not None
    if emit_kv:
        for n, prev in enumerate(kv_out):
            aliases[len(args)] = 1 + n
            in_specs.append(pl.BlockSpec(memory_space=pl.ANY))
            args.append(prev)
        kv_spec = pl.BlockSpec((None, None, SEQ, kw), lambda b, h: (b, layer, 0, h))
        out_specs += [kv_spec, kv_spec]
        out_shape += [_sds((BATCH, DEPTH, SEQ, kv_total * HD), F32)] * 2
    s_len = t_len + (PAST if latent else 0)
    return pl.pallas_call(
        functools.partial(_attn_kernel, kind=kind, t_len=t_len, latent=latent, n_kv=n_kv, gq=gq,
                          qk_norm=qk_gains is not None, use_sink=sink is not None, emit_kv=emit_kv,
                          n_alias=len(aliases), lambda_init=0.8 - 0.6 * math.exp(-0.3 * layer)),
        grid=grid,
        in_specs=in_specs, out_specs=out_specs, out_shape=out_shape,
        scratch_shapes=[pltpu.VMEM((n_kv, s_len, HD), BF16), pltpu.VMEM((n_kv, s_len, HD), BF16)],
        input_output_aliases=aliases,
        compiler_params=_params(2),
        name=kind + ("_lat" if latent else "_ctx"),
    )(*args)


def _conv_kernel(x_ref, w_ref, b_ref, o_ref, *, t_len):
    u = x_ref[...]
    w = w_ref[...]
    t = lax.broadcasted_iota(jnp.int32, (t_len, 1), 0)
    half = D_CONV // 2
    acc = u * w[half:half + 1, :] + b_ref[...]
    for s in range(1, half + 1):
        before = jnp.where(t >= s, pltpu.roll(u, s, 0), 0.0)
        after = jnp.where(t < t_len - s, pltpu.roll(u, t_len - s, 0), 0.0)
        acc = acc + before * w[half - s:half - s + 1, :] + after * w[half + s:half + s + 1, :]
    o_ref[...] = acc * _sigmoid(acc)


def _conv_call(proj, conv_w, conv_b, layer, latent):
    t_len, nb, rb0 = (DEC_SEQ, DEC_BATCH, R_CTX // DEC_SEQ) if latent else (SEQ, BATCH, 0)
    tc = 512
    return pl.pallas_call(
        functools.partial(_conv_kernel, t_len=t_len),
        grid=(nb, CONV_DIM // tc),
        in_specs=[pl.BlockSpec((t_len, tc), lambda b, j: (rb0 + b, C_XBC // tc + j)),
                  pl.BlockSpec((None, D_CONV, tc), lambda b, j: (layer, 0, j)),
                  pl.BlockSpec((None, 1, tc), lambda b, j: (layer, 0, j))],
        out_specs=pl.BlockSpec((t_len, tc), lambda b, j: (b, j)),
        out_shape=_sds((nb * t_len, CONV_DIM), F32),
        compiler_params=_params(2),
        name="conv_lat" if latent else "conv_ctx",
    )(proj, conv_w, conv_b.reshape(DEPTH, 1, CONV_DIM))


def _split3(v):
    hi = v.astype(BF16).astype(F32)
    r1 = v - hi
    mid = r1.astype(BF16).astype(F32)
    lo = (r1 - mid).astype(BF16).astype(F32)
    return hi, mid, lo


def _exact_dot(tri, v, tri_first):
    parts = [p.astype(BF16) for p in _split3(v)]
    if tri_first:
        return sum(jnp.dot(tri, p, preferred_element_type=F32) for p in parts)
    return sum(jnp.dot(p, tri, preferred_element_type=F32) for p in parts)


def _lane_spread(v, width):
    hi, mid, lo = _split3(v)
    packed = (hi + pltpu.roll(mid, SSM_H, 1) + pltpu.roll(lo, 2 * SSM_H, 1)).astype(BF16)
    n = SSM_H * width
    src = lax.broadcasted_iota(jnp.int32, (HD, n), 0)
    dst = lax.broadcasted_iota(jnp.int32, (HD, n), 1)
    sel = jnp.where((src < 3 * SSM_H) & (src % SSM_H == dst // width), 1.0, 0.0).astype(BF16)
    return jnp.dot(packed, sel, preferred_element_type=F32)


def _softplus(x):
    return jnp.maximum(x, 0.0) + jnp.log1p(jnp.exp(-jnp.abs(x)))


def _scan_kernel(*refs, direction, has_h0, emit_state, n_chunks, n_alias):
    it = iter(refs)
    xa_ref, dt_ref, dtt_ref, bias_r_ref, bias_c_ref, alog_r_ref, alog_c_ref = (next(it) for _ in range(7))
    if has_h0:
        h0_ref = next(it)
    for _ in range(n_alias):
        next(it)
    y_ref = next(it)
    if emit_state:
        hend_ref = next(it)
    ht_scr = next(it)
    c = pl.program_id(1)
    hs = slice(direction * SSM_H, (direction + 1) * SSM_H)

    @pl.when(c == 0)
    def _():
        if has_h0:
            ht_scr[...] = h0_ref[...].reshape(SSM_INNER, SSM_N).T
        else:
            ht_scr[...] = jnp.zeros(ht_scr.shape, F32)

    ii = lax.broadcasted_iota(jnp.int32, (CHUNK, CHUNK), 0)
    jj = lax.broadcasted_iota(jnp.int32, (CHUNK, CHUNK), 1)
    if direction == 0:
        keep, keep_t = jj <= ii, ii <= jj
        edge = CHUNK - 1
    else:
        keep, keep_t = jj >= ii, ii >= jj
        edge = 0
    tri_col = jnp.where(keep, 1.0, 0.0).astype(BF16)
    tri_row = jnp.where(keep_t, 1.0, 0.0).astype(BF16)

    raw = dt_ref[...]
    if direction == 1:
        raw = pltpu.roll(raw, HD - SSM_H, 1)
    head_lane = lax.broadcasted_iota(jnp.int32, (1, HD), 1) < SSM_H
    dt_c = jnp.where(head_lane, _softplus(raw + bias_r_ref[...]), 0.0)
    a_col = _exact_dot(tri_col, dt_c * (-jnp.exp(alog_r_ref[...])), True)
    dt_r = _softplus(dtt_ref[hs, :] + bias_c_ref[hs, :])
    a_row = _exact_dot(tri_row, dt_r * (-jnp.exp(alog_c_ref[hs, :])), False)

    a_sq = _lane_spread(a_col, CHUNK)
    a_hp = _lane_spread(a_col, SSM_P)
    dt_hp = _lane_spread(dt_c, SSM_P)

    xa = xa_ref[...]
    xdt = xa[:, :SSM_INNER] * dt_hp
    a_tot = a_hp[edge:edge + 1, :]
    grow = jnp.exp(a_hp)
    w_end = (xdt * jnp.exp(a_tot - a_hp)).astype(BF16)
    carry = jnp.exp(a_tot)
    lower_half = lax.broadcasted_iota(jnp.int32, (1, 2 * SSM_P), 1) < SSM_P

    hg = SSM_H // SSM_G
    gw = hg * SSM_P
    for g in range(SSM_G):
        bm = xa[:, SSM_INNER + g * SSM_N:SSM_INNER + (g + 1) * SSM_N].astype(BF16)
        cm = xa[:, SSM_INNER + (SSM_G + g) * SSM_N:SSM_INNER + (SSM_G + g + 1) * SSM_N].astype(BF16)
        cb = _nt_dot(cm, bm)
        gl = slice(g * gw, (g + 1) * gw)
        ht = ht_scr[:, gl]
        y_off = jnp.dot(cm, ht.astype(BF16), preferred_element_type=F32) * grow[:, gl]
        for pr in range(hg // 2):
            pl_ = slice(g * gw + pr * 2 * SSM_P, g * gw + (pr + 1) * 2 * SSM_P)
            x_pair = xdt[:, pl_]
            y_pair = y_off[:, pr * 2 * SSM_P:(pr + 1) * 2 * SSM_P]
            for half in range(2):
                h = g * hg + pr * 2 + half
                decay = jnp.exp(jnp.where(keep, a_sq[:, h * CHUNK:(h + 1) * CHUNK] - a_row[h:h + 1, :], -1e30))
                rhs = jnp.where(lower_half if half == 0 else ~lower_half, x_pair, 0.0).astype(BF16)
                y_pair = y_pair + jnp.dot((cb * decay).astype(BF16), rhs, preferred_element_type=F32)
            y_ref[:, pl_] = y_pair
        st = lax.dot_general(bm, w_end[:, gl], (((0,), (0,)), ((), ())), preferred_element_type=F32)
        ht_scr[:, gl] = ht * carry[:, gl] + st

    if emit_state:
        @pl.when(c == n_chunks - 1)
        def _():
            hend_ref[...] = ht_scr[...].T.reshape(SSM_H, SSM_P, SSM_N)


def _pad_lanes(v):
    return jnp.pad(v, ((0, 0), (0, 0), (0, HD - SSM_H))).reshape(DEPTH, 2, 1, HD)


def _scan_call(xact, proj, dtt, dt_bias, a_log, state, layer, latent, direction, state_out=None):
    t_len, nb, row0 = (DEC_SEQ, DEC_BATCH, R_CTX) if latent else (SEQ, BATCH, 0)
    nc = t_len // CHUNK
    rb0 = row0 // CHUNK

    def chunk(c):
        return c if direction == 0 else nc - 1 - c

    row_vec = pl.BlockSpec((None, None, 1, HD), lambda b, c: (layer, direction, 0, 0))
    col_vec = pl.BlockSpec((None, 2 * SSM_H, 1), lambda b, c: (layer, 0, 0))
    in_specs = [pl.BlockSpec((CHUNK, CONV_DIM), lambda b, c: (b * nc + chunk(c), 0)),
                pl.BlockSpec((CHUNK, HD), lambda b, c: (rb0 + b * nc + chunk(c), C_DT // HD)),
                pl.BlockSpec((2 * SSM_H, CHUNK), lambda b, c: (0, rb0 + b * nc + chunk(c))),
                row_vec, col_vec, row_vec, col_vec]
    args = [xact, proj, dtt,
            _pad_lanes(dt_bias), dt_bias.reshape(DEPTH, 2 * SSM_H, 1),
            _pad_lanes(a_log), a_log.reshape(DEPTH, 2 * SSM_H, 1)]
    if latent:
        in_specs.append(pl.BlockSpec((None, None, None, SSM_H, SSM_P, SSM_N),
                                     lambda b, c: (b, layer, direction, 0, 0, 0)))
        args.append(state)
    out_specs = [pl.BlockSpec((CHUNK, SSM_INNER), lambda b, c: (b * nc + chunk(c), 0))]
    out_shape = [_sds((nb * t_len, SSM_INNER), F32)]
    aliases = {}
    emit_state = state_out is not None
    if emit_state:
        for prev in state_out:
            aliases[len(args)] = 1
            in_specs.append(pl.BlockSpec(memory_space=pl.ANY))
            args.append(prev)
        out_specs.append(pl.BlockSpec((None, None, None, SSM_H, SSM_P, SSM_N),
                                      lambda b, c: (b, layer, direction, 0, 0, 0)))
        out_shape.append(_sds((nb, DEPTH, 2, SSM_H, SSM_P, SSM_N), F32))
    return pl.pallas_call(
        functools.partial(_scan_kernel, direction=direction, has_h0=latent, emit_state=emit_state, n_chunks=nc,
                          n_alias=len(aliases)),
        grid=(nb, nc),
        in_specs=in_specs, out_specs=out_specs, out_shape=out_shape,
        scratch_shapes=[pltpu.VMEM((SSM_N, SSM_INNER), F32)],
        input_output_aliases=aliases,
        compiler_params=_params(2),
        name=f"ssd_{'lat' if latent else 'ctx'}_{'fwd' if direction == 0 else 'bwd'}",
    )(*args)


def _ssd_out_kernel(yf_ref, yb_ref, xs_ref, z_ref, dskip_ref, g_ref, mix_ref, o_ref):
    del mix_ref
    dsk = dskip_ref[0:1, :] + dskip_ref[1:2, :]
    z = z_ref[...]
    y = (yf_ref[...] + yb_ref[...] + dsk * xs_ref[...]) * (z * _sigmoid(z))
    gw = SSM_INNER // SSM_G
    g = g_ref[...]
    for k in range(SSM_G):
        lanes = slice(k * gw, (k + 1) * gw)
        yk = y[:, lanes]
        yk = yk * lax.rsqrt(jnp.mean(yk * yk, axis=-1, keepdims=True) + EPS)
        o_ref[:, lanes] = (yk * g[:, lanes]).astype(BF16)


def _ssd_out_call(yf, yb, xact, proj, d_skip, g_norm, mix, layer, latent):
    tm = 256
    rb0 = (R_CTX if latent else 0) // tm
    n_rows = yf.shape[0]
    local = pl.BlockSpec((tm, SSM_INNER), lambda i: (i, 0))
    return pl.pallas_call(
        _ssd_out_kernel,
        grid=(n_rows // tm,),
        in_specs=[local, local, local,
                  pl.BlockSpec((tm, SSM_INNER), lambda i: (rb0 + i, C_Z // SSM_INNER)),
                  pl.BlockSpec((None, 2, SSM_INNER), lambda i: (layer, 0, 0)),
                  pl.BlockSpec((None, 1, SSM_INNER), lambda i: (layer, 0, 0)),
                  pl.BlockSpec(memory_space=pl.ANY)],
        out_specs=pl.BlockSpec((tm, SSM_INNER), lambda i: (rb0 + i, 3)),
        out_shape=_sds((R, D), BF16),
        input_output_aliases={6: 0},
        compiler_params=_params(1),
        name="ssd_out_lat" if latent else "ssd_out_ctx",
    )(yf, yb, xact, proj, jnp.repeat(d_skip, SSM_P, axis=-1), g_norm.reshape(DEPTH, 1, SSM_INNER), mix)


def kernel(x_prompt, x_sample, c, cache_a_k, cache_a_v, cache_b_k, cache_b_v, cache_c_k, cache_c_v, state_ssm, c_ctx, w_mod, b_mod, g_pre_mix, g_post_mix, g_pre_ffn, g_post_ffn, w_in, w_out, lam_q1, lam_k1, lam_q2, lam_k2, g_subln, g_qnorm, g_knorm, sink, conv_w, conv_b, dt_bias, a_log, d_skip, g_ssm_norm, w_up, w_down):
    x = (x_prompt.reshape(R_CTX, D), x_sample.reshape(R_LAT, D))
    cvec = jnp.concatenate([c_ctx[None, :], c, jnp.zeros((8 - 1 - DEC_BATCH, D), F32)], axis=0)
    mods4 = _mod_call(cvec, w_mod, b_mod).reshape(DEPTH, 8, 1, N_MOD * D)

    w_in_b = jnp.pad(w_in[0], ((0, 0), (0, IN_WP - IN_W))).astype(BF16)
    w_out_b = w_out[0].astype(BF16)
    tabs_a = _rope_tables(HD // 8)
    tabs_bc = _rope_tables(HD // 4)
    lam_vecs = (lam_q1, lam_k1, lam_q2, lam_k2)

    (h,) = _norm_call(x, mods4, nxt=(g_pre_mix, 0, 1, 0))
    kv_a, kv_b, kv_c, ssm_new = (), (), (), ()
    for layer in range(DEPTH):
        last = layer + 1 == DEPTH
        proj, w_up_b = _mm_call(h, w_in_b, F32, name="in_proj", cast=(w_up, layer, D_FF))

        mix, *kv_a = _attn_call("diff", proj, None, layer, False, (C_QA, C_KA, C_VA), 0, N_HEADS,
                                lam_vecs=lam_vecs, g_subln=g_subln, kv_out=kv_a)
        mix, *kv_b = _attn_call("gqa", proj, mix, layer, False, (C_QB, C_KB, C_VB), 1024, N_KV,
                                qk_gains=(g_qnorm, g_knorm), kv_out=kv_b)
        mix, *kv_c = _attn_call("gqa", proj, mix, layer, False, (C_QC, C_KC, C_VC), 2048, 1,
                                sink=sink, kv_out=kv_c)
        mix, = _attn_call("diff", proj, mix, layer, True, (C_QA, C_KA, C_VA), 0, 1,
                          caches=(cache_a_k, cache_a_v), tabs=tabs_a, lam_vecs=lam_vecs, g_subln=g_subln)
        mix, = _attn_call("gqa", proj, mix, layer, True, (C_QB, C_KB, C_VB), 1024, 1,
                          caches=(cache_b_k, cache_b_v), tabs=tabs_bc, qk_gains=(g_qnorm, g_knorm))
        mix, = _attn_call("win", proj, mix, layer, True, (C_QC, C_KC, C_VC), 2048, 1,
                          caches=(cache_c_k, cache_c_v), tabs=tabs_bc, sink=sink)

        dtt = proj[:, C_DT:C_DT + 2 * SSM_H].T
        for latent in (False, True):
            xact = _conv_call(proj, conv_w, conv_b, layer, latent)
            ys = []
            for direction in (0, 1):
                res = _scan_call(xact, proj, dtt, dt_bias, a_log, state_ssm, layer, latent, direction,
                                 state_out=None if latent else ssm_new)
                ys.append(res[0])
                if not latent:
                    ssm_new = (res[1],)
            mix = _ssd_out_call(ys[0], ys[1], xact, proj, d_skip, g_ssm_norm, mix, layer, latent)

        if last:
            mm = _mm_call(mix, w_out_b, F32, name="out_proj")
        else:
            mm, w_out_b = _mm_call(mix, w_out_b, F32, name="out_proj", cast=(w_out, layer + 1, D))
        x, h2 = _norm_call(x, mods4, mm=mm, post=(g_post_mix, layer, 2), nxt=(g_pre_ffn, layer, 4, 3))
        u, w_down_b = _mm_call(h2, w_up_b, BF16, relu2=True, name="ffn_up", cast=(w_down, layer, D))
        if last:
            mm = _mm_call(u, w_down_b, F32, name="ffn_down")
            x_ctx, x_lat = _norm_call(x, mods4, mm=mm, post=(g_post_ffn, layer, 5), split_out=True)
        else:
            mm, w_in_b = _mm_call(u, w_down_b, F32, name="ffn_down", cast=(w_in, layer + 1, IN_WP))
            x, h = _norm_call(x, mods4, mm=mm, post=(g_post_ffn, layer, 5), nxt=(g_pre_mix, layer + 1, 1, 0))

    new_kv = [a.reshape(BATCH, DEPTH, SEQ, -1, HD) for a in (*kv_a, *kv_b, *kv_c)]
    return (x_ctx.reshape(BATCH, SEQ, D), x_lat.reshape(DEC_BATCH, DEC_SEQ, D), *new_kv, ssm_new[0])
```

```python
import functools
import math

import numpy as np
import jax
import jax.numpy as jnp
from jax import lax
from jax.experimental import pallas as pl
from jax.experimental.pallas import tpu as pltpu

F32 = jnp.float32
BF16 = jnp.bfloat16

D = 4096
BATCH, SEQ = 16, 256
DEC_BATCH, DEC_SEQ = 4, 1024
PAST = 256
DEPTH = 2
GRID_W = 64
HD = 128
N_HEADS = 8
N_KV = 2
GQ = N_HEADS // N_KV
WINDOW = 128
QB = 128
SSM_H, SSM_P, SSM_N, SSM_G = 16, 64, 128, 2
SSM_INNER = SSM_H * SSM_P
CHUNK = 128
D_CONV = 5
CONV_DIM = SSM_INNER + 2 * SSM_G * SSM_N
D_FF = 4 * D
EPS = 1e-6
ROPE_BASE = 10000.0
N_MOD = 6
LOG2E = 1.4426950408889634

R_CTX = BATCH * SEQ
R_LAT = DEC_BATCH * DEC_SEQ
R = R_CTX + R_LAT

C_QA, C_KA, C_VA = 0, 1024, 2048
C_QB, C_KB, C_VB = 3072, 4096, 4352
C_QC, C_KC, C_VC = 4608, 5632, 5888
C_Z, C_XBC, C_DT = 6144, 7168, 8704
IN_W = C_DT + 2 * SSM_H
IN_WP = 9216

VMEM_LIMIT = 56 * 1024 * 1024


def _sds(shape, dtype):
    return jax.ShapeDtypeStruct(shape, dtype)


def _params(n_grid, vmem=VMEM_LIMIT):
    return pltpu.CompilerParams(dimension_semantics=("arbitrary",) * n_grid, vmem_limit_bytes=vmem)


def _rms(x, g):
    return x * lax.rsqrt(jnp.mean(x * x, axis=-1, keepdims=True) + EPS) * g


def _sigmoid(x):
    return 1.0 / (1.0 + jnp.exp(-x))


def _mod_row(i, tm):
    r = i * tm
    return jnp.where(r >= R_CTX, 1 + (r - R_CTX) // DEC_SEQ, 0)


def _mod_kernel(c_ref, w_ref, b_ref, o_ref):
    c = c_ref[...]
    s = (c * _sigmoid(c)).astype(BF16)
    o_ref[...] = jnp.dot(s, w_ref[...].astype(BF16), preferred_element_type=F32) + b_ref[...]


def _mod_call(cvec, w_mod, b_mod):
    tn = 512
    return pl.pallas_call(
        _mod_kernel,
        grid=(DEPTH, N_MOD * D // tn),
        in_specs=[pl.BlockSpec((8, D), lambda l, j: (0, 0)),
                  pl.BlockSpec((None, D, tn), lambda l, j: (l, 0, j)),
                  pl.BlockSpec((None, 1, tn), lambda l, j: (l, 0, j))],
        out_specs=pl.BlockSpec((None, 8, tn), lambda l, j: (l, 0, j)),
        out_shape=_sds((DEPTH, 8, N_MOD * D), F32),
        compiler_params=_params(2),
        name="modulation",
    )(cvec, w_mod, b_mod.reshape(DEPTH, 1, N_MOD * D))


def _cast_band(cw_ref, cwo_ref):
    n_in = cw_ref.shape[-1]
    cwo_ref[:, :n_in] = cw_ref[...].astype(BF16)
    if cwo_ref.shape[-1] > n_in:
        cwo_ref[:, n_in:] = jnp.zeros((cwo_ref.shape[0], cwo_ref.shape[-1] - n_in), BF16)


def _cast_specs(cast, n_steps, step_of):
    cw, layer, width = cast
    rows, cols = cw.shape[1:]
    n_bands = min(n_steps, rows // 16)
    while rows % n_bands or (rows // n_bands) % 16:
        n_bands -= 1
    rb = rows // n_bands

    def band(*idx):
        return jnp.minimum(step_of(*idx), n_bands - 1)

    return (pl.BlockSpec((None, rb, cols), lambda *idx: (layer, band(*idx), 0)),
            pl.BlockSpec((rb, width), lambda *idx: (band(*idx), 0)),
            _sds((rows, width), BF16))


TM_ROW = 256
N_CTX_TILES = R_CTX // TM_ROW


def _mod_spec(layer, chunk, tm):
    return pl.BlockSpec((None, None, 1, D), lambda i: (layer, _mod_row(i, tm), 0, chunk))


def _gain_spec(layer):
    return pl.BlockSpec((None, 1, D), lambda i: (layer, 0, 0))


_ROW = pl.BlockSpec((TM_ROW, D), lambda i: (i, 0))
_ROW_CTX = pl.BlockSpec((TM_ROW, D), lambda i: (jnp.minimum(i, N_CTX_TILES - 1), 0))
_ROW_LAT = pl.BlockSpec((TM_ROW, D), lambda i: (jnp.maximum(i - N_CTX_TILES, 0), 0))


def _norm_kernel(*refs, has_mm, split_in, with_next, split_out, n_cast):
    it = iter(refs)
    if has_mm:
        mm_ref = next(it)
    x_refs = (next(it), next(it)) if split_in else (next(it),)
    if has_mm:
        gpost_ref, gt_ref = next(it), next(it)
    if with_next:
        gpre_ref, sc_ref, sh_ref = next(it), next(it), next(it)
    cw_refs = [next(it) for _ in range(n_cast)]
    if has_mm:
        xo_refs = (next(it), next(it)) if split_out else (next(it),)
    if with_next:
        h_ref = next(it)
    for cw_ref in cw_refs:
        _cast_band(cw_ref, next(it))

    def body(x_ref, xo_ref):
        x = x_ref[...]
        if has_mm:
            x = x + gt_ref[...] * _rms(mm_ref[...], gpost_ref[...])
            xo_ref[...] = x
        if with_next:
            h_ref[...] = (_rms(x, gpre_ref[...]) * (1.0 + sc_ref[...]) + sh_ref[...]).astype(BF16)

    if split_in or split_out:
        i = pl.program_id(0)

        @pl.when(i < N_CTX_TILES)
        def _():
            body(x_refs[0], xo_refs[0] if has_mm else None)

        @pl.when(i >= N_CTX_TILES)
        def _():
            body(x_refs[-1], xo_refs[-1] if has_mm else None)
    else:
        body(x_refs[0], xo_refs[0] if has_mm else None)


def _norm_call(x, mods4, mm=None, post=None, nxt=None, split_out=False, casts=()):
    split_in = isinstance(x, tuple)
    in_specs, args = [], []
    if mm is not None:
        in_specs.append(_ROW)
        args.append(mm)
    in_specs += [_ROW_CTX, _ROW_LAT] if split_in else [_ROW]
    args += list(x) if split_in else [x]
    out_specs, out_shape = [], []
    if mm is not None:
        g_post, pl_layer, gate_chunk = post
        in_specs += [_gain_spec(pl_layer), _mod_spec(pl_layer, gate_chunk, TM_ROW)]
        args += [g_post.reshape(DEPTH, 1, D), mods4]
        if split_out:
            out_specs += [_ROW_CTX, _ROW_LAT]
            out_shape += [_sds((R_CTX, D), F32), _sds((R_LAT, D), F32)]
        else:
            out_specs.append(_ROW)
            out_shape.append(_sds((R, D), F32))
    if nxt is not None:
        g_pre, nl, sc_chunk, sh_chunk = nxt
        in_specs += [_gain_spec(nl), _mod_spec(nl, sc_chunk, TM_ROW), _mod_spec(nl, sh_chunk, TM_ROW)]
        args += [g_pre.reshape(DEPTH, 1, D), mods4, mods4]
        out_specs.append(_ROW)
        out_shape.append(_sds((R, D), BF16))
    for cast in casts:
        c_in, c_out, c_shape = _cast_specs(cast, R // TM_ROW, lambda i: i)
        in_specs.append(c_in)
        args.append(cast[0])
        out_specs.append(c_out)
        out_shape.append(c_shape)
    return pl.pallas_call(
        functools.partial(_norm_kernel, has_mm=mm is not None, split_in=split_in, with_next=nxt is not None,
                          split_out=split_out, n_cast=len(casts)),
        grid=(R // TM_ROW,),
        in_specs=in_specs, out_specs=out_specs, out_shape=out_shape,
        compiler_params=_params(1),
        name="post_norm" if mm is not None else "pre_norm",
    )(*args)


def _mm_kernel(*refs, nk, relu2, cast_weight):
    if cast_weight:
        a_ref, w_ref, cw_ref, o_ref, cwo_ref = refs
        _cast_band(cw_ref, cwo_ref)
    else:
        a_ref, w_ref, o_ref = refs
    acc = jnp.dot(a_ref[...], w_ref[...], preferred_element_type=F32)
    if nk == 1:
        if relu2:
            u = jnp.maximum(acc, 0.0)
            acc = u * u
        o_ref[...] = acc.astype(o_ref.dtype)
    else:
        k = pl.program_id(2)

        @pl.when(k == 0)
        def _():
            o_ref[...] = acc

        @pl.when(k > 0)
        def _():
            o_ref[...] += acc


CAST_BAND_BYTES = 2 * 1024 * 1024


def _mm_tiles(m, n, kdim, cast):
    tm, tn, tk = 1024, 1024, 4096
    if cast is not None:
        steps = (m // tm) * (n // tn) * (kdim // tk)
        if cast[0].shape[1] * cast[0].shape[2] * 4 // steps > CAST_BAND_BYTES:
            tn //= 2
    return tm, tn, tk


def _mm_call(a, w, out_dtype, relu2=False, name="proj", cast=None):
    m, kdim = a.shape
    n = w.shape[1]
    tm, tn, tk = _mm_tiles(m, n, kdim, cast)
    nk = kdim // tk
    assert nk == 1 or (out_dtype == F32 and not relu2)
    nj = n // tn
    grid = (m // tm, nj, nk)
    in_specs = [pl.BlockSpec((tm, tk), lambda i, j, k: (i, k)),
                pl.BlockSpec((tk, tn), lambda i, j, k: (k, j))]
    args = [a, w]
    out_specs = [pl.BlockSpec((tm, tn), lambda i, j, k: (i, j))]
    out_shape = [_sds((m, n), out_dtype)]
    if cast is not None:
        c_in, c_out, c_shape = _cast_specs(cast, grid[0] * nj * nk, lambda i, j, k: (i * nj + j) * nk + k)
        in_specs.append(c_in)
        args.append(cast[0])
        out_specs.append(c_out)
        out_shape.append(c_shape)
    res = pl.pallas_call(
        functools.partial(_mm_kernel, nk=nk, relu2=relu2, cast_weight=cast is not None),
        grid=grid,
        in_specs=in_specs, out_specs=out_specs, out_shape=out_shape,
        compiler_params=_params(3),
        name=name,
    )(*args)
    return res if cast is not None else res[0]


def _rope_tables(pair):
    t = np.arange(DEC_SEQ)
    row, col = t // GRID_W, t % GRID_W
    lane = np.arange(HD)
    da = 2 * pair
    inv_freq = 1.0 / (ROPE_BASE ** (np.arange(0, da, 2, dtype=np.float64) / da))
    pos = np.where((lane // da) % 2 == 0, row[:, None], col[:, None]).astype(np.float64)
    ang = pos * inv_freq[lane % pair][None, :]
    first = (lane % da) < pair
    cos = np.cos(ang)
    sin = np.sin(ang)
    sin_a = np.where(first[None, :], -sin, 0.0)
    sin_b = np.where(first[None, :], 0.0, sin)
    return tuple(jnp.asarray(a, F32) for a in (cos, sin_a, sin_b))


def _rope(x, cos, sin_a, sin_b, pair):
    return x * cos + pltpu.roll(x, HD - pair, 1) * sin_a + pltpu.roll(x, pair, 1) * sin_b


def _nt_dot(a, b):
    return lax.dot_general(a, b, (((1,), (1,)), ((), ())), preferred_element_type=F32)


def _exp2_rows(segs, sink_row=None):
    m = None
    for n, s in enumerate(segs):
        if n == 0 and sink_row is not None:
            mn = jnp.maximum(jnp.max(jnp.maximum(s[:, :HD], sink_row), axis=-1, keepdims=True),
                             jnp.max(s[:, HD:], axis=-1, keepdims=True))
        else:
            mn = jnp.max(s, axis=-1, keepdims=True)
        m = mn if m is None else jnp.maximum(m, mn)
    es = [jnp.exp2(s - m) for s in segs]
    den = sum(jnp.sum(e, axis=-1, keepdims=True) for e in es)
    if sink_row is not None:
        den = den + jnp.sum(jnp.exp2(sink_row - m), axis=-1, keepdims=True)
    return es, den


def _attn_kernel(*refs, kind, t_len, latent, n_kv, gq, qk_norm, use_sink, emit_kv, n_alias, lambda_init):
    it = iter(refs)
    q_ref, k_ref, v_ref = next(it), next(it), next(it)
    if latent:
        ck_ref, cv_ref, cos_ref, sa_ref, sb_ref = next(it), next(it), next(it), next(it), next(it)
    if kind == "diff":
        lq1_ref, lk1_ref, lq2_ref, lk2_ref, gsub_ref = next(it), next(it), next(it), next(it), next(it)
    if qk_norm:
        gq_ref, gk_ref = next(it), next(it)
    if use_sink:
        sink_ref = next(it)
    for _ in range(n_alias):
        next(it)
    o_ref = next(it)
    if emit_kv:
        ko_ref, vo_ref = next(it), next(it)
    kb_scr, vb_scr = next(it), next(it)

    past = PAST if latent else 0
    pair = HD // 8 if kind == "diff" else HD // 4
    dh = HD // 2 if kind == "diff" else HD
    q_scale = dh ** -0.5 * LOG2E

    for kv in range(n_kv):
        kl = slice(kv * HD, (kv + 1) * HD)
        k = k_ref[:, kl]
        v = v_ref[:, kl]
        if qk_norm:
            k = _rms(k, gk_ref[...])
        if emit_kv:
            ko_ref[:, kl] = k
            vo_ref[:, kl] = v
        if latent:
            k = _rope(k, cos_ref[...], sa_ref[...], sb_ref[...], pair)
            kb_scr[kv, 0:past, :] = ck_ref[:, kl].astype(BF16)
            vb_scr[kv, 0:past, :] = cv_ref[:, kl].astype(BF16)
        kb_scr[kv, past:past + t_len, :] = k.astype(BF16)
        vb_scr[kv, past:past + t_len, :] = v.astype(BF16)

    if kind == "diff":
        lam = (jnp.exp(jnp.sum(lq1_ref[...] * lk1_ref[...], axis=-1, keepdims=True))
               - jnp.exp(jnp.sum(lq2_ref[...] * lk2_ref[...], axis=-1, keepdims=True)) + lambda_init)
        first_map = lax.broadcasted_iota(jnp.int32, (1, HD), 1) < HD // 2

    def unit(qb, kv, g):
        rows = pl.ds(qb * QB, QB) if isinstance(qb, int) else pl.ds(pl.multiple_of(qb * QB, QB), QB)
        lanes = slice((kv * gq + g) * HD, (kv * gq + g + 1) * HD)
        q = q_ref[rows, lanes]
        if qk_norm:
            q = _rms(q, gq_ref[...])
        if latent:
            q = _rope(q, cos_ref[rows, :], sa_ref[rows, :], sb_ref[rows, :], pair)
        q = q * q_scale
        if kind == "diff":
            kb, vb = kb_scr[kv], vb_scr[kv]
            (e1,), d1 = _exp2_rows([_nt_dot(jnp.where(first_map, q, 0.0).astype(BF16), kb)])
            (e2,), d2 = _exp2_rows([_nt_dot(jnp.where(first_map, 0.0, q).astype(BF16), kb)])
            o = (jnp.dot(e1.astype(BF16), vb, preferred_element_type=F32) * (1.0 / d1)
                 - jnp.dot(e2.astype(BF16), vb, preferred_element_type=F32) * (lam / d2))
            o = _rms(o, gsub_ref[...]) * (1.0 - lambda_init)
        elif kind == "gqa":
            sink = sink_ref[kv * gq + g] * LOG2E if use_sink else None
            (e,), den = _exp2_rows([_nt_dot(q.astype(BF16), kb_scr[kv])], sink)
            o = jnp.dot(e.astype(BF16), vb_scr[kv], preferred_element_type=F32) * (1.0 / den)
        else:
            sink = sink_ref[kv * gq + g] * LOG2E
            w_len = 3 * QB
            start = jnp.clip((qb - 1) * QB, 0, t_len - w_len)
            loc = pl.ds(pl.multiple_of(past + start, QB), w_len)
            qh = q.astype(BF16)
            s_ctx = _nt_dot(qh, kb_scr[kv, 0:past, :])
            s_loc = _nt_dot(qh, kb_scr[kv, loc, :])
            qpos = qb * QB + lax.broadcasted_iota(jnp.int32, (QB, 1), 0)
            kpos = start + lax.broadcasted_iota(jnp.int32, (1, w_len), 1)
            s_loc = jnp.where(jnp.abs(qpos - kpos) <= WINDOW, s_loc, -1e30)
            (e_ctx, e_loc), den = _exp2_rows([s_ctx, s_loc], sink)
            o = (jnp.dot(e_ctx.astype(BF16), vb_scr[kv, 0:past, :], preferred_element_type=F32)
                 + jnp.dot(e_loc.astype(BF16), vb_scr[kv, loc, :], preferred_element_type=F32)) * (1.0 / den)
        o_ref[rows, lanes] = o.astype(BF16)

    n_blk = t_len // QB
    per_iter = min(n_blk, 4)
    for kv in range(n_kv):
        for g in range(gq):
            if n_blk == per_iter:
                for qb in range(n_blk):
                    unit(qb, kv, g)
            else:
                def some_blocks(i, carry, kv=kv, g=g):
                    for n in range(per_iter):
                        unit(per_iter * i + n, kv, g)
                    return carry

                lax.fori_loop(0, n_blk // per_iter, some_blocks, 0)


def _attn_call(kind, proj, mix, layer, latent, cols, mix_col, n_kv, caches=None, tabs=None, lam_vecs=None,
               g_subln=None, qk_gains=None, sink=None, kv_out=None):
    t_len, nb, rb0 = (DEC_SEQ, DEC_BATCH, R_CTX // DEC_SEQ) if latent else (SEQ, BATCH, 0)
    gq = 1 if kind == "diff" else GQ
    kv_total = N_HEADS // gq
    c_q, c_k, c_v = cols
    qw, kw = n_kv * gq * HD, n_kv * HD
    grid = (nb, kv_total // n_kv)

    def tok(width, c0):
        return pl.BlockSpec((t_len, width), lambda b, h: (rb0 + b, c0 // width + h))

    in_specs = [tok(qw, c_q), tok(kw, c_k), tok(kw, c_v)]
    args = [proj, proj, proj]
    if latent:
        cache = pl.BlockSpec((None, None, PAST, kw), lambda b, h: (b, layer, 0, h))
        tab = pl.BlockSpec((t_len, HD), lambda b, h: (0, 0))
        in_specs += [cache, cache, tab, tab, tab]
        args += [c.reshape(DEC_BATCH, DEPTH, PAST, kv_total * HD) for c in caches] + list(tabs)
    if kind == "diff":
        vec = pl.BlockSpec((None, 1, HD // 2), lambda b, h: (layer, 0, 0))
        in_specs += [vec] * 4 + [pl.BlockSpec((None, 1, HD), lambda b, h: (layer, 0, 0))]
        args += [v.reshape(DEPTH, 1, HD // 2) for v in lam_vecs] + [g_subln.reshape(DEPTH, 1, HD)]
    if qk_gains is not None:
        gain = pl.BlockSpec((None, 1, HD), lambda b, h: (layer, 0, 0))
        in_specs += [gain, gain]
        args += [g.reshape(DEPTH, 1, HD) for g in qk_gains]
    if sink is not None:
        in_specs.append(pl.BlockSpec((None, n_kv * GQ, 1, HD), lambda b, h: (layer, h, 0, 0)))
        args.append(jnp.pad(sink.reshape(DEPTH, N_HEADS, 1, 1), ((0, 0), (0, 0), (0, 0), (0, HD - 1)),
                            constant_values=-1e30))
    aliases = {len(args): 0}
    in_specs.append(pl.BlockSpec(memory_space=pl.ANY))
    args.append(mix)
    out_specs = [tok(qw, mix_col)]
    out_shape = [_sds((R, D), BF16)]
    emit_kv = kv_out is not None
    if emit_kv:
        for n, prev in enumerate(kv_out):
            aliases[len(args)] = 1 + n
            in_specs.append(pl.BlockSpec(memory_space=pl.ANY))
            args.append(prev)
        kv_spec = pl.BlockSpec((None, None, SEQ, kw), lambda b, h: (b, layer, 0, h))
        out_specs += [kv_spec, kv_spec]
        out_shape += [_sds((BATCH, DEPTH, SEQ, kv_total * HD), F32)] * 2
    s_len = t_len + (PAST if latent else 0)
    return pl.pallas_call(
        functools.partial(_attn_kernel, kind=kind, t_len=t_len, latent=latent, n_kv=n_kv, gq=gq,
                          qk_norm=qk_gains is not None, use_sink=sink is not None, emit_kv=emit_kv,
                          n_alias=len(aliases), lambda_init=0.8 - 0.6 * math.exp(-0.3 * layer)),
        grid=grid,
        in_specs=in_specs, out_specs=out_specs, out_shape=out_shape,
        scratch_shapes=[pltpu.VMEM((n_kv, s_len, HD), BF16), pltpu.VMEM((n_kv, s_len, HD), BF16)],
        input_output_aliases=aliases,
        compiler_params=_params(2),
        name=kind + ("_lat" if latent else "_ctx"),
    )(*args)


def _conv_kernel(x_ref, w_ref, b_ref, o_ref, *, t_len):
    u = x_ref[...]
    w = w_ref[...]
    t = lax.broadcasted_iota(jnp.int32, (t_len, 1), 0)
    half = D_CONV // 2
    acc = u * w[half:half + 1, :] + b_ref[...]
    for s in range(1, half + 1):
        before = jnp.where(t >= s, pltpu.roll(u, s, 0), 0.0)
        after = jnp.where(t < t_len - s, pltpu.roll(u, t_len - s, 0), 0.0)
        acc = acc + before * w[half - s:half - s + 1, :] + after * w[half + s:half + s + 1, :]
    o_ref[...] = acc * _sigmoid(acc)


def _conv_call(proj, conv_w, conv_b, layer, latent):
    t_len, nb, rb0 = (DEC_SEQ, DEC_BATCH, R_CTX // DEC_SEQ) if latent else (SEQ, BATCH, 0)
    tc = 512
    return pl.pallas_call(
        functools.partial(_conv_kernel, t_len=t_len),
        grid=(nb, CONV_DIM // tc),
        in_specs=[pl.BlockSpec((t_len, tc), lambda b, j: (rb0 + b, C_XBC // tc + j)),
                  pl.BlockSpec((None, D_CONV, tc), lambda b, j: (layer, 0, j)),
                  pl.BlockSpec((None, 1, tc), lambda b, j: (layer, 0, j))],
        out_specs=pl.BlockSpec((t_len, tc), lambda b, j: (b, j)),
        out_shape=_sds((nb * t_len, CONV_DIM), F32),
        compiler_params=_params(2),
        name="conv_lat" if latent else "conv_ctx",
    )(proj, conv_w, conv_b.reshape(DEPTH, 1, CONV_DIM))


def _split3(v):
    hi = v.astype(BF16).astype(F32)
    r1 = v - hi
    mid = r1.astype(BF16).astype(F32)
    lo = (r1 - mid).astype(BF16).astype(F32)
    return hi, mid, lo


def _exact_dot(tri, v, tri_first):
    parts = [p.astype(BF16) for p in _split3(v)]
    if tri_first:
        return sum(jnp.dot(tri, p, preferred_element_type=F32) for p in parts)
    return sum(jnp.dot(p, tri, preferred_element_type=F32) for p in parts)


def _lane_spread(v, width):
    hi, mid, lo = _split3(v)
    packed = (hi + pltpu.roll(mid, SSM_H, 1) + pltpu.roll(lo, 2 * SSM_H, 1)).astype(BF16)
    n = SSM_H * width
    src = lax.broadcasted_iota(jnp.int32, (HD, n), 0)
    dst = lax.broadcasted_iota(jnp.int32, (HD, n), 1)
    sel = jnp.where((src < 3 * SSM_H) & (src % SSM_H == dst // width), 1.0, 0.0).astype(BF16)
    return jnp.dot(packed, sel, preferred_element_type=F32)


def _softplus(x):
    return jnp.maximum(x, 0.0) + jnp.log1p(jnp.exp(-jnp.abs(x)))


def _scan_kernel(*refs, direction, has_h0, emit_state, n_chunks, n_alias):
    it = iter(refs)
    xa_ref, dt_ref, dtt_ref, bias_r_ref, bias_c_ref, alog_r_ref, alog_c_ref = (next(it) for _ in range(7))
    if has_h0:
        h0_ref = next(it)
    for _ in range(n_alias):
        next(it)
    y_ref = next(it)
    if emit_state:
        hend_ref = next(it)
    ht_scr = next(it)
    c = pl.program_id(1)
    hs = slice(direction * SSM_H, (direction + 1) * SSM_H)

    @pl.when(c == 0)
    def _():
        if has_h0:
            ht_scr[...] = h0_ref[...].reshape(SSM_INNER, SSM_N).T
        else:
            ht_scr[...] = jnp.zeros(ht_scr.shape, F32)

    ii = lax.broadcasted_iota(jnp.int32, (CHUNK, CHUNK), 0)
    jj = lax.broadcasted_iota(jnp.int32, (CHUNK, CHUNK), 1)
    if direction == 0:
        keep, keep_t = jj <= ii, ii <= jj
        edge = CHUNK - 1
    else:
        keep, keep_t = jj >= ii, ii >= jj
        edge = 0
    tri_col = jnp.where(keep, 1.0, 0.0).astype(BF16)
    tri_row = jnp.where(keep_t, 1.0, 0.0).astype(BF16)

    raw = dt_ref[...]
    if direction == 1:
        raw = pltpu.roll(raw, HD - SSM_H, 1)
    head_lane = lax.broadcasted_iota(jnp.int32, (1, HD), 1) < SSM_H
    dt_c = jnp.where(head_lane, _softplus(raw + bias_r_ref[...]), 0.0)
    a_col = _exact_dot(tri_col, dt_c * (-jnp.exp(alog_r_ref[...])), True)
    dt_r = _softplus(dtt_ref[hs, :] + bias_c_ref[hs, :])
    a_row = _exact_dot(tri_row, dt_r * (-jnp.exp(alog_c_ref[hs, :])), False)

    a_sq = _lane_spread(a_col, CHUNK)
    a_hp = _lane_spread(a_col, SSM_P)
    dt_hp = _lane_spread(dt_c, SSM_P)

    xa = xa_ref[...]
    xdt = xa[:, :SSM_INNER] * dt_hp
    a_tot = a_hp[edge:edge + 1, :]
    grow = jnp.exp(a_hp)
    w_end = (xdt * jnp.exp(a_tot - a_hp)).astype(BF16)
    carry = jnp.exp(a_tot)
    lower_half = lax.broadcasted_iota(jnp.int32, (1, 2 * SSM_P), 1) < SSM_P

    hg = SSM_H // SSM_G
    gw = hg * SSM_P
    for g in range(SSM_G):
        bm = xa[:, SSM_INNER + g * SSM_N:SSM_INNER + (g + 1) * SSM_N].astype(BF16)
        cm = xa[:, SSM_INNER + (SSM_G + g) * SSM_N:SSM_INNER + (SSM_G + g + 1) * SSM_N].astype(BF16)
        cb = _nt_dot(cm, bm)
        gl = slice(g * gw, (g + 1) * gw)
        ht = ht_scr[:, gl]
        y_off = jnp.dot(cm, ht.astype(BF16), preferred_element_type=F32) * grow[:, gl]
        for pr in range(hg // 2):
            pl_ = slice(g * gw + pr * 2 * SSM_P, g * gw + (pr + 1) * 2 * SSM_P)
            x_pair = xdt[:, pl_]
            y_pair = y_off[:, pr * 2 * SSM_P:(pr + 1) * 2 * SSM_P]
            for half in range(2):
                h = g * hg + pr * 2 + half
                decay = jnp.exp(jnp.where(keep, a_sq[:, h * CHUNK:(h + 1) * CHUNK] - a_row[h:h + 1, :], -1e30))
                rhs = jnp.where(lower_half if half == 0 else ~lower_half, x_pair, 0.0).astype(BF16)
                y_pair = y_pair + jnp.dot((cb * decay).astype(BF16), rhs, preferred_element_type=F32)
            y_ref[:, pl_] = y_pair
        st = lax.dot_general(bm, w_end[:, gl], (((0,), (0,)), ((), ())), preferred_element_type=F32)
        ht_scr[:, gl] = ht * carry[:, gl] + st

    if emit_state:
        @pl.when(c == n_chunks - 1)
        def _():
            hend_ref[...] = ht_scr[...].T.reshape(SSM_H, SSM_P, SSM_N)


def _pad_lanes(v):
    return jnp.pad(v, ((0, 0), (0, 0), (0, HD - SSM_H))).reshape(DEPTH, 2, 1, HD)


def _scan_call(xact, proj, dtt, dt_bias, a_log, state, layer, latent, direction, state_out=None):
    t_len, nb, row0 = (DEC_SEQ, DEC_BATCH, R_CTX) if latent else (SEQ, BATCH, 0)
    nc = t_len // CHUNK
    rb0 = row0 // CHUNK

    def chunk(c):
        return c if direction == 0 else nc - 1 - c

    row_vec = pl.BlockSpec((None, None, 1, HD), lambda b, c: (layer, direction, 0, 0))
    col_vec = pl.BlockSpec((None, 2 * SSM_H, 1), lambda b, c: (layer, 0, 0))
    in_specs = [pl.BlockSpec((CHUNK, CONV_DIM), lambda b, c: (b * nc + chunk(c), 0)),
                pl.BlockSpec((CHUNK, HD), lambda b, c: (rb0 + b * nc + chunk(c), C_DT // HD)),
                pl.BlockSpec((2 * SSM_H, CHUNK), lambda b, c: (0, rb0 + b * nc + chunk(c))),
                row_vec, col_vec, row_vec, col_vec]
    args = [xact, proj, dtt,
            _pad_lanes(dt_bias), dt_bias.reshape(DEPTH, 2 * SSM_H, 1),
            _pad_lanes(a_log), a_log.reshape(DEPTH, 2 * SSM_H, 1)]
    if latent:
        in_specs.append(pl.BlockSpec((None, None, None, SSM_H, SSM_P, SSM_N),
                                     lambda b, c: (b, layer, direction, 0, 0, 0)))
        args.append(state)
    out_specs = [pl.BlockSpec((CHUNK, SSM_INNER), lambda b, c: (b * nc + chunk(c), 0))]
    out_shape = [_sds((nb * t_len, SSM_INNER), F32)]
    aliases = {}
    emit_state = state_out is not None
    if emit_state:
        aliases[len(args)] = 1
        in_specs.append(pl.BlockSpec(memory_space=pl.ANY))
        args.append(state_out)
        out_specs.append(pl.BlockSpec((None, None, None, SSM_H, SSM_P, SSM_N),
                                      lambda b, c: (b, layer, direction, 0, 0, 0)))
        out_shape.append(_sds((nb, DEPTH, 2, SSM_H, SSM_P, SSM_N), F32))
    return pl.pallas_call(
        functools.partial(_scan_kernel, direction=direction, has_h0=latent, emit_state=emit_state, n_chunks=nc,
                          n_alias=len(aliases)),
        grid=(nb, nc),
        in_specs=in_specs, out_specs=out_specs, out_shape=out_shape,
        scratch_shapes=[pltpu.VMEM((SSM_N, SSM_INNER), F32)],
        input_output_aliases=aliases,
        compiler_params=_params(2),
        name=f"ssd_{'lat' if latent else 'ctx'}_{'fwd' if direction == 0 else 'bwd'}",
    )(*args)


def _ssd_out_kernel(yf_ref, yb_ref, xs_ref, z_ref, dskip_ref, g_ref, mix_ref, o_ref):
    del mix_ref
    dsk = dskip_ref[0:1, :] + dskip_ref[1:2, :]
    z = z_ref[...]
    y = (yf_ref[...] + yb_ref[...] + dsk * xs_ref[...]) * (z * _sigmoid(z))
    gw = SSM_INNER // SSM_G
    g = g_ref[...]
    for k in range(SSM_G):
        lanes = slice(k * gw, (k + 1) * gw)
        yk = y[:, lanes]
        yk = yk * lax.rsqrt(jnp.mean(yk * yk, axis=-1, keepdims=True) + EPS)
        o_ref[:, lanes] = (yk * g[:, lanes]).astype(BF16)


def _ssd_out_call(yf, yb, xact, proj, d_skip, g_norm, mix, layer, latent):
    tm = 256
    rb0 = (R_CTX if latent else 0) // tm
    n_rows = yf.shape[0]
    local = pl.BlockSpec((tm, SSM_INNER), lambda i: (i, 0))
    return pl.pallas_call(
        _ssd_out_kernel,
        grid=(n_rows // tm,),
        in_specs=[local, local, local,
                  pl.BlockSpec((tm, SSM_INNER), lambda i: (rb0 + i, C_Z // SSM_INNER)),
                  pl.BlockSpec((None, 2, SSM_INNER), lambda i: (layer, 0, 0)),
                  pl.BlockSpec((None, 1, SSM_INNER), lambda i: (layer, 0, 0)),
                  pl.BlockSpec(memory_space=pl.ANY)],
        out_specs=pl.BlockSpec((tm, SSM_INNER), lambda i: (rb0 + i, 3)),
        out_shape=_sds((R, D), BF16),
        input_output_aliases={6: 0},
        compiler_params=_params(1),
        name="ssd_out_lat" if latent else "ssd_out_ctx",
    )(yf, yb, xact, proj, jnp.repeat(d_skip, SSM_P, axis=-1), g_norm.reshape(DEPTH, 1, SSM_INNER), mix)


def kernel(x_prompt, x_sample, c, cache_a_k, cache_a_v, cache_b_k, cache_b_v, cache_c_k, cache_c_v, state_ssm, c_ctx, w_mod, b_mod, g_pre_mix, g_post_mix, g_pre_ffn, g_post_ffn, w_in, w_out, lam_q1, lam_k1, lam_q2, lam_k2, g_subln, g_qnorm, g_knorm, sink, conv_w, conv_b, dt_bias, a_log, d_skip, g_ssm_norm, w_up, w_down):
    x = (x_prompt.reshape(R_CTX, D), x_sample.reshape(R_LAT, D))
    cvec = jnp.concatenate([c_ctx[None, :], c, jnp.zeros((8 - 1 - DEC_BATCH, D), F32)], axis=0)
    mods4 = _mod_call(cvec, w_mod, b_mod).reshape(DEPTH, 8, 1, N_MOD * D)

    tabs_a = _rope_tables(HD // 8)
    tabs_bc = _rope_tables(HD // 4)
    lam_vecs = (lam_q1, lam_k1, lam_q2, lam_k2)

    h, w_in_b, w_out_b = _norm_call(x, mods4, nxt=(g_pre_mix, 0, 1, 0),
                                    casts=((w_in, 0, IN_WP), (w_out, 0, D)))
    kv_a = [jnp.zeros((BATCH, DEPTH, SEQ, N_HEADS * HD), F32) for _ in range(2)]
    kv_b = [jnp.zeros((BATCH, DEPTH, SEQ, N_KV * HD), F32) for _ in range(2)]
    kv_c = [jnp.zeros((BATCH, DEPTH, SEQ, N_KV * HD), F32) for _ in range(2)]
    ssm_new = jnp.zeros((BATCH, DEPTH, 2, SSM_H, SSM_P, SSM_N), F32)
    for layer in range(DEPTH):
        last = layer + 1 == DEPTH
        proj, w_up_b = _mm_call(h, w_in_b, F32, name="in_proj", cast=(w_up, layer, D_FF))

        mix = jnp.zeros((R, D), BF16)
        mix, *kv_a = _attn_call("diff", proj, mix, layer, False, (C_QA, C_KA, C_VA), 0, N_HEADS,
                                lam_vecs=lam_vecs, g_subln=g_subln, kv_out=kv_a)
        mix, *kv_b = _attn_call("gqa", proj, mix, layer, False, (C_QB, C_KB, C_VB), 1024, N_KV,
                                qk_gains=(g_qnorm, g_knorm), kv_out=kv_b)
        mix, *kv_c = _attn_call("gqa", proj, mix, layer, False, (C_QC, C_KC, C_VC), 2048, 1,
                                sink=sink, kv_out=kv_c)
        mix, = _attn_call("diff", proj, mix, layer, True, (C_QA, C_KA, C_VA), 0, 1,
                          caches=(cache_a_k, cache_a_v), tabs=tabs_a, lam_vecs=lam_vecs, g_subln=g_subln)
        mix, = _attn_call("gqa", proj, mix, layer, True, (C_QB, C_KB, C_VB), 1024, 1,
                          caches=(cache_b_k, cache_b_v), tabs=tabs_bc, qk_gains=(g_qnorm, g_knorm))
        mix, = _attn_call("win", proj, mix, layer, True, (C_QC, C_KC, C_VC), 2048, 1,
                          caches=(cache_c_k, cache_c_v), tabs=tabs_bc, sink=sink)

        dtt = proj[:, C_DT:C_DT + 2 * SSM_H].T
        for latent in (False, True):
            xact = _conv_call(proj, conv_w, conv_b, layer, latent)
            ys = []
            for direction in (0, 1):
                res = _scan_call(xact, proj, dtt, dt_bias, a_log, state_ssm, layer, latent, direction,
                                 state_out=None if latent else ssm_new)
                ys.append(res[0])
                if not latent:
                    ssm_new = res[1]
            mix = _ssd_out_call(ys[0], ys[1], xact, proj, d_skip, g_ssm_norm, mix, layer, latent)

        if last:
            mm = _mm_call(mix, w_out_b, F32, name="out_proj")
        else:
            mm, w_out_b = _mm_call(mix, w_out_b, F32, name="out_proj", cast=(w_out, layer + 1, D))
        x, h2 = _norm_call(x, mods4, mm=mm, post=(g_post_mix, layer, 2), nxt=(g_pre_ffn, layer, 4, 3))
        u, w_down_b = _mm_call(h2, w_up_b, BF16, relu2=True, name="ffn_up", cast=(w_down, layer, D))
        if last:
            mm = _mm_call(u, w_down_b, F32, name="ffn_down")
            x_ctx, x_lat = _norm_call(x, mods4, mm=mm, post=(g_post_ffn, layer, 5), split_out=True)
        else:
            mm, w_in_b = _mm_call(u, w_down_b, F32, name="ffn_down", cast=(w_in, layer + 1, IN_WP))
            x, h = _norm_call(x, mods4, mm=mm, post=(g_post_ffn, layer, 5), nxt=(g_pre_mix, layer + 1, 1, 0))

    new_kv = [a.reshape(BATCH, DEPTH, SEQ, -1, HD) for a in (*kv_a, *kv_b, *kv_c)]
    return (x_ctx.reshape(BATCH, SEQ, D), x_lat.reshape(DEC_BATCH, DEC_SEQ, D), *new_kv, ssm_new)
```

```python
import functools
import math

import numpy as np
import jax
import jax.numpy as jnp
from jax import lax
from jax.experimental import pallas as pl
from jax.experimental.pallas import tpu as pltpu

F32 = jnp.float32
BF16 = jnp.bfloat16

D = 4096
BATCH, SEQ = 16, 256
DEC_BATCH, DEC_SEQ = 4, 1024
PAST = 256
DEPTH = 2
GRID_W = 64
HD = 128
N_HEADS = 8
N_KV = 2
GQ = N_HEADS // N_KV
WINDOW = 128
QB = 128
SSM_H, SSM_P, SSM_N, SSM_G = 16, 64, 128, 2
SSM_INNER = SSM_H * SSM_P
CHUNK = 128
D_CONV = 5
CONV_DIM = SSM_INNER + 2 * SSM_G * SSM_N
D_FF = 4 * D
EPS = 1e-6
ROPE_BASE = 10000.0
N_MOD = 6
LOG2E = 1.4426950408889634

R_CTX = BATCH * SEQ
R_LAT = DEC_BATCH * DEC_SEQ
R = R_CTX + R_LAT

C_QA, C_KA, C_VA = 0, 1024, 2048
C_QB, C_KB, C_VB = 3072, 4096, 4352
C_QC, C_KC, C_VC = 4608, 5632, 5888
C_Z, C_XBC, C_DT = 6144, 7168, 8704
IN_W = C_DT + 2 * SSM_H
IN_WP = 9216

VMEM_LIMIT = 56 * 1024 * 1024


def _sds(shape, dtype):
    return jax.ShapeDtypeStruct(shape, dtype)


def _params(n_grid, vmem=VMEM_LIMIT):
    return pltpu.CompilerParams(dimension_semantics=("arbitrary",) * n_grid, vmem_limit_bytes=vmem)


def _rms(x, g):
    return x * lax.rsqrt(jnp.mean(x * x, axis=-1, keepdims=True) + EPS) * g


def _sigmoid(x):
    return 1.0 / (1.0 + jnp.exp(-x))


def _mod_row(i, tm):
    r = i * tm
    return jnp.where(r >= R_CTX, 1 + (r - R_CTX) // DEC_SEQ, 0)


def _linear_step(grid):
    step = pl.program_id(0)
    for axis in range(1, len(grid)):
        step = step * grid[axis] + pl.program_id(axis)
    return step


def _cast_plan(cast, grid):
    cw, layer, out_rows = cast
    rows, cols = cw.shape[1:]
    n_steps = math.prod(grid)
    common = math.gcd(rows, out_rows)
    rb = min(d for d in range(16, common + 1, 16) if common % d == 0 and out_rows // d <= n_steps)
    n_live, n_bands = rows // rb, out_rows // rb

    def step_of(*idx):
        step = idx[0]
        for axis in range(1, len(grid)):
            step = step * grid[axis] + idx[axis]
        return step

    return (pl.BlockSpec((None, rb, cols), lambda *idx: (layer, jnp.minimum(step_of(*idx), n_live - 1), 0)),
            pl.BlockSpec((rb, cols), lambda *idx: (jnp.minimum(step_of(*idx), n_bands - 1), 0)),
            _sds((out_rows, cols), BF16), (n_live, n_bands))


def _cast_band(cw_ref, cwo_ref, grid, bands):
    n_live, n_bands = bands
    if n_live == n_bands:
        cwo_ref[...] = cw_ref[...].astype(BF16)
        return
    step = _linear_step(grid)

    @pl.when(step < n_live)
    def _():
        cwo_ref[...] = cw_ref[...].astype(BF16)

    @pl.when(step >= n_live)
    def _():
        cwo_ref[...] = jnp.zeros(cwo_ref.shape, BF16)


def _mod_kernel(c_ref, w_ref, b_ref, cw_ref, o_ref, cwo_ref, *, grid, bands):
    c = c_ref[...]
    s = (c * _sigmoid(c)).astype(BF16)
    o_ref[...] = jnp.dot(s, w_ref[...].astype(BF16), preferred_element_type=F32) + b_ref[...]
    _cast_band(cw_ref, cwo_ref, grid, bands)


def _mod_call(cvec, w_mod, b_mod, cast):
    tn = 512
    grid = (DEPTH, N_MOD * D // tn)
    c_in, c_out, c_shape, bands = _cast_plan(cast, grid)
    return pl.pallas_call(
        functools.partial(_mod_kernel, grid=grid, bands=bands),
        grid=grid,
        in_specs=[pl.BlockSpec((8, D), lambda l, j: (0, 0)),
                  pl.BlockSpec((None, D, tn), lambda l, j: (l, 0, j)),
                  pl.BlockSpec((None, 1, tn), lambda l, j: (l, 0, j)), c_in],
        out_specs=[pl.BlockSpec((None, 8, tn), lambda l, j: (l, 0, j)), c_out],
        out_shape=[_sds((DEPTH, 8, N_MOD * D), F32), c_shape],
        compiler_params=_params(2),
        name="modulation",
    )(cvec, w_mod, b_mod.reshape(DEPTH, 1, N_MOD * D), cast[0])


TM_ROW = 256
N_CTX_TILES = R_CTX // TM_ROW


def _mod_spec(layer, chunk, tm):
    return pl.BlockSpec((None, None, 1, D), lambda i: (layer, _mod_row(i, tm), 0, chunk))


def _gain_spec(layer):
    return pl.BlockSpec((None, 1, D), lambda i: (layer, 0, 0))


_ROW = pl.BlockSpec((TM_ROW, D), lambda i: (i, 0))
_ROW_CTX = pl.BlockSpec((TM_ROW, D), lambda i: (jnp.minimum(i, N_CTX_TILES - 1), 0))
_ROW_LAT = pl.BlockSpec((TM_ROW, D), lambda i: (jnp.maximum(i - N_CTX_TILES, 0), 0))


def _norm_kernel(*refs, has_mm, split_in, with_next, split_out, cast_bands):
    it = iter(refs)
    if has_mm:
        mm_ref = next(it)
    x_refs = (next(it), next(it)) if split_in else (next(it),)
    if has_mm:
        gpost_ref, gt_ref = next(it), next(it)
    if with_next:
        gpre_ref, sc_ref, sh_ref = next(it), next(it), next(it)
    cw_refs = [next(it) for _ in cast_bands]
    if has_mm:
        xo_refs = (next(it), next(it)) if split_out else (next(it),)
    if with_next:
        h_ref = next(it)
    for cw_ref, bands in zip(cw_refs, cast_bands):
        _cast_band(cw_ref, next(it), (R // TM_ROW,), bands)

    def body(x_ref, xo_ref):
        x = x_ref[...]
        if has_mm:
            x = x + gt_ref[...] * _rms(mm_ref[...], gpost_ref[...])
            xo_ref[...] = x
        if with_next:
            h_ref[...] = (_rms(x, gpre_ref[...]) * (1.0 + sc_ref[...]) + sh_ref[...]).astype(BF16)

    if split_in or split_out:
        i = pl.program_id(0)

        @pl.when(i < N_CTX_TILES)
        def _():
            body(x_refs[0], xo_refs[0] if has_mm else None)

        @pl.when(i >= N_CTX_TILES)
        def _():
            body(x_refs[-1], xo_refs[-1] if has_mm else None)
    else:
        body(x_refs[0], xo_refs[0] if has_mm else None)


def _norm_call(x, mods4, mm=None, post=None, nxt=None, split_out=False, casts=()):
    split_in = isinstance(x, tuple)
    in_specs, args = [], []
    if mm is not None:
        in_specs.append(_ROW)
        args.append(mm)
    in_specs += [_ROW_CTX, _ROW_LAT] if split_in else [_ROW]
    args += list(x) if split_in else [x]
    out_specs, out_shape = [], []
    if mm is not None:
        g_post, pl_layer, gate_chunk = post
        in_specs += [_gain_spec(pl_layer), _mod_spec(pl_layer, gate_chunk, TM_ROW)]
        args += [g_post.reshape(DEPTH, 1, D), mods4]
        if split_out:
            out_specs += [_ROW_CTX, _ROW_LAT]
            out_shape += [_sds((R_CTX, D), F32), _sds((R_LAT, D), F32)]
        else:
            out_specs.append(_ROW)
            out_shape.append(_sds((R, D), F32))
    if nxt is not None:
        g_pre, nl, sc_chunk, sh_chunk = nxt
        in_specs += [_gain_spec(nl), _mod_spec(nl, sc_chunk, TM_ROW), _mod_spec(nl, sh_chunk, TM_ROW)]
        args += [g_pre.reshape(DEPTH, 1, D), mods4, mods4]
        out_specs.append(_ROW)
        out_shape.append(_sds((R, D), BF16))
    cast_bands = []
    for cast in casts:
        c_in, c_out, c_shape, bands = _cast_plan(cast, (R // TM_ROW,))
        in_specs.append(c_in)
        args.append(cast[0])
        out_specs.append(c_out)
        out_shape.append(c_shape)
        cast_bands.append(bands)
    return pl.pallas_call(
        functools.partial(_norm_kernel, has_mm=mm is not None, split_in=split_in, with_next=nxt is not None,
                          split_out=split_out, cast_bands=tuple(cast_bands)),
        grid=(R // TM_ROW,),
        in_specs=in_specs, out_specs=out_specs, out_shape=out_shape,
        compiler_params=_params(1),
        name="post_norm" if mm is not None else "pre_norm",
    )(*args)


def _mm_kernel(*refs, grid, relu2, w_rows_out, bands, zero_fill):
    it = iter(refs)
    a_ref, w_ref = next(it), next(it)
    cw_ref = next(it) if bands else None
    o_ref = next(it)
    if bands:
        _cast_band(cw_ref, next(it), grid, bands)
    if zero_fill:
        z_ref = next(it)
        z_ref[...] = jnp.zeros(z_ref.shape, z_ref.dtype)
    nk = grid[2]
    if w_rows_out:
        acc = _nt_dot(a_ref[...], w_ref[...])
    else:
        acc = jnp.dot(a_ref[...], w_ref[...], preferred_element_type=F32)
    if nk == 1:
        if relu2:
            u = jnp.maximum(acc, 0.0)
            acc = u * u
        o_ref[...] = acc.astype(o_ref.dtype)
    else:
        k = pl.program_id(2)

        @pl.when(k == 0)
        def _():
            o_ref[...] = acc

        @pl.when(k > 0)
        def _():
            o_ref[...] += acc


CAST_BAND_BYTES = 2 * 1024 * 1024


def _mm_tiles(m, n, kdim, cast):
    tm, tn, tk = 1024, 1024, 4096
    if cast is not None:
        steps = (m // tm) * (n // tn) * (kdim // tk)
        if cast[0].shape[1] * cast[0].shape[2] * 4 // steps > CAST_BAND_BYTES:
            tn //= 2
    return tm, tn, tk


def _mm_call(a, w, out_dtype, relu2=False, name="proj", cast=None, w_rows_out=False, zero_fill=None):
    m, kdim = a.shape
    n = w.shape[0] if w_rows_out else w.shape[1]
    tm, tn, tk = _mm_tiles(m, n, kdim, cast)
    nk = kdim // tk
    assert nk == 1 or (out_dtype == F32 and not relu2)
    grid = (m // tm, n // tn, nk)
    in_specs = [pl.BlockSpec((tm, tk), lambda i, j, k: (i, k)),
                pl.BlockSpec((tn, tk), lambda i, j, k: (j, k)) if w_rows_out else
                pl.BlockSpec((tk, tn), lambda i, j, k: (k, j))]
    args = [a, w]
    out_specs = [pl.BlockSpec((tm, tn), lambda i, j, k: (i, j))]
    out_shape = [_sds((m, n), out_dtype)]
    bands = None
    if cast is not None:
        c_in, c_out, c_shape, bands = _cast_plan(cast, grid)
        in_specs.append(c_in)
        args.append(cast[0])
        out_specs.append(c_out)
        out_shape.append(c_shape)
    if zero_fill is not None:
        (z_rows, z_cols), z_dtype = zero_fill
        n_steps = math.prod(grid)
        zb = min(d for d in range(16, z_rows + 1, 16) if z_rows % d == 0 and z_rows // d <= n_steps)
        nzb = z_rows // zb
        out_specs.append(pl.BlockSpec(
            (zb, z_cols), lambda i, j, k: (jnp.minimum((i * grid[1] + j) * grid[2] + k, nzb - 1), 0)))
        out_shape.append(_sds((z_rows, z_cols), z_dtype))
    return pl.pallas_call(
        functools.partial(_mm_kernel, grid=grid, relu2=relu2, w_rows_out=w_rows_out, bands=bands,
                          zero_fill=zero_fill is not None),
        grid=grid,
        in_specs=in_specs, out_specs=out_specs, out_shape=out_shape,
        compiler_params=_params(3),
        name=name,
    )(*args)


def _rope_tables(pair):
    t = np.arange(DEC_SEQ)
    row, col = t // GRID_W, t % GRID_W
    lane = np.arange(HD)
    da = 2 * pair
    inv_freq = 1.0 / (ROPE_BASE ** (np.arange(0, da, 2, dtype=np.float64) / da))
    pos = np.where((lane // da) % 2 == 0, row[:, None], col[:, None]).astype(np.float64)
    ang = pos * inv_freq[lane % pair][None, :]
    first = (lane % da) < pair
    cos = np.cos(ang)
    sin = np.sin(ang)
    sin_a = np.where(first[None, :], -sin, 0.0)
    sin_b = np.where(first[None, :], 0.0, sin)
    return tuple(jnp.asarray(a, F32) for a in (cos, sin_a, sin_b))


def _rope(x, cos, sin_a, sin_b, pair):
    return x * cos + pltpu.roll(x, HD - pair, 1) * sin_a + pltpu.roll(x, pair, 1) * sin_b


def _nt_dot(a, b):
    return lax.dot_general(a, b, (((1,), (1,)), ((), ())), preferred_element_type=F32)


def _exp2_rows(segs, sink_row=None):
    m = None
    for n, s in enumerate(segs):
        if n == 0 and sink_row is not None:
            mn = jnp.maximum(jnp.max(jnp.maximum(s[:, :HD], sink_row), axis=-1, keepdims=True),
                             jnp.max(s[:, HD:], axis=-1, keepdims=True))
        else:
            mn = jnp.max(s, axis=-1, keepdims=True)
        m = mn if m is None else jnp.maximum(m, mn)
    es = [jnp.exp2(s - m) for s in segs]
    den = sum(jnp.sum(e, axis=-1, keepdims=True) for e in es)
    if sink_row is not None:
        den = den + jnp.sum(jnp.exp2(sink_row - m), axis=-1, keepdims=True)
    return es, den


def _attn_kernel(*refs, kind, t_len, latent, n_kv, gq, qk_norm, use_sink, emit_kv, n_alias, lambda_init):
    it = iter(refs)
    q_ref, k_ref, v_ref = next(it), next(it), next(it)
    if latent:
        ck_ref, cv_ref, cos_ref, sa_ref, sb_ref = next(it), next(it), next(it), next(it), next(it)
    if kind == "diff":
        lq1_ref, lk1_ref, lq2_ref, lk2_ref, gsub_ref = next(it), next(it), next(it), next(it), next(it)
    if qk_norm:
        gq_ref, gk_ref = next(it), next(it)
    if use_sink:
        sink_ref = next(it)
    for _ in range(n_alias):
        next(it)
    o_ref = next(it)
    if emit_kv:
        ko_ref, vo_ref = next(it), next(it)
    kb_scr, vb_scr = next(it), next(it)

    past = PAST if latent else 0
    pair = HD // 8 if kind == "diff" else HD // 4
    dh = HD // 2 if kind == "diff" else HD
    q_scale = dh ** -0.5 * LOG2E

    for kv in range(n_kv):
        kl = slice(kv * HD, (kv + 1) * HD)
        k = k_ref[:, kl]
        v = v_ref[:, kl]
        if qk_norm:
            k = _rms(k, gk_ref[...])
        if emit_kv:
            ko_ref[:, kl] = k
            vo_ref[:, kl] = v
        if latent:
            k = _rope(k, cos_ref[...], sa_ref[...], sb_ref[...], pair)
            kb_scr[kv, 0:past, :] = ck_ref[:, kl].astype(BF16)
            vb_scr[kv, 0:past, :] = cv_ref[:, kl].astype(BF16)
        kb_scr[kv, past:past + t_len, :] = k.astype(BF16)
        vb_scr[kv, past:past + t_len, :] = v.astype(BF16)

    if kind == "diff":
        lam = (jnp.exp(jnp.sum(lq1_ref[...] * lk1_ref[...], axis=-1, keepdims=True))
               - jnp.exp(jnp.sum(lq2_ref[...] * lk2_ref[...], axis=-1, keepdims=True)) + lambda_init)
        first_map = lax.broadcasted_iota(jnp.int32, (1, HD), 1) < HD // 2

    def unit(qb, kv, g):
        rows = pl.ds(qb * QB, QB) if isinstance(qb, int) else pl.ds(pl.multiple_of(qb * QB, QB), QB)
        lanes = slice((kv * gq + g) * HD, (kv * gq + g + 1) * HD)
        q = q_ref[rows, lanes]
        if qk_norm:
            q = _rms(q, gq_ref[...])
        if latent:
            q = _rope(q, cos_ref[rows, :], sa_ref[rows, :], sb_ref[rows, :], pair)
        q = q * q_scale
        if kind == "diff":
            kb, vb = kb_scr[kv], vb_scr[kv]
            (e1,), d1 = _exp2_rows([_nt_dot(jnp.where(first_map, q, 0.0).astype(BF16), kb)])
            (e2,), d2 = _exp2_rows([_nt_dot(jnp.where(first_map, 0.0, q).astype(BF16), kb)])
            o = (jnp.dot(e1.astype(BF16), vb, preferred_element_type=F32) * (1.0 / d1)
                 - jnp.dot(e2.astype(BF16), vb, preferred_element_type=F32) * (lam / d2))
            o = _rms(o, gsub_ref[...]) * (1.0 - lambda_init)
        elif kind == "gqa":
            sink = sink_ref[kv * gq + g] * LOG2E if use_sink else None
            (e,), den = _exp2_rows([_nt_dot(q.astype(BF16), kb_scr[kv])], sink)
            o = jnp.dot(e.astype(BF16), vb_scr[kv], preferred_element_type=F32) * (1.0 / den)
        else:
            sink = sink_ref[kv * gq + g] * LOG2E
            w_len = 3 * QB
            start = jnp.clip((qb - 1) * QB, 0, t_len - w_len)
            loc = pl.ds(pl.multiple_of(past + start, QB), w_len)
            qh = q.astype(BF16)
            s_ctx = _nt_dot(qh, kb_scr[kv, 0:past, :])
            s_loc = _nt_dot(qh, kb_scr[kv, loc, :])
            qpos = qb * QB + lax.broadcasted_iota(jnp.int32, (QB, 1), 0)
            kpos = start + lax.broadcasted_iota(jnp.int32, (1, w_len), 1)
            s_loc = jnp.where(jnp.abs(qpos - kpos) <= WINDOW, s_loc, -1e30)
            (e_ctx, e_loc), den = _exp2_rows([s_ctx, s_loc], sink)
            o = (jnp.dot(e_ctx.astype(BF16), vb_scr[kv, 0:past, :], preferred_element_type=F32)
                 + jnp.dot(e_loc.astype(BF16), vb_scr[kv, loc, :], preferred_element_type=F32)) * (1.0 / den)
        o_ref[rows, lanes] = o.astype(BF16)

    n_blk = t_len // QB
    per_iter = min(n_blk, 4)
    for kv in range(n_kv):
        for g in range(gq):
            if n_blk == per_iter:
                for qb in range(n_blk):
                    unit(qb, kv, g)
            else:
                def some_blocks(i, carry, kv=kv, g=g):
                    for n in range(per_iter):
                        unit(per_iter * i + n, kv, g)
                    return carry

                lax.fori_loop(0, n_blk // per_iter, some_blocks, 0)


def _attn_call(kind, proj, mix, layer, latent, cols, mix_col, n_kv, caches=None, tabs=None, lam_vecs=None,
               g_subln=None, qk_gains=None, sink=None, kv_out=None):
    t_len, nb, rb0 = (DEC_SEQ, DEC_BATCH, R_CTX // DEC_SEQ) if latent else (SEQ, BATCH, 0)
    gq = 1 if kind == "diff" else GQ
    kv_total = N_HEADS // gq
    c_q, c_k, c_v = cols
    qw, kw = n_kv * gq * HD, n_kv * HD
    grid = (nb, kv_total // n_kv)

    def tok(width, c0):
        return pl.BlockSpec((t_len, width), lambda b, h: (rb0 + b, c0 // width + h))

    in_specs = [tok(qw, c_q), tok(kw, c_k), tok(kw, c_v)]
    args = [proj, proj, proj]
    if latent:
        cache = pl.BlockSpec((None, None, PAST, kw), lambda b, h: (b, layer, 0, h))
        tab = pl.BlockSpec((t_len, HD), lambda b, h: (0, 0))
        in_specs += [cache, cache, tab, tab, tab]
        args += [c.reshape(DEC_BATCH, DEPTH, PAST, kv_total * HD) for c in caches] + list(tabs)
    if kind == "diff":
        vec = pl.BlockSpec((None, 1, HD // 2), lambda b, h: (layer, 0, 0))
        in_specs += [vec] * 4 + [pl.BlockSpec((None, 1, HD), lambda b, h: (layer, 0, 0))]
        args += [v.reshape(DEPTH, 1, HD // 2) for v in lam_vecs] + [g_subln.reshape(DEPTH, 1, HD)]
    if qk_gains is not None:
        gain = pl.BlockSpec((None, 1, HD), lambda b, h: (layer, 0, 0))
        in_specs += [gain, gain]
        args += [g.reshape(DEPTH, 1, HD) for g in qk_gains]
    if sink is not None:
        in_specs.append(pl.BlockSpec((None, n_kv * GQ, 1, HD), lambda b, h: (layer, h, 0, 0)))
        args.append(jnp.pad(sink.reshape(DEPTH, N_HEADS, 1, 1), ((0, 0), (0, 0), (0, 0), (0, HD - 1)),
                            constant_values=-1e30))
    aliases = {len(args): 0}
    in_specs.append(pl.BlockSpec(memory_space=pl.ANY))
    args.append(mix)
    out_specs = [tok(qw, mix_col)]
    out_shape = [_sds((R, D), BF16)]
    emit_kv = kv_out is not None
    if emit_kv:
        for n, prev in enumerate(kv_out):
            aliases[len(args)] = 1 + n
            in_specs.append(pl.BlockSpec(memory_space=pl.ANY))
            args.append(prev)
        kv_spec = pl.BlockSpec((None, None, SEQ, kw), lambda b, h: (b, layer, 0, h))
        out_specs += [kv_spec, kv_spec]
        out_shape += [_sds((BATCH, DEPTH, SEQ, kv_total * HD), F32)] * 2
    s_len = t_len + (PAST if latent else 0)
    return pl.pallas_call(
        functools.partial(_attn_kernel, kind=kind, t_len=t_len, latent=latent, n_kv=n_kv, gq=gq,
                          qk_norm=qk_gains is not None, use_sink=sink is not None, emit_kv=emit_kv,
                          n_alias=len(aliases), lambda_init=0.8 - 0.6 * math.exp(-0.3 * layer)),
        grid=grid,
        in_specs=in_specs, out_specs=out_specs, out_shape=out_shape,
        scratch_shapes=[pltpu.VMEM((n_kv, s_len, HD), BF16), pltpu.VMEM((n_kv, s_len, HD), BF16)],
        input_output_aliases=aliases,
        compiler_params=_params(2),
        name=kind + ("_lat" if latent else "_ctx"),
    )(*args)


def _conv_kernel(x_ref, w_ref, b_ref, o_ref, *, t_len):
    u = x_ref[...]
    w = w_ref[...]
    t = lax.broadcasted_iota(jnp.int32, (t_len, 1), 0)
    half = D_CONV // 2
    acc = u * w[half:half + 1, :] + b_ref[...]
    for s in range(1, half + 1):
        before = jnp.where(t >= s, pltpu.roll(u, s, 0), 0.0)
        after = jnp.where(t < t_len - s, pltpu.roll(u, t_len - s, 0), 0.0)
        acc = acc + before * w[half - s:half - s + 1, :] + after * w[half + s:half + s + 1, :]
    o_ref[...] = acc * _sigmoid(acc)


def _conv_call(proj, conv_w, conv_b, layer, latent):
    t_len, nb, rb0 = (DEC_SEQ, DEC_BATCH, R_CTX // DEC_SEQ) if latent else (SEQ, BATCH, 0)
    tc = 512
    return pl.pallas_call(
        functools.partial(_conv_kernel, t_len=t_len),
        grid=(nb, CONV_DIM // tc),
        in_specs=[pl.BlockSpec((t_len, tc), lambda b, j: (rb0 + b, C_XBC // tc + j)),
                  pl.BlockSpec((None, D_CONV, tc), lambda b, j: (layer, 0, j)),
                  pl.BlockSpec((None, 1, tc), lambda b, j: (layer, 0, j))],
        out_specs=pl.BlockSpec((t_len, tc), lambda b, j: (b, j)),
        out_shape=_sds((nb * t_len, CONV_DIM), F32),
        compiler_params=_params(2),
        name="conv_lat" if latent else "conv_ctx",
    )(proj, conv_w, conv_b.reshape(DEPTH, 1, CONV_DIM))


def _split3(v):
    hi = v.astype(BF16).astype(F32)
    r1 = v - hi
    mid = r1.astype(BF16).astype(F32)
    lo = (r1 - mid).astype(BF16).astype(F32)
    return hi, mid, lo


def _exact_dot(tri, v, tri_first):
    parts = [p.astype(BF16) for p in _split3(v)]
    if tri_first:
        return sum(jnp.dot(tri, p, preferred_element_type=F32) for p in parts)
    return sum(jnp.dot(p, tri, preferred_element_type=F32) for p in parts)


def _lane_spread(v, width):
    hi, mid, lo = _split3(v)
    packed = (hi + pltpu.roll(mid, SSM_H, 1) + pltpu.roll(lo, 2 * SSM_H, 1)).astype(BF16)
    n = SSM_H * width
    src = lax.broadcasted_iota(jnp.int32, (HD, n), 0)
    dst = lax.broadcasted_iota(jnp.int32, (HD, n), 1)
    sel = jnp.where((src < 3 * SSM_H) & (src % SSM_H == dst // width), 1.0, 0.0).astype(BF16)
    return jnp.dot(packed, sel, preferred_element_type=F32)


def _softplus(x):
    return jnp.maximum(x, 0.0) + jnp.log1p(jnp.exp(-jnp.abs(x)))


def _scan_kernel(*refs, direction, has_h0, emit_state, n_chunks, n_alias):
    it = iter(refs)
    xa_ref, dt_ref, bias_ref, alog_ref = (next(it) for _ in range(4))
    if has_h0:
        h0_ref = next(it)
    for _ in range(n_alias):
        next(it)
    y_ref = next(it)
    if emit_state:
        hend_ref = next(it)
    ht_scr = next(it)
    c = pl.program_id(1)

    @pl.when(c == 0)
    def _():
        if has_h0:
            ht_scr[...] = h0_ref[...].reshape(SSM_INNER, SSM_N).T
        else:
            ht_scr[...] = jnp.zeros(ht_scr.shape, F32)

    ii = lax.broadcasted_iota(jnp.int32, (CHUNK, CHUNK), 0)
    jj = lax.broadcasted_iota(jnp.int32, (CHUNK, CHUNK), 1)
    if direction == 0:
        keep, keep_t = jj <= ii, ii <= jj
        edge = CHUNK - 1
    else:
        keep, keep_t = jj >= ii, ii >= jj
        edge = 0
    tri_col = jnp.where(keep, 1.0, 0.0).astype(BF16)
    tri_row = jnp.where(keep_t, 1.0, 0.0).astype(BF16)

    raw = dt_ref[...]
    if direction == 1:
        raw = pltpu.roll(raw, HD - SSM_H, 1)
    head_lane = lax.broadcasted_iota(jnp.int32, (1, HD), 1) < SSM_H
    dt_c = jnp.where(head_lane, _softplus(raw + bias_ref[...]), 0.0)
    da_c = dt_c * (-jnp.exp(alog_ref[...]))
    a_col = _exact_dot(tri_col, da_c, True)
    a_row = _exact_dot(tri_row, da_c.T[0:SSM_H, :], False)

    a_sq = _lane_spread(a_col, CHUNK)
    a_hp = _lane_spread(a_col, SSM_P)
    dt_hp = _lane_spread(dt_c, SSM_P)

    xa = xa_ref[...]
    xdt = xa[:, :SSM_INNER] * dt_hp
    a_tot = a_hp[edge:edge + 1, :]
    grow = jnp.exp(a_hp)
    w_end = (xdt * jnp.exp(a_tot - a_hp)).astype(BF16)
    carry = jnp.exp(a_tot)
    lower_half = lax.broadcasted_iota(jnp.int32, (1, 2 * SSM_P), 1) < SSM_P

    hg = SSM_H // SSM_G
    gw = hg * SSM_P
    for g in range(SSM_G):
        bm = xa[:, SSM_INNER + g * SSM_N:SSM_INNER + (g + 1) * SSM_N].astype(BF16)
        cm = xa[:, SSM_INNER + (SSM_G + g) * SSM_N:SSM_INNER + (SSM_G + g + 1) * SSM_N].astype(BF16)
        cb = _nt_dot(cm, bm)
        gl = slice(g * gw, (g + 1) * gw)
        ht = ht_scr[:, gl]
        y_off = jnp.dot(cm, ht.astype(BF16), preferred_element_type=F32) * grow[:, gl]
        for pr in range(hg // 2):
            pl_ = slice(g * gw + pr * 2 * SSM_P, g * gw + (pr + 1) * 2 * SSM_P)
            x_pair = xdt[:, pl_]
            y_pair = y_off[:, pr * 2 * SSM_P:(pr + 1) * 2 * SSM_P]
            for half in range(2):
                h = g * hg + pr * 2 + half
                decay = jnp.exp(jnp.where(keep, a_sq[:, h * CHUNK:(h + 1) * CHUNK] - a_row[h:h + 1, :], -1e30))
                rhs = jnp.where(lower_half if half == 0 else ~lower_half, x_pair, 0.0).astype(BF16)
                y_pair = y_pair + jnp.dot((cb * decay).astype(BF16), rhs, preferred_element_type=F32)
            y_ref[:, pl_] = y_pair
        st = lax.dot_general(bm, w_end[:, gl], (((0,), (0,)), ((), ())), preferred_element_type=F32)
        ht_scr[:, gl] = ht * carry[:, gl] + st

    if emit_state:
        @pl.when(c == n_chunks - 1)
        def _():
            hend_ref[...] = ht_scr[...].T.reshape(SSM_H, SSM_P, SSM_N)


def _pad_lanes(v):
    return jnp.pad(v, ((0, 0), (0, 0), (0, HD - SSM_H))).reshape(DEPTH, 2, 1, HD)


def _scan_call(xact, proj, dt_bias, a_log, state, layer, latent, direction, state_out=None):
    t_len, nb, row0 = (DEC_SEQ, DEC_BATCH, R_CTX) if latent else (SEQ, BATCH, 0)
    nc = t_len // CHUNK
    rb0 = row0 // CHUNK

    def chunk(c):
        return c if direction == 0 else nc - 1 - c

    row_vec = pl.BlockSpec((None, None, 1, HD), lambda b, c: (layer, direction, 0, 0))
    in_specs = [pl.BlockSpec((CHUNK, CONV_DIM), lambda b, c: (b * nc + chunk(c), 0)),
                pl.BlockSpec((CHUNK, HD), lambda b, c: (rb0 + b * nc + chunk(c), C_DT // HD)),
                row_vec, row_vec]
    args = [xact, proj, _pad_lanes(dt_bias), _pad_lanes(a_log)]
    if latent:
        in_specs.append(pl.BlockSpec((None, None, None, SSM_H, SSM_P, SSM_N),
                                     lambda b, c: (b, layer, direction, 0, 0, 0)))
        args.append(state)
    out_specs = [pl.BlockSpec((CHUNK, SSM_INNER), lambda b, c: (b * nc + chunk(c), 0))]
    out_shape = [_sds((nb * t_len, SSM_INNER), F32)]
    aliases = {}
    emit_state = state_out is not None
    if emit_state:
        aliases[len(args)] = 1
        in_specs.append(pl.BlockSpec(memory_space=pl.ANY))
        args.append(state_out)
        out_specs.append(pl.BlockSpec((None, None, None, SSM_H, SSM_P, SSM_N),
                                      lambda b, c: (b, layer, direction, 0, 0, 0)))
        out_shape.append(_sds((nb, DEPTH, 2, SSM_H, SSM_P, SSM_N), F32))
    return pl.pallas_call(
        functools.partial(_scan_kernel, direction=direction, has_h0=latent, emit_state=emit_state, n_chunks=nc,
                          n_alias=len(aliases)),
        grid=(nb, nc),
        in_specs=in_specs, out_specs=out_specs, out_shape=out_shape,
        scratch_shapes=[pltpu.VMEM((SSM_N, SSM_INNER), F32)],
        input_output_aliases=aliases,
        compiler_params=_params(2),
        name=f"ssd_{'lat' if latent else 'ctx'}_{'fwd' if direction == 0 else 'bwd'}",
    )(*args)


def _ssd_out_kernel(yf_ref, yb_ref, xs_ref, z_ref, dskip_ref, g_ref, mix_ref, o_ref):
    del mix_ref
    dsk = dskip_ref[0:1, :] + dskip_ref[1:2, :]
    z = z_ref[...]
    y = (yf_ref[...] + yb_ref[...] + dsk * xs_ref[...]) * (z * _sigmoid(z))
    gw = SSM_INNER // SSM_G
    g = g_ref[...]
    for k in range(SSM_G):
        lanes = slice(k * gw, (k + 1) * gw)
        yk = y[:, lanes]
        yk = yk * lax.rsqrt(jnp.mean(yk * yk, axis=-1, keepdims=True) + EPS)
        o_ref[:, lanes] = (yk * g[:, lanes]).astype(BF16)


def _ssd_out_call(yf, yb, xact, proj, d_skip, g_norm, mix, layer, latent):
    tm = 256
    rb0 = (R_CTX if latent else 0) // tm
    n_rows = yf.shape[0]
    local = pl.BlockSpec((tm, SSM_INNER), lambda i: (i, 0))
    return pl.pallas_call(
        _ssd_out_kernel,
        grid=(n_rows // tm,),
        in_specs=[local, local, local,
                  pl.BlockSpec((tm, SSM_INNER), lambda i: (rb0 + i, C_Z // SSM_INNER)),
                  pl.BlockSpec((None, 2, SSM_INNER), lambda i: (layer, 0, 0)),
                  pl.BlockSpec((None, 1, SSM_INNER), lambda i: (layer, 0, 0)),
                  pl.BlockSpec(memory_space=pl.ANY)],
        out_specs=pl.BlockSpec((tm, SSM_INNER), lambda i: (rb0 + i, 3)),
        out_shape=_sds((R, D), BF16),
        input_output_aliases={6: 0},
        compiler_params=_params(1),
        name="ssd_out_lat" if latent else "ssd_out_ctx",
    )(yf, yb, xact, proj, jnp.repeat(d_skip, SSM_P, axis=-1), g_norm.reshape(DEPTH, 1, SSM_INNER), mix)


def kernel(x_prompt, x_sample, c, cache_a_k, cache_a_v, cache_b_k, cache_b_v, cache_c_k, cache_c_v, state_ssm, c_ctx, w_mod, b_mod, g_pre_mix, g_post_mix, g_pre_ffn, g_post_ffn, w_in, w_out, lam_q1, lam_k1, lam_q2, lam_k2, g_subln, g_qnorm, g_knorm, sink, conv_w, conv_b, dt_bias, a_log, d_skip, g_ssm_norm, w_up, w_down):
    x = (x_prompt.reshape(R_CTX, D), x_sample.reshape(R_LAT, D))
    cvec = jnp.concatenate([c_ctx[None, :], c, jnp.zeros((8 - 1 - DEC_BATCH, D), F32)], axis=0)
    w_in_t = jnp.swapaxes(w_in, 1, 2)
    mods, w_in_b = _mod_call(cvec, w_mod, b_mod, (w_in_t, 0, IN_WP))
    mods4 = mods.reshape(DEPTH, 8, 1, N_MOD * D)

    tabs_a = _rope_tables(HD // 8)
    tabs_bc = _rope_tables(HD // 4)
    lam_vecs = (lam_q1, lam_k1, lam_q2, lam_k2)

    h, w_out_b = _norm_call(x, mods4, nxt=(g_pre_mix, 0, 1, 0), casts=((w_out, 0, D),))
    kv_a = [jnp.zeros((BATCH, DEPTH, SEQ, N_HEADS * HD), F32) for _ in range(2)]
    kv_b = [jnp.zeros((BATCH, DEPTH, SEQ, N_KV * HD), F32) for _ in range(2)]
    kv_c = [jnp.zeros((BATCH, DEPTH, SEQ, N_KV * HD), F32) for _ in range(2)]
    ssm_new = jnp.zeros((BATCH, DEPTH, 2, SSM_H, SSM_P, SSM_N), F32)
    for layer in range(DEPTH):
        last = layer + 1 == DEPTH
        proj, w_up_b, mix = _mm_call(h, w_in_b, F32, name="in_proj", cast=(w_up, layer, D), w_rows_out=True,
                                     zero_fill=((R, D), BF16))

        mix, *kv_a = _attn_call("diff", proj, mix, layer, False, (C_QA, C_KA, C_VA), 0, N_HEADS,
                                lam_vecs=lam_vecs, g_subln=g_subln, kv_out=kv_a)
        mix, *kv_b = _attn_call("gqa", proj, mix, layer, False, (C_QB, C_KB, C_VB), 1024, N_KV,
                                qk_gains=(g_qnorm, g_knorm), kv_out=kv_b)
        mix, *kv_c = _attn_call("gqa", proj, mix, layer, False, (C_QC, C_KC, C_VC), 2048, 1,
                                sink=sink, kv_out=kv_c)
        mix, = _attn_call("diff", proj, mix, layer, True, (C_QA, C_KA, C_VA), 0, 1,
                          caches=(cache_a_k, cache_a_v), tabs=tabs_a, lam_vecs=lam_vecs, g_subln=g_subln)
        mix, = _attn_call("gqa", proj, mix, layer, True, (C_QB, C_KB, C_VB), 1024, 1,
                          caches=(cache_b_k, cache_b_v), tabs=tabs_bc, qk_gains=(g_qnorm, g_knorm))
        mix, = _attn_call("win", proj, mix, layer, True, (C_QC, C_KC, C_VC), 2048, 1,
                          caches=(cache_c_k, cache_c_v), tabs=tabs_bc, sink=sink)

        for latent in (False, True):
            xact = _conv_call(proj, conv_w, conv_b, layer, latent)
            ys = []
            for direction in (0, 1):
                res = _scan_call(xact, proj, dt_bias, a_log, state_ssm, layer, latent, direction,
                                 state_out=None if latent else ssm_new)
                ys.append(res[0])
                if not latent:
                    ssm_new = res[1]
            mix = _ssd_out_call(ys[0], ys[1], xact, proj, d_skip, g_ssm_norm, mix, layer, latent)

        if last:
            (mm,) = _mm_call(mix, w_out_b, F32, name="out_proj")
        else:
            mm, w_out_b = _mm_call(mix, w_out_b, F32, name="out_proj", cast=(w_out, layer + 1, D))
        x, h2 = _norm_call(x, mods4, mm=mm, post=(g_post_mix, layer, 2), nxt=(g_pre_ffn, layer, 4, 3))
        u, w_down_b = _mm_call(h2, w_up_b, BF16, relu2=True, name="ffn_up", cast=(w_down, layer, D_FF))
        if last:
            (mm,) = _mm_call(u, w_down_b, F32, name="ffn_down")
            x_ctx, x_lat = _norm_call(x, mods4, mm=mm, post=(g_post_ffn, layer, 5), split_out=True)
        else:
            mm, w_in_b = _mm_call(u, w_down_b, F32, name="ffn_down", cast=(w_in_t, layer + 1, IN_WP))
            x, h = _norm_call(x, mods4, mm=mm, post=(g_post_ffn, layer, 5), nxt=(g_pre_mix, layer + 1, 1, 0))

    new_kv = [a.reshape(BATCH, DEPTH, SEQ, -1, HD) for a in (*kv_a, *kv_b, *kv_c)]
    return (x_ctx.reshape(BATCH, SEQ, D), x_lat.reshape(DEC_BATCH, DEC_SEQ, D), *new_kv, ssm_new)
```

```python
import functools
import math

import numpy as np
import jax
import jax.numpy as jnp
from jax import lax
from jax.experimental import pallas as pl
from jax.experimental.pallas import tpu as pltpu

F32 = jnp.float32
BF16 = jnp.bfloat16

D = 4096
BATCH, SEQ = 16, 256
DEC_BATCH, DEC_SEQ = 4, 1024
PAST = 256
DEPTH = 2
GRID_W = 64
HD = 128
N_HEADS = 8
N_KV = 2
GQ = N_HEADS // N_KV
WINDOW = 128
QB = 128
SSM_H, SSM_P, SSM_N, SSM_G = 16, 64, 128, 2
SSM_INNER = SSM_H * SSM_P
CHUNK = 128
D_CONV = 5
CONV_DIM = SSM_INNER + 2 * SSM_G * SSM_N
D_FF = 4 * D
EPS = 1e-6
ROPE_BASE = 10000.0
N_MOD = 6
LOG2E = 1.4426950408889634

R_CTX = BATCH * SEQ
R_LAT = DEC_BATCH * DEC_SEQ
R = R_CTX + R_LAT

C_QA, C_KA, C_VA = 0, 1024, 2048
C_QB, C_KB, C_VB = 3072, 4096, 4352
C_QC, C_KC, C_VC = 4608, 5632, 5888
C_Z, C_XBC, C_DT = 6144, 7168, 8704
IN_W = C_DT + 2 * SSM_H
IN_WP = 9216

VMEM_LIMIT = 56 * 1024 * 1024


def _sds(shape, dtype):
    return jax.ShapeDtypeStruct(shape, dtype)


def _params(n_grid, vmem=VMEM_LIMIT):
    return pltpu.CompilerParams(dimension_semantics=("arbitrary",) * n_grid, vmem_limit_bytes=vmem)


def _rms(x, g):
    return x * lax.rsqrt(jnp.mean(x * x, axis=-1, keepdims=True) + EPS) * g


def _sigmoid(x):
    return 1.0 / (1.0 + jnp.exp(-x))


def _mod_row(i, tm):
    r = i * tm
    return jnp.where(r >= R_CTX, 1 + (r - R_CTX) // DEC_SEQ, 0)


def _linear_step(grid):
    step = pl.program_id(0)
    for axis in range(1, len(grid)):
        step = step * grid[axis] + pl.program_id(axis)
    return step


def _cast_plan(cast, grid):
    cw, layer, out_rows = cast
    rows, cols = cw.shape[1:]
    n_steps = math.prod(grid)
    common = math.gcd(rows, out_rows)
    rb = min(d for d in range(16, common + 1, 16) if common % d == 0 and out_rows // d <= n_steps)
    n_live, n_bands = rows // rb, out_rows // rb

    def step_of(*idx):
        step = idx[0]
        for axis in range(1, len(grid)):
            step = step * grid[axis] + idx[axis]
        return step

    return (pl.BlockSpec((None, rb, cols), lambda *idx: (layer, jnp.minimum(step_of(*idx), n_live - 1), 0)),
            pl.BlockSpec((rb, cols), lambda *idx: (jnp.minimum(step_of(*idx), n_bands - 1), 0)),
            _sds((out_rows, cols), BF16), (n_live, n_bands))


def _cast_band(cw_ref, cwo_ref, grid, bands):
    n_live, n_bands = bands
    if n_live == n_bands:
        cwo_ref[...] = cw_ref[...].astype(BF16)
        return
    step = _linear_step(grid)

    @pl.when(step < n_live)
    def _():
        cwo_ref[...] = cw_ref[...].astype(BF16)

    @pl.when(step >= n_live)
    def _():
        cwo_ref[...] = jnp.zeros(cwo_ref.shape, BF16)


def _mod_kernel(c_ref, w_ref, b_ref, cw_ref, o_ref, cwo_ref, *, grid, bands):
    c = c_ref[...]
    s = (c * _sigmoid(c)).astype(BF16)
    o_ref[...] = jnp.dot(s, w_ref[...].astype(BF16), preferred_element_type=F32) + b_ref[...]
    _cast_band(cw_ref, cwo_ref, grid, bands)


def _mod_call(cvec, w_mod, b_mod, cast):
    tn = 512
    grid = (DEPTH, N_MOD * D // tn)
    c_in, c_out, c_shape, bands = _cast_plan(cast, grid)
    return pl.pallas_call(
        functools.partial(_mod_kernel, grid=grid, bands=bands),
        grid=grid,
        in_specs=[pl.BlockSpec((8, D), lambda l, j: (0, 0)),
                  pl.BlockSpec((None, D, tn), lambda l, j: (l, 0, j)),
                  pl.BlockSpec((None, 1, tn), lambda l, j: (l, 0, j)), c_in],
        out_specs=[pl.BlockSpec((None, 8, tn), lambda l, j: (l, 0, j)), c_out],
        out_shape=[_sds((DEPTH, 8, N_MOD * D), F32), c_shape],
        compiler_params=_params(2),
        name="modulation",
    )(cvec, w_mod, b_mod.reshape(DEPTH, 1, N_MOD * D), cast[0])


TM_ROW = 256
N_CTX_TILES = R_CTX // TM_ROW


def _mod_spec(layer, chunk, tm):
    return pl.BlockSpec((None, None, 1, D), lambda i: (layer, _mod_row(i, tm), 0, chunk))


def _gain_spec(layer):
    return pl.BlockSpec((None, 1, D), lambda i: (layer, 0, 0))


_ROW = pl.BlockSpec((TM_ROW, D), lambda i: (i, 0))
_ROW_CTX = pl.BlockSpec((TM_ROW, D), lambda i: (jnp.minimum(i, N_CTX_TILES - 1), 0))
_ROW_LAT = pl.BlockSpec((TM_ROW, D), lambda i: (jnp.maximum(i - N_CTX_TILES, 0), 0))


def _norm_kernel(*refs, has_mm, split_in, with_next, split_out, cast_bands):
    it = iter(refs)
    if has_mm:
        mm_ref = next(it)
    x_refs = (next(it), next(it)) if split_in else (next(it),)
    if has_mm:
        gpost_ref, gt_ref = next(it), next(it)
    if with_next:
        gpre_ref, sc_ref, sh_ref = next(it), next(it), next(it)
    cw_refs = [next(it) for _ in cast_bands]
    if has_mm:
        xo_refs = (next(it), next(it)) if split_out else (next(it),)
    if with_next:
        h_ref = next(it)
    for cw_ref, bands in zip(cw_refs, cast_bands):
        _cast_band(cw_ref, next(it), (R // TM_ROW,), bands)

    def body(x_ref, xo_ref):
        x = x_ref[...]
        if has_mm:
            x = x + gt_ref[...] * _rms(mm_ref[...], gpost_ref[...])
            xo_ref[...] = x
        if with_next:
            h_ref[...] = (_rms(x, gpre_ref[...]) * (1.0 + sc_ref[...]) + sh_ref[...]).astype(BF16)

    if split_in or split_out:
        i = pl.program_id(0)

        @pl.when(i < N_CTX_TILES)
        def _():
            body(x_refs[0], xo_refs[0] if has_mm else None)

        @pl.when(i >= N_CTX_TILES)
        def _():
            body(x_refs[-1], xo_refs[-1] if has_mm else None)
    else:
        body(x_refs[0], xo_refs[0] if has_mm else None)


def _norm_call(x, mods4, mm=None, post=None, nxt=None, split_out=False, casts=()):
    split_in = isinstance(x, tuple)
    in_specs, args = [], []
    if mm is not None:
        in_specs.append(_ROW)
        args.append(mm)
    in_specs += [_ROW_CTX, _ROW_LAT] if split_in else [_ROW]
    args += list(x) if split_in else [x]
    out_specs, out_shape = [], []
    if mm is not None:
        g_post, pl_layer, gate_chunk = post
        in_specs += [_gain_spec(pl_layer), _mod_spec(pl_layer, gate_chunk, TM_ROW)]
        args += [g_post.reshape(DEPTH, 1, D), mods4]
        if split_out:
            out_specs += [_ROW_CTX, _ROW_LAT]
            out_shape += [_sds((R_CTX, D), F32), _sds((R_LAT, D), F32)]
        else:
            out_specs.append(_ROW)
            out_shape.append(_sds((R, D), F32))
    if nxt is not None:
        g_pre, nl, sc_chunk, sh_chunk = nxt
        in_specs += [_gain_spec(nl), _mod_spec(nl, sc_chunk, TM_ROW), _mod_spec(nl, sh_chunk, TM_ROW)]
        args += [g_pre.reshape(DEPTH, 1, D), mods4, mods4]
        out_specs.append(_ROW)
        out_shape.append(_sds((R, D), BF16))
    cast_bands = []
    for cast in casts:
        c_in, c_out, c_shape, bands = _cast_plan(cast, (R // TM_ROW,))
        in_specs.append(c_in)
        args.append(cast[0])
        out_specs.append(c_out)
        out_shape.append(c_shape)
        cast_bands.append(bands)
    return pl.pallas_call(
        functools.partial(_norm_kernel, has_mm=mm is not None, split_in=split_in, with_next=nxt is not None,
                          split_out=split_out, cast_bands=tuple(cast_bands)),
        grid=(R // TM_ROW,),
        in_specs=in_specs, out_specs=out_specs, out_shape=out_shape,
        compiler_params=_params(1),
        name="post_norm" if mm is not None else "pre_norm",
    )(*args)


def _mm_kernel(*refs, grid, relu2, w_rows_out, bands, zero_fill):
    it = iter(refs)
    a_ref, w_ref = next(it), next(it)
    cw_ref = next(it) if bands else None
    o_ref = next(it)
    if bands:
        _cast_band(cw_ref, next(it), grid, bands)
    if zero_fill:
        z_ref = next(it)
        z_ref[...] = jnp.zeros(z_ref.shape, z_ref.dtype)
    nk = grid[2]
    if w_rows_out:
        acc = _nt_dot(a_ref[...], w_ref[...])
    else:
        acc = jnp.dot(a_ref[...], w_ref[...], preferred_element_type=F32)
    if nk == 1:
        if relu2:
            u = jnp.maximum(acc, 0.0)
            acc = u * u
        o_ref[...] = acc.astype(o_ref.dtype)
    else:
        k = pl.program_id(2)

        @pl.when(k == 0)
        def _():
            o_ref[...] = acc

        @pl.when(k > 0)
        def _():
            o_ref[...] += acc


CAST_BAND_BYTES = 2 * 1024 * 1024


def _mm_tiles(m, n, kdim, cast):
    tm, tn, tk = 1024, 1024, 4096
    if cast is not None:
        steps = (m // tm) * (n // tn) * (kdim // tk)
        if cast[0].shape[1] * cast[0].shape[2] * 4 // steps > CAST_BAND_BYTES:
            tn //= 2
    return tm, tn, tk


def _mm_call(a, w, out_dtype, relu2=False, name="proj", cast=None, w_rows_out=False, zero_fill=None):
    m, kdim = a.shape
    n = w.shape[0] if w_rows_out else w.shape[1]
    tm, tn, tk = _mm_tiles(m, n, kdim, cast)
    nk = kdim // tk
    assert nk == 1 or (out_dtype == F32 and not relu2)
    grid = (m // tm, n // tn, nk)
    in_specs = [pl.BlockSpec((tm, tk), lambda i, j, k: (i, k)),
                pl.BlockSpec((tn, tk), lambda i, j, k: (j, k)) if w_rows_out else
                pl.BlockSpec((tk, tn), lambda i, j, k: (k, j))]
    args = [a, w]
    out_specs = [pl.BlockSpec((tm, tn), lambda i, j, k: (i, j))]
    out_shape = [_sds((m, n), out_dtype)]
    bands = None
    if cast is not None:
        c_in, c_out, c_shape, bands = _cast_plan(cast, grid)
        in_specs.append(c_in)
        args.append(cast[0])
        out_specs.append(c_out)
        out_shape.append(c_shape)
    if zero_fill is not None:
        (z_rows, z_cols), z_dtype = zero_fill
        n_steps = math.prod(grid)
        zb = min(d for d in range(16, z_rows + 1, 16) if z_rows % d == 0 and z_rows // d <= n_steps)
        nzb = z_rows // zb
        out_specs.append(pl.BlockSpec(
            (zb, z_cols), lambda i, j, k: (jnp.minimum((i * grid[1] + j) * grid[2] + k, nzb - 1), 0)))
        out_shape.append(_sds((z_rows, z_cols), z_dtype))
    return pl.pallas_call(
        functools.partial(_mm_kernel, grid=grid, relu2=relu2, w_rows_out=w_rows_out, bands=bands,
                          zero_fill=zero_fill is not None),
        grid=grid,
        in_specs=in_specs, out_specs=out_specs, out_shape=out_shape,
        compiler_params=_params(3),
        name=name,
    )(*args)


def _rope_tables(pair):
    t = np.arange(DEC_SEQ)
    row, col = t // GRID_W, t % GRID_W
    lane = np.arange(HD)
    da = 2 * pair
    inv_freq = 1.0 / (ROPE_BASE ** (np.arange(0, da, 2, dtype=np.float64) / da))
    pos = np.where((lane // da) % 2 == 0, row[:, None], col[:, None]).astype(np.float64)
    ang = pos * inv_freq[lane % pair][None, :]
    first = (lane % da) < pair
    cos = np.cos(ang)
    sin = np.sin(ang)
    sin_a = np.where(first[None, :], -sin, 0.0)
    sin_b = np.where(first[None, :], 0.0, sin)
    return tuple(jnp.asarray(a, F32) for a in (cos, sin_a, sin_b))


def _rope(x, cos, sin_a, sin_b, pair):
    return x * cos + pltpu.roll(x, HD - pair, 1) * sin_a + pltpu.roll(x, pair, 1) * sin_b


def _nt_dot(a, b):
    return lax.dot_general(a, b, (((1,), (1,)), ((), ())), preferred_element_type=F32)


def _exp2_rows(segs, sink_row=None):
    m = None
    for n, s in enumerate(segs):
        if n == 0 and sink_row is not None:
            mn = jnp.maximum(jnp.max(jnp.maximum(s[:, :HD], sink_row), axis=-1, keepdims=True),
                             jnp.max(s[:, HD:], axis=-1, keepdims=True))
        else:
            mn = jnp.max(s, axis=-1, keepdims=True)
        m = mn if m is None else jnp.maximum(m, mn)
    es = [jnp.exp2(s - m) for s in segs]
    den = sum(jnp.sum(e, axis=-1, keepdims=True) for e in es)
    if sink_row is not None:
        den = den + jnp.sum(jnp.exp2(sink_row - m), axis=-1, keepdims=True)
    return es, den


def _attn_kernel(*refs, kind, t_len, latent, n_kv, gq, qk_norm, use_sink, emit_kv, n_alias, lambda_init, stack):
    it = iter(refs)
    q_ref, k_ref, v_ref = next(it), next(it), next(it)
    if latent:
        ck_ref, cv_ref, cos_ref, sa_ref, sb_ref = next(it), next(it), next(it), next(it), next(it)
    if kind == "diff":
        lq1_ref, lk1_ref, lq2_ref, lk2_ref, gsub_ref = next(it), next(it), next(it), next(it), next(it)
    if qk_norm:
        gq_ref, gk_ref = next(it), next(it)
    if use_sink:
        sink_ref = next(it)
    for _ in range(n_alias):
        next(it)
    o_ref = next(it)
    if emit_kv:
        ko_ref, vo_ref = next(it), next(it)
    kb_scr, vb_scr = next(it), next(it)

    past = PAST if latent else 0
    pair = HD // 8 if kind == "diff" else HD // 4
    dh = HD // 2 if kind == "diff" else HD
    q_scale = dh ** -0.5 * LOG2E

    for kv in range(n_kv):
        kl = slice(kv * HD, (kv + 1) * HD)
        k = k_ref[:, kl]
        v = v_ref[:, kl]
        if qk_norm:
            k = _rms(k, gk_ref[...])
        if emit_kv:
            ko_ref[:, kl] = k
            vo_ref[:, kl] = v
        if latent:
            k = _rope(k, cos_ref[...], sa_ref[...], sb_ref[...], pair)
            kb_scr[kv, 0:past, :] = ck_ref[:, kl].astype(BF16)
            vb_scr[kv, 0:past, :] = cv_ref[:, kl].astype(BF16)
        kb_scr[kv, past:past + t_len, :] = k.astype(BF16)
        vb_scr[kv, past:past + t_len, :] = v.astype(BF16)

    if kind == "diff":
        lam = (jnp.exp(jnp.sum(lq1_ref[...] * lk1_ref[...], axis=-1, keepdims=True))
               - jnp.exp(jnp.sum(lq2_ref[...] * lk2_ref[...], axis=-1, keepdims=True)) + lambda_init)
        first_map = lax.broadcasted_iota(jnp.int32, (1, HD), 1) < HD // 2

    def scores(qh, qb, kv, sink):
        if kind == "win":
            w_len = 3 * QB
            start = jnp.clip((qb - 1) * QB, 0, t_len - w_len)
            loc = pl.ds(pl.multiple_of(past + start, QB), w_len)
            s_ctx = _nt_dot(qh, kb_scr[kv, 0:past, :])
            s_loc = _nt_dot(qh, kb_scr[kv, loc, :])
            qpos = qb * QB + lax.broadcasted_iota(jnp.int32, (qh.shape[0], 1), 0) % QB
            kpos = start + lax.broadcasted_iota(jnp.int32, (1, w_len), 1)
            s_loc = jnp.where(jnp.abs(qpos - kpos) <= WINDOW, s_loc, -1e30)
            es, den = _exp2_rows([s_ctx, s_loc], sink)
            return es, den, (slice(0, past), loc)
        es, den = _exp2_rows([_nt_dot(qh, kb_scr[kv])], sink)
        return es, den, (slice(None),)

    def values(es, segs, kv):
        return sum(jnp.dot(e.astype(BF16), vb_scr[kv, seg, :], preferred_element_type=F32)
                   for e, seg in zip(es, segs))

    def unit_scores(qb, kv, gs):
        rows = pl.ds(qb * QB, QB) if isinstance(qb, int) else pl.ds(pl.multiple_of(qb * QB, QB), QB)
        head_lanes = [slice((kv * gq + g) * HD, (kv * gq + g + 1) * HD) for g in gs]
        qs = []
        for lanes in head_lanes:
            q = q_ref[rows, lanes]
            if qk_norm:
                q = _rms(q, gq_ref[...])
            if latent:
                q = _rope(q, cos_ref[rows, :], sa_ref[rows, :], sb_ref[rows, :], pair)
            q = q * q_scale
            if kind == "diff":
                qs += [jnp.where(first_map, q, 0.0).astype(BF16), jnp.where(first_map, 0.0, q).astype(BF16)]
            else:
                qs.append(q.astype(BF16))
        sink = None
        if use_sink:
            sink = [jnp.broadcast_to(sink_ref[kv * gq + g] * LOG2E, (QB, HD)) for g in gs]
            sink = jnp.concatenate(sink, axis=0) if len(sink) > 1 else sink[0]
        if stack or len(qs) == 1:
            parts = [scores(jnp.concatenate(qs, axis=0) if len(qs) > 1 else qs[0], qb, kv, sink)]
        else:
            parts = [scores(q, qb, kv, sink) for q in qs]
        return rows, head_lanes, kv, parts

    def unit_values(state):
        rows, head_lanes, kv, parts = state
        outs = [(values(es, segs, kv), den) for es, den, segs in parts]
        if kind == "diff":
            if len(outs) == 1:
                (o, den), = outs
                o1, d1, o2, d2 = o[:QB], den[:QB], o[QB:], den[QB:]
            else:
                (o1, d1), (o2, d2) = outs
            o = o1 * (1.0 / d1) - o2 * (lam / d2)
            o_ref[rows, head_lanes[0]] = (_rms(o, gsub_ref[...]) * (1.0 - lambda_init)).astype(BF16)
            return
        (o, den), = outs
        o = o * (1.0 / den)
        for n, lanes in enumerate(head_lanes):
            o_ref[rows, lanes] = o[n * QB:(n + 1) * QB].astype(BF16)

    n_blk = t_len // QB
    head_groups = (tuple(range(gq)),) if stack else tuple((g,) for g in range(gq))
    per_iter = min(n_blk, 4 if stack and gq > 1 else 8)

    def blocks(qbs, kv, gs):
        if latent:
            for st in [unit_scores(qb, kv, gs) for qb in qbs]:
                unit_values(st)
        else:
            for qb in qbs:
                unit_values(unit_scores(qb, kv, gs))

    for kv in range(n_kv):
        for gs in head_groups:
            if n_blk == per_iter:
                blocks(range(n_blk), kv, gs)
            else:
                def some_blocks(i, carry, kv=kv, gs=gs):
                    blocks([per_iter * i + n for n in range(per_iter)], kv, gs)
                    return carry

                lax.fori_loop(0, n_blk // per_iter, some_blocks, 0)


def _attn_call(kind, proj, mix, layer, latent, cols, mix_col, n_kv, caches=None, tabs=None, lam_vecs=None,
               g_subln=None, qk_gains=None, sink=None, kv_out=None):
    t_len, nb, rb0 = (DEC_SEQ, DEC_BATCH, R_CTX // DEC_SEQ) if latent else (SEQ, BATCH, 0)
    gq = 1 if kind == "diff" else GQ
    kv_total = N_HEADS // gq
    c_q, c_k, c_v = cols
    qw, kw = n_kv * gq * HD, n_kv * HD
    grid = (nb, kv_total // n_kv)

    def tok(width, c0):
        return pl.BlockSpec((t_len, width), lambda b, h: (rb0 + b, c0 // width + h))

    in_specs = [tok(qw, c_q), tok(kw, c_k), tok(kw, c_v)]
    args = [proj, proj, proj]
    if latent:
        cache = pl.BlockSpec((None, None, PAST, kw), lambda b, h: (b, layer, 0, h))
        tab = pl.BlockSpec((t_len, HD), lambda b, h: (0, 0))
        in_specs += [cache, cache, tab, tab, tab]
        args += [c.reshape(DEC_BATCH, DEPTH, PAST, kv_total * HD) for c in caches] + list(tabs)
    if kind == "diff":
        vec = pl.BlockSpec((None, 1, HD // 2), lambda b, h: (layer, 0, 0))
        in_specs += [vec] * 4 + [pl.BlockSpec((None, 1, HD), lambda b, h: (layer, 0, 0))]
        args += [v.reshape(DEPTH, 1, HD // 2) for v in lam_vecs] + [g_subln.reshape(DEPTH, 1, HD)]
    if qk_gains is not None:
        gain = pl.BlockSpec((None, 1, HD), lambda b, h: (layer, 0, 0))
        in_specs += [gain, gain]
        args += [g.reshape(DEPTH, 1, HD) for g in qk_gains]
    if sink is not None:
        in_specs.append(pl.BlockSpec((None, n_kv * GQ, 1, HD), lambda b, h: (layer, h, 0, 0)))
        args.append(jnp.pad(sink.reshape(DEPTH, N_HEADS, 1, 1), ((0, 0), (0, 0), (0, 0), (0, HD - 1)),
                            constant_values=-1e30))
    aliases = {len(args): 0}
    in_specs.append(pl.BlockSpec(memory_space=pl.ANY))
    args.append(mix)
    out_specs = [tok(qw, mix_col)]
    out_shape = [_sds((R, D), BF16)]
    emit_kv = kv_out is not None
    if emit_kv:
        for n, prev in enumerate(kv_out):
            aliases[len(args)] = 1 + n
            in_specs.append(pl.BlockSpec(memory_space=pl.ANY))
            args.append(prev)
        kv_spec = pl.BlockSpec((None, None, SEQ, kw), lambda b, h: (b, layer, 0, h))
        out_specs += [kv_spec, kv_spec]
        out_shape += [_sds((BATCH, DEPTH, SEQ, kv_total * HD), F32)] * 2
    s_len = t_len + (PAST if latent else 0)
    return pl.pallas_call(
        functools.partial(_attn_kernel, kind=kind, t_len=t_len, latent=latent, n_kv=n_kv, gq=gq,
                          qk_norm=qk_gains is not None, use_sink=sink is not None, emit_kv=emit_kv,
                          n_alias=len(aliases), lambda_init=0.8 - 0.6 * math.exp(-0.3 * layer),
                          stack=not latent or kind == "win"),
        grid=grid,
        in_specs=in_specs, out_specs=out_specs, out_shape=out_shape,
        scratch_shapes=[pltpu.VMEM((n_kv, s_len, HD), BF16), pltpu.VMEM((n_kv, s_len, HD), BF16)],
        input_output_aliases=aliases,
        compiler_params=_params(2),
        name=kind + ("_lat" if latent else "_ctx"),
    )(*args)


def _conv_kernel(x_ref, w_ref, b_ref, o_ref, *, t_len):
    u = x_ref[...]
    w = w_ref[...]
    n_rows = u.shape[0]
    t = lax.broadcasted_iota(jnp.int32, (n_rows, 1), 0) % t_len
    half = D_CONV // 2
    acc = u * w[half:half + 1, :] + b_ref[...]
    for s in range(1, half + 1):
        before = jnp.where(t >= s, pltpu.roll(u, s, 0), 0.0)
        after = jnp.where(t < t_len - s, pltpu.roll(u, n_rows - s, 0), 0.0)
        acc = acc + before * w[half - s:half - s + 1, :] + after * w[half + s:half + s + 1, :]
    o_ref[...] = acc * _sigmoid(acc)


def _conv_call(proj, conv_w, conv_b, layer, latent):
    t_len, n_rows, row0 = (DEC_SEQ, R_LAT, R_CTX) if latent else (SEQ, R_CTX, 0)
    tc, tr = 512, 1024
    rb0 = row0 // tr
    return pl.pallas_call(
        functools.partial(_conv_kernel, t_len=t_len),
        grid=(n_rows // tr, CONV_DIM // tc),
        in_specs=[pl.BlockSpec((tr, tc), lambda b, j: (rb0 + b, C_XBC // tc + j)),
                  pl.BlockSpec((None, D_CONV, tc), lambda b, j: (layer, 0, j)),
                  pl.BlockSpec((None, 1, tc), lambda b, j: (layer, 0, j))],
        out_specs=pl.BlockSpec((tr, tc), lambda b, j: (b, j)),
        out_shape=_sds((n_rows, CONV_DIM), F32),
        compiler_params=_params(2),
        name="conv_lat" if latent else "conv_ctx",
    )(proj, conv_w, conv_b.reshape(DEPTH, 1, CONV_DIM))


def _split3(v):
    hi = v.astype(BF16).astype(F32)
    r1 = v - hi
    mid = r1.astype(BF16).astype(F32)
    lo = (r1 - mid).astype(BF16).astype(F32)
    return hi, mid, lo


def _exact_dot(tri, v, tri_first):
    parts = [p.astype(BF16) for p in _split3(v)]
    if tri_first:
        return sum(jnp.dot(tri, p, preferred_element_type=F32) for p in parts)
    return sum(jnp.dot(p, tri, preferred_element_type=F32) for p in parts)


def _lane_spread(v, width):
    hi, mid, lo = _split3(v)
    packed = (hi + pltpu.roll(mid, SSM_H, 1) + pltpu.roll(lo, 2 * SSM_H, 1)).astype(BF16)
    n = SSM_H * width
    src = lax.broadcasted_iota(jnp.int32, (HD, n), 0)
    dst = lax.broadcasted_iota(jnp.int32, (HD, n), 1)
    sel = jnp.where((src < 3 * SSM_H) & (src % SSM_H == dst // width), 1.0, 0.0).astype(BF16)
    return jnp.dot(packed, sel, preferred_element_type=F32)


def _softplus(x):
    return jnp.maximum(x, 0.0) + jnp.log1p(jnp.exp(-jnp.abs(x)))


def _scan_kernel(*refs, direction, has_h0, emit_state, n_chunks, n_alias):
    it = iter(refs)
    xa_ref, dt_ref, bias_ref, alog_ref = (next(it) for _ in range(4))
    if has_h0:
        h0_ref = next(it)
    for _ in range(n_alias):
        next(it)
    y_ref = next(it)
    if emit_state:
        hend_ref = next(it)
    ht_scr = next(it)
    c = pl.program_id(1)

    @pl.when(c == 0)
    def _():
        if has_h0:
            ht_scr[...] = h0_ref[...].reshape(SSM_INNER, SSM_N).T
        else:
            ht_scr[...] = jnp.zeros(ht_scr.shape, F32)

    ii = lax.broadcasted_iota(jnp.int32, (CHUNK, CHUNK), 0)
    jj = lax.broadcasted_iota(jnp.int32, (CHUNK, CHUNK), 1)
    if direction == 0:
        keep, keep_t = jj <= ii, ii <= jj
        edge = CHUNK - 1
    else:
        keep, keep_t = jj >= ii, ii >= jj
        edge = 0
    tri_col = jnp.where(keep, 1.0, 0.0).astype(BF16)
    tri_row = jnp.where(keep_t, 1.0, 0.0).astype(BF16)

    raw = dt_ref[...]
    if direction == 1:
        raw = pltpu.roll(raw, HD - SSM_H, 1)
    head_lane = lax.broadcasted_iota(jnp.int32, (1, HD), 1) < SSM_H
    dt_c = jnp.where(head_lane, _softplus(raw + bias_ref[...]), 0.0)
    da_c = dt_c * (-jnp.exp(alog_ref[...]))
    a_col = _exact_dot(tri_col, da_c, True)
    a_row = _exact_dot(tri_row, da_c.T[0:SSM_H, :], False)

    a_sq = _lane_spread(a_col, CHUNK)
    a_hp = _lane_spread(a_col, SSM_P)
    dt_hp = _lane_spread(dt_c, SSM_P)

    xa = xa_ref[...]
    xdt = xa[:, :SSM_INNER] * dt_hp
    a_tot = a_hp[edge:edge + 1, :]
    grow = jnp.exp(a_hp)
    w_end = (xdt * jnp.exp(a_tot - a_hp)).astype(BF16)
    carry = jnp.exp(a_tot)
    lower_half = lax.broadcasted_iota(jnp.int32, (1, 2 * SSM_P), 1) < SSM_P

    hg = SSM_H // SSM_G
    gw = hg * SSM_P
    for g in range(SSM_G):
        bm = xa[:, SSM_INNER + g * SSM_N:SSM_INNER + (g + 1) * SSM_N].astype(BF16)
        cm = xa[:, SSM_INNER + (SSM_G + g) * SSM_N:SSM_INNER + (SSM_G + g + 1) * SSM_N].astype(BF16)
        cb = _nt_dot(cm, bm)
        gl = slice(g * gw, (g + 1) * gw)
        ht = ht_scr[:, gl]
        y_off = jnp.dot(cm, ht.astype(BF16), preferred_element_type=F32) * grow[:, gl]
        for pr in range(hg // 2):
            pl_ = slice(g * gw + pr * 2 * SSM_P, g * gw + (pr + 1) * 2 * SSM_P)
            x_pair = xdt[:, pl_]
            y_pair = y_off[:, pr * 2 * SSM_P:(pr + 1) * 2 * SSM_P]
            for half in range(2):
                h = g * hg + pr * 2 + half
                decay = jnp.exp(jnp.where(keep, a_sq[:, h * CHUNK:(h + 1) * CHUNK] - a_row[h:h + 1, :], -1e30))
                rhs = jnp.where(lower_half if half == 0 else ~lower_half, x_pair, 0.0).astype(BF16)
                y_pair = y_pair + jnp.dot((cb * decay).astype(BF16), rhs, preferred_element_type=F32)
            y_ref[:, pl_] = y_pair
        st = lax.dot_general(bm, w_end[:, gl], (((0,), (0,)), ((), ())), preferred_element_type=F32)
        ht_scr[:, gl] = ht * carry[:, gl] + st

    if emit_state:
        @pl.when(c == n_chunks - 1)
        def _():
            hend_ref[...] = ht_scr[...].T.reshape(SSM_H, SSM_P, SSM_N)


def _pad_lanes(v):
    return jnp.pad(v, ((0, 0), (0, 0), (0, HD - SSM_H))).reshape(DEPTH, 2, 1, HD)


def _scan_call(xact, proj, dt_bias, a_log, state, layer, latent, direction, state_out=None):
    t_len, nb, row0 = (DEC_SEQ, DEC_BATCH, R_CTX) if latent else (SEQ, BATCH, 0)
    nc = t_len // CHUNK
    rb0 = row0 // CHUNK

    def chunk(c):
        return c if direction == 0 else nc - 1 - c

    row_vec = pl.BlockSpec((None, None, 1, HD), lambda b, c: (layer, direction, 0, 0))
    in_specs = [pl.BlockSpec((CHUNK, CONV_DIM), lambda b, c: (b * nc + chunk(c), 0)),
                pl.BlockSpec((CHUNK, HD), lambda b, c: (rb0 + b * nc + chunk(c), C_DT // HD)),
                row_vec, row_vec]
    args = [xact, proj, _pad_lanes(dt_bias), _pad_lanes(a_log)]
    if latent:
        in_specs.append(pl.BlockSpec((None, None, None, SSM_H, SSM_P, SSM_N),
                                     lambda b, c: (b, layer, direction, 0, 0, 0)))
        args.append(state)
    out_specs = [pl.BlockSpec((CHUNK, SSM_INNER), lambda b, c: (b * nc + chunk(c), 0))]
    out_shape = [_sds((nb * t_len, SSM_INNER), F32)]
    aliases = {}
    emit_state = state_out is not None
    if emit_state:
        aliases[len(args)] = 1
        in_specs.append(pl.BlockSpec(memory_space=pl.ANY))
        args.append(state_out)
        out_specs.append(pl.BlockSpec((None, None, None, SSM_H, SSM_P, SSM_N),
                                      lambda b, c: (b, layer, direction, 0, 0, 0)))
        out_shape.append(_sds((nb, DEPTH, 2, SSM_H, SSM_P, SSM_N), F32))
    return pl.pallas_call(
        functools.partial(_scan_kernel, direction=direction, has_h0=latent, emit_state=emit_state, n_chunks=nc,
                          n_alias=len(aliases)),
        grid=(nb, nc),
        in_specs=in_specs, out_specs=out_specs, out_shape=out_shape,
        scratch_shapes=[pltpu.VMEM((SSM_N, SSM_INNER), F32)],
        input_output_aliases=aliases,
        compiler_params=_params(2),
        name=f"ssd_{'lat' if latent else 'ctx'}_{'fwd' if direction == 0 else 'bwd'}",
    )(*args)


def _ssd_out_kernel(yf_ref, yb_ref, xs_ref, z_ref, dskip_ref, g_ref, mix_ref, o_ref):
    del mix_ref
    dsk = dskip_ref[0:1, :] + dskip_ref[1:2, :]
    z = z_ref[...]
    y = (yf_ref[...] + yb_ref[...] + dsk * xs_ref[...]) * (z * _sigmoid(z))
    gw = SSM_INNER // SSM_G
    g = g_ref[...]
    for k in range(SSM_G):
        lanes = slice(k * gw, (k + 1) * gw)
        yk = y[:, lanes]
        yk = yk * lax.rsqrt(jnp.mean(yk * yk, axis=-1, keepdims=True) + EPS)
        o_ref[:, lanes] = (yk * g[:, lanes]).astype(BF16)


def _ssd_out_call(yf, yb, xact, proj, d_skip, g_norm, mix, layer, latent):
    tm = 256
    rb0 = (R_CTX if latent else 0) // tm
    n_rows = yf.shape[0]
    local = pl.BlockSpec((tm, SSM_INNER), lambda i: (i, 0))
    return pl.pallas_call(
        _ssd_out_kernel,
        grid=(n_rows // tm,),
        in_specs=[local, local, local,
                  pl.BlockSpec((tm, SSM_INNER), lambda i: (rb0 + i, C_Z // SSM_INNER)),
                  pl.BlockSpec((None, 2, SSM_INNER), lambda i: (layer, 0, 0)),
                  pl.BlockSpec((None, 1, SSM_INNER), lambda i: (layer, 0, 0)),
                  pl.BlockSpec(memory_space=pl.ANY)],
        out_specs=pl.BlockSpec((tm, SSM_INNER), lambda i: (rb0 + i, 3)),
        out_shape=_sds((R, D), BF16),
        input_output_aliases={6: 0},
        compiler_params=_params(1),
        name="ssd_out_lat" if latent else "ssd_out_ctx",
    )(yf, yb, xact, proj, jnp.repeat(d_skip, SSM_P, axis=-1), g_norm.reshape(DEPTH, 1, SSM_INNER), mix)


def kernel(x_prompt, x_sample, c, cache_a_k, cache_a_v, cache_b_k, cache_b_v, cache_c_k, cache_c_v, state_ssm, c_ctx, w_mod, b_mod, g_pre_mix, g_post_mix, g_pre_ffn, g_post_ffn, w_in, w_out, lam_q1, lam_k1, lam_q2, lam_k2, g_subln, g_qnorm, g_knorm, sink, conv_w, conv_b, dt_bias, a_log, d_skip, g_ssm_norm, w_up, w_down):
    x = (x_prompt.reshape(R_CTX, D), x_sample.reshape(R_LAT, D))
    cvec = jnp.concatenate([c_ctx[None, :], c, jnp.zeros((8 - 1 - DEC_BATCH, D), F32)], axis=0)
    w_in_t = jnp.swapaxes(w_in, 1, 2)
    mods, w_in_b = _mod_call(cvec, w_mod, b_mod, (w_in_t, 0, IN_WP))
    mods4 = mods.reshape(DEPTH, 8, 1, N_MOD * D)

    tabs_a = _rope_tables(HD // 8)
    tabs_bc = _rope_tables(HD // 4)
    lam_vecs = (lam_q1, lam_k1, lam_q2, lam_k2)

    h, w_out_b = _norm_call(x, mods4, nxt=(g_pre_mix, 0, 1, 0), casts=((w_out, 0, D),))
    kv_a = [jnp.zeros((BATCH, DEPTH, SEQ, N_HEADS * HD), F32) for _ in range(2)]
    kv_b = [jnp.zeros((BATCH, DEPTH, SEQ, N_KV * HD), F32) for _ in range(2)]
    kv_c = [jnp.zeros((BATCH, DEPTH, SEQ, N_KV * HD), F32) for _ in range(2)]
    ssm_new = jnp.zeros((BATCH, DEPTH, 2, SSM_H, SSM_P, SSM_N), F32)
    for layer in range(DEPTH):
        last = layer + 1 == DEPTH
        proj, w_up_b, mix = _mm_call(h, w_in_b, F32, name="in_proj", cast=(w_up, layer, D), w_rows_out=True,
                                     zero_fill=((R, D), BF16))

        mix, *kv_a = _attn_call("diff", proj, mix, layer, False, (C_QA, C_KA, C_VA), 0, N_HEADS,
                                lam_vecs=lam_vecs, g_subln=g_subln, kv_out=kv_a)
        mix, *kv_b = _attn_call("gqa", proj, mix, layer, False, (C_QB, C_KB, C_VB), 1024, N_KV,
                                qk_gains=(g_qnorm, g_knorm), kv_out=kv_b)
        mix, *kv_c = _attn_call("gqa", proj, mix, layer, False, (C_QC, C_KC, C_VC), 2048, 1,
                                sink=sink, kv_out=kv_c)
        mix, = _attn_call("diff", proj, mix, layer, True, (C_QA, C_KA, C_VA), 0, 1,
                          caches=(cache_a_k, cache_a_v), tabs=tabs_a, lam_vecs=lam_vecs, g_subln=g_subln)
        mix, = _attn_call("gqa", proj, mix, layer, True, (C_QB, C_KB, C_VB), 1024, 1,
                          caches=(cache_b_k, cache_b_v), tabs=tabs_bc, qk_gains=(g_qnorm, g_knorm))
        mix, = _attn_call("win", proj, mix, layer, True, (C_QC, C_KC, C_VC), 2048, 1,
                          caches=(cache_c_k, cache_c_v), tabs=tabs_bc, sink=sink)

        for latent in (False, True):
            xact = _conv_call(proj, conv_w, conv_b, layer, latent)
            ys = []
            for direction in (0, 1):
                res = _scan_call(xact, proj, dt_bias, a_log, state_ssm, layer, latent, direction,
                                 state_out=None if latent else ssm_new)
                ys.append(res[0])
                if not latent:
                    ssm_new = res[1]
            mix = _ssd_out_call(ys[0], ys[1], xact, proj, d_skip, g_ssm_norm, mix, layer, latent)

        if last:
            (mm,) = _mm_call(mix, w_out_b, F32, name="out_proj")
        else:
            mm, w_out_b = _mm_call(mix, w_out_b, F32, name="out_proj", cast=(w_out, layer + 1, D))
        x, h2 = _norm_call(x, mods4, mm=mm, post=(g_post_mix, layer, 2), nxt=(g_pre_ffn, layer, 4, 3))
        u, w_down_b = _mm_call(h2, w_up_b, BF16, relu2=True, name="ffn_up", cast=(w_down, layer, D_FF))
        if last:
            (mm,) = _mm_call(u, w_down_b, F32, name="ffn_down")
            x_ctx, x_lat = _norm_call(x, mods4, mm=mm, post=(g_post_ffn, layer, 5), split_out=True)
        else:
            mm, w_in_b = _mm_call(u, w_down_b, F32, name="ffn_down", cast=(w_in_t, layer + 1, IN_WP))
            x, h = _norm_call(x, mods4, mm=mm, post=(g_post_ffn, layer, 5), nxt=(g_pre_mix, layer + 1, 1, 0))

    new_kv = [a.reshape(BATCH, DEPTH, SEQ, -1, HD) for a in (*kv_a, *kv_b, *kv_c)]
    return (x_ctx.reshape(BATCH, SEQ, D), x_lat.reshape(DEC_BATCH, DEC_SEQ, D), *new_kv, ssm_new)
```

```python
import functools
import math

import numpy as np
import jax
import jax.numpy as jnp
from jax import lax
from jax.experimental import pallas as pl
from jax.experimental.pallas import tpu as pltpu

F32 = jnp.float32
BF16 = jnp.bfloat16

D = 4096
BATCH, SEQ = 16, 256
DEC_BATCH, DEC_SEQ = 4, 1024
PAST = 256
DEPTH = 2
GRID_W = 64
HD = 128
N_HEADS = 8
N_KV = 2
GQ = N_HEADS // N_KV
WINDOW = 128
QB = 128
SSM_H, SSM_P, SSM_N, SSM_G = 16, 64, 128, 2
SSM_INNER = SSM_H * SSM_P
CHUNK = 128
D_CONV = 5
CONV_DIM = SSM_INNER + 2 * SSM_G * SSM_N
D_FF = 4 * D
EPS = 1e-6
ROPE_BASE = 10000.0
N_MOD = 6
LOG2E = 1.4426950408889634

R_CTX = BATCH * SEQ
R_LAT = DEC_BATCH * DEC_SEQ
R = R_CTX + R_LAT

C_QA, C_KA, C_VA = 0, 1024, 2048
C_QB, C_KB, C_VB = 3072, 4096, 4352
C_QC, C_KC, C_VC = 4608, 5632, 5888
C_Z, C_XBC, C_DT = 6144, 7168, 8704
IN_W = C_DT + 2 * SSM_H
IN_WP = 9216

VMEM_LIMIT = 56 * 1024 * 1024


def _sds(shape, dtype):
    return jax.ShapeDtypeStruct(shape, dtype)


def _params(n_grid, vmem=VMEM_LIMIT):
    return pltpu.CompilerParams(dimension_semantics=("arbitrary",) * n_grid, vmem_limit_bytes=vmem)


def _rms(x, g):
    return x * lax.rsqrt(jnp.mean(x * x, axis=-1, keepdims=True) + EPS) * g


def _sigmoid(x):
    return 1.0 / (1.0 + jnp.exp(-x))


def _mod_row(i, tm):
    r = i * tm
    return jnp.where(r >= R_CTX, 1 + (r - R_CTX) // DEC_SEQ, 0)


def _linear_step(grid):
    step = pl.program_id(0)
    for axis in range(1, len(grid)):
        step = step * grid[axis] + pl.program_id(axis)
    return step


def _cast_plan(cast, grid):
    cw, layer, out_rows = cast
    rows, cols = cw.shape[1:]
    n_steps = math.prod(grid)
    common = math.gcd(rows, out_rows)
    rb = min(d for d in range(16, common + 1, 16) if common % d == 0 and out_rows // d <= n_steps)
    n_live, n_bands = rows // rb, out_rows // rb

    def step_of(*idx):
        step = idx[0]
        for axis in range(1, len(grid)):
            step = step * grid[axis] + idx[axis]
        return step

    return (pl.BlockSpec((None, rb, cols), lambda *idx: (layer, jnp.minimum(step_of(*idx), n_live - 1), 0)),
            pl.BlockSpec((rb, cols), lambda *idx: (jnp.minimum(step_of(*idx), n_bands - 1), 0)),
            _sds((out_rows, cols), BF16), (n_live, n_bands))


def _cast_band(cw_ref, cwo_ref, grid, bands):
    n_live, n_bands = bands
    if n_live == n_bands:
        cwo_ref[...] = cw_ref[...].astype(BF16)
        return
    step = _linear_step(grid)

    @pl.when(step < n_live)
    def _():
        cwo_ref[...] = cw_ref[...].astype(BF16)

    @pl.when(step >= n_live)
    def _():
        cwo_ref[...] = jnp.zeros(cwo_ref.shape, BF16)


def _mod_kernel(c_ref, w_ref, b_ref, cw_ref, o_ref, cwo_ref, *, grid, bands):
    c = c_ref[...]
    s = (c * _sigmoid(c)).astype(BF16)
    o_ref[...] = jnp.dot(s, w_ref[...].astype(BF16), preferred_element_type=F32) + b_ref[...]
    _cast_band(cw_ref, cwo_ref, grid, bands)


def _mod_call(cvec, w_mod, b_mod, cast):
    tn = 512
    grid = (DEPTH, N_MOD * D // tn)
    c_in, c_out, c_shape, bands = _cast_plan(cast, grid)
    return pl.pallas_call(
        functools.partial(_mod_kernel, grid=grid, bands=bands),
        grid=grid,
        in_specs=[pl.BlockSpec((8, D), lambda l, j: (0, 0)),
                  pl.BlockSpec((None, D, tn), lambda l, j: (l, 0, j)),
                  pl.BlockSpec((None, 1, tn), lambda l, j: (l, 0, j)), c_in],
        out_specs=[pl.BlockSpec((None, 8, tn), lambda l, j: (l, 0, j)), c_out],
        out_shape=[_sds((DEPTH, 8, N_MOD * D), F32), c_shape],
        compiler_params=_params(2),
        name="modulation",
    )(cvec, w_mod, b_mod.reshape(DEPTH, 1, N_MOD * D), cast[0])


TM_ROW = 256
N_CTX_TILES = R_CTX // TM_ROW


def _mod_spec(layer, chunk, tm):
    return pl.BlockSpec((None, None, 1, D), lambda i: (layer, _mod_row(i, tm), 0, chunk))


def _gain_spec(layer):
    return pl.BlockSpec((None, 1, D), lambda i: (layer, 0, 0))


_ROW = pl.BlockSpec((TM_ROW, D), lambda i: (i, 0))
_ROW_CTX = pl.BlockSpec((TM_ROW, D), lambda i: (jnp.minimum(i, N_CTX_TILES - 1), 0))
_ROW_LAT = pl.BlockSpec((TM_ROW, D), lambda i: (jnp.maximum(i - N_CTX_TILES, 0), 0))


def _norm_kernel(*refs, has_mm, split_in, with_next, split_out, cast_bands):
    it = iter(refs)
    if has_mm:
        mm_ref = next(it)
    x_refs = (next(it), next(it)) if split_in else (next(it),)
    if has_mm:
        gpost_ref, gt_ref = next(it), next(it)
    if with_next:
        gpre_ref, sc_ref, sh_ref = next(it), next(it), next(it)
    cw_refs = [next(it) for _ in cast_bands]
    if has_mm:
        xo_refs = (next(it), next(it)) if split_out else (next(it),)
    if with_next:
        h_ref = next(it)
    for cw_ref, bands in zip(cw_refs, cast_bands):
        _cast_band(cw_ref, next(it), (R // TM_ROW,), bands)

    def body(x_ref, xo_ref):
        x = x_ref[...]
        if has_mm:
            x = x + gt_ref[...] * _rms(mm_ref[...], gpost_ref[...])
            xo_ref[...] = x
        if with_next:
            h_ref[...] = (_rms(x, gpre_ref[...]) * (1.0 + sc_ref[...]) + sh_ref[...]).astype(BF16)

    if split_in or split_out:
        i = pl.program_id(0)

        @pl.when(i < N_CTX_TILES)
        def _():
            body(x_refs[0], xo_refs[0] if has_mm else None)

        @pl.when(i >= N_CTX_TILES)
        def _():
            body(x_refs[-1], xo_refs[-1] if has_mm else None)
    else:
        body(x_refs[0], xo_refs[0] if has_mm else None)


def _norm_call(x, mods4, mm=None, post=None, nxt=None, split_out=False, casts=()):
    split_in = isinstance(x, tuple)
    in_specs, args = [], []
    if mm is not None:
        in_specs.append(_ROW)
        args.append(mm)
    in_specs += [_ROW_CTX, _ROW_LAT] if split_in else [_ROW]
    args += list(x) if split_in else [x]
    out_specs, out_shape = [], []
    if mm is not None:
        g_post, pl_layer, gate_chunk = post
        in_specs += [_gain_spec(pl_layer), _mod_spec(pl_layer, gate_chunk, TM_ROW)]
        args += [g_post.reshape(DEPTH, 1, D), mods4]
        if split_out:
            out_specs += [_ROW_CTX, _ROW_LAT]
            out_shape += [_sds((R_CTX, D), F32), _sds((R_LAT, D), F32)]
        else:
            out_specs.append(_ROW)
            out_shape.append(_sds((R, D), F32))
    if nxt is not None:
        g_pre, nl, sc_chunk, sh_chunk = nxt
        in_specs += [_gain_spec(nl), _mod_spec(nl, sc_chunk, TM_ROW), _mod_spec(nl, sh_chunk, TM_ROW)]
        args += [g_pre.reshape(DEPTH, 1, D), mods4, mods4]
        out_specs.append(_ROW)
        out_shape.append(_sds((R, D), BF16))
    cast_bands = []
    for cast in casts:
        c_in, c_out, c_shape, bands = _cast_plan(cast, (R // TM_ROW,))
        in_specs.append(c_in)
        args.append(cast[0])
        out_specs.append(c_out)
        out_shape.append(c_shape)
        cast_bands.append(bands)
    return pl.pallas_call(
        functools.partial(_norm_kernel, has_mm=mm is not None, split_in=split_in, with_next=nxt is not None,
                          split_out=split_out, cast_bands=tuple(cast_bands)),
        grid=(R // TM_ROW,),
        in_specs=in_specs, out_specs=out_specs, out_shape=out_shape,
        compiler_params=_params(1),
        name="post_norm" if mm is not None else "pre_norm",
    )(*args)


def _mm_kernel(*refs, grid, relu2, w_rows_out, bands, zero_fill):
    it = iter(refs)
    a_ref, w_ref = next(it), next(it)
    cw_ref = next(it) if bands else None
    o_ref = next(it)
    if bands:
        _cast_band(cw_ref, next(it), grid, bands)
    if zero_fill:
        z_ref = next(it)
        z_ref[...] = jnp.zeros(z_ref.shape, z_ref.dtype)
    nk = grid[2]
    if w_rows_out:
        acc = _nt_dot(a_ref[...], w_ref[...])
    else:
        acc = jnp.dot(a_ref[...], w_ref[...], preferred_element_type=F32)
    if nk == 1:
        if relu2:
            u = jnp.maximum(acc, 0.0)
            acc = u * u
        o_ref[...] = acc.astype(o_ref.dtype)
    else:
        k = pl.program_id(2)

        @pl.when(k == 0)
        def _():
            o_ref[...] = acc

        @pl.when(k > 0)
        def _():
            o_ref[...] += acc


CAST_BAND_BYTES = 2 * 1024 * 1024


def _mm_tiles(m, n, kdim, cast):
    tm, tn, tk = 1024, 1024, 4096
    if cast is not None:
        steps = (m // tm) * (n // tn) * (kdim // tk)
        if cast[0].shape[1] * cast[0].shape[2] * 4 // steps > CAST_BAND_BYTES:
            tn //= 2
    return tm, tn, tk


def _mm_call(a, w, out_dtype, relu2=False, name="proj", cast=None, w_rows_out=False, zero_fill=None):
    m, kdim = a.shape
    n = w.shape[0] if w_rows_out else w.shape[1]
    tm, tn, tk = _mm_tiles(m, n, kdim, cast)
    nk = kdim // tk
    assert nk == 1 or (out_dtype == F32 and not relu2)
    grid = (m // tm, n // tn, nk)
    in_specs = [pl.BlockSpec((tm, tk), lambda i, j, k: (i, k)),
                pl.BlockSpec((tn, tk), lambda i, j, k: (j, k)) if w_rows_out else
                pl.BlockSpec((tk, tn), lambda i, j, k: (k, j))]
    args = [a, w]
    out_specs = [pl.BlockSpec((tm, tn), lambda i, j, k: (i, j))]
    out_shape = [_sds((m, n), out_dtype)]
    bands = None
    if cast is not None:
        c_in, c_out, c_shape, bands = _cast_plan(cast, grid)
        in_specs.append(c_in)
        args.append(cast[0])
        out_specs.append(c_out)
        out_shape.append(c_shape)
    if zero_fill is not None:
        (z_rows, z_cols), z_dtype = zero_fill
        n_steps = math.prod(grid)
        zb = min(d for d in range(16, z_rows + 1, 16) if z_rows % d == 0 and z_rows // d <= n_steps)
        nzb = z_rows // zb
        out_specs.append(pl.BlockSpec(
            (zb, z_cols), lambda i, j, k: (jnp.minimum((i * grid[1] + j) * grid[2] + k, nzb - 1), 0)))
        out_shape.append(_sds((z_rows, z_cols), z_dtype))
    return pl.pallas_call(
        functools.partial(_mm_kernel, grid=grid, relu2=relu2, w_rows_out=w_rows_out, bands=bands,
                          zero_fill=zero_fill is not None),
        grid=grid,
        in_specs=in_specs, out_specs=out_specs, out_shape=out_shape,
        compiler_params=_params(3),
        name=name,
    )(*args)


def _rope_tables(pair):
    t = np.arange(DEC_SEQ)
    row, col = t // GRID_W, t % GRID_W
    lane = np.arange(HD)
    da = 2 * pair
    inv_freq = 1.0 / (ROPE_BASE ** (np.arange(0, da, 2, dtype=np.float64) / da))
    pos = np.where((lane // da) % 2 == 0, row[:, None], col[:, None]).astype(np.float64)
    ang = pos * inv_freq[lane % pair][None, :]
    first = (lane % da) < pair
    cos = np.cos(ang)
    sin = np.sin(ang)
    sin_a = np.where(first[None, :], -sin, 0.0)
    sin_b = np.where(first[None, :], 0.0, sin)
    return tuple(jnp.asarray(a, F32) for a in (cos, sin_a, sin_b))


def _rope(x, cos, sin_a, sin_b, pair):
    return x * cos + pltpu.roll(x, HD - pair, 1) * sin_a + pltpu.roll(x, pair, 1) * sin_b


def _nt_dot(a, b):
    return lax.dot_general(a, b, (((1,), (1,)), ((), ())), preferred_element_type=F32)


def _exp2_rows(segs, sink_row=None):
    m = None
    for n, s in enumerate(segs):
        if n == 0 and sink_row is not None:
            mn = jnp.maximum(jnp.max(jnp.maximum(s[:, :HD], sink_row), axis=-1, keepdims=True),
                             jnp.max(s[:, HD:], axis=-1, keepdims=True))
        else:
            mn = jnp.max(s, axis=-1, keepdims=True)
        m = mn if m is None else jnp.maximum(m, mn)
    es = [jnp.exp2(s - m) for s in segs]
    den = sum(jnp.sum(e, axis=-1, keepdims=True) for e in es)
    if sink_row is not None:
        den = den + jnp.sum(jnp.exp2(sink_row - m), axis=-1, keepdims=True)
    return es, den


def _attn_kernel(*refs, kind, t_len, latent, n_kv, gq, qk_norm, use_sink, emit_kv, n_alias, lambda_init, stack):
    it = iter(refs)
    q_ref, k_ref, v_ref = next(it), next(it), next(it)
    if latent:
        ck_ref, cv_ref, cos_ref, sa_ref, sb_ref = next(it), next(it), next(it), next(it), next(it)
    if kind == "diff":
        lq1_ref, lk1_ref, lq2_ref, lk2_ref, gsub_ref = next(it), next(it), next(it), next(it), next(it)
    if qk_norm:
        gq_ref, gk_ref = next(it), next(it)
    if use_sink:
        sink_ref = next(it)
    for _ in range(n_alias):
        next(it)
    o_ref = next(it)
    if emit_kv:
        ko_ref, vo_ref = next(it), next(it)
    kb_scr, vb_scr = next(it), next(it)

    past = PAST if latent else 0
    pair = HD // 8 if kind == "diff" else HD // 4
    dh = HD // 2 if kind == "diff" else HD
    q_scale = dh ** -0.5 * LOG2E

    for kv in range(n_kv):
        kl = slice(kv * HD, (kv + 1) * HD)
        k = k_ref[:, kl]
        v = v_ref[:, kl]
        if qk_norm:
            k = _rms(k, gk_ref[...])
        if emit_kv:
            ko_ref[:, kl] = k
            vo_ref[:, kl] = v
        if latent:
            k = _rope(k, cos_ref[...], sa_ref[...], sb_ref[...], pair)
            kb_scr[kv, 0:past, :] = ck_ref[:, kl].astype(BF16)
            vb_scr[kv, 0:past, :] = cv_ref[:, kl].astype(BF16)
        kb_scr[kv, past:past + t_len, :] = k.astype(BF16)
        vb_scr[kv, past:past + t_len, :] = v.astype(BF16)

    if kind == "diff":
        lam = (jnp.exp(jnp.sum(lq1_ref[...] * lk1_ref[...], axis=-1, keepdims=True))
               - jnp.exp(jnp.sum(lq2_ref[...] * lk2_ref[...], axis=-1, keepdims=True)) + lambda_init)
        first_map = lax.broadcasted_iota(jnp.int32, (1, HD), 1) < HD // 2

    def scores(qh, qb, kv, sink):
        if kind == "win":
            w_len = 3 * QB
            start = jnp.clip((qb - 1) * QB, 0, t_len - w_len)
            loc = pl.ds(pl.multiple_of(past + start, QB), w_len)
            s_ctx = _nt_dot(qh, kb_scr[kv, 0:past, :])
            s_loc = _nt_dot(qh, kb_scr[kv, loc, :])
            qpos = qb * QB + lax.broadcasted_iota(jnp.int32, (qh.shape[0], 1), 0) % QB
            kpos = start + lax.broadcasted_iota(jnp.int32, (1, w_len), 1)
            s_loc = jnp.where(jnp.abs(qpos - kpos) <= WINDOW, s_loc, -1e30)
            es, den = _exp2_rows([s_ctx, s_loc], sink)
            return es, den, (slice(0, past), loc)
        es, den = _exp2_rows([_nt_dot(qh, kb_scr[kv])], sink)
        return es, den, (slice(None),)

    def values(es, segs, kv):
        return sum(jnp.dot(e.astype(BF16), vb_scr[kv, seg, :], preferred_element_type=F32)
                   for e, seg in zip(es, segs))

    def unit_scores(qb, kv, gs):
        rows = pl.ds(qb * QB, QB) if isinstance(qb, int) else pl.ds(pl.multiple_of(qb * QB, QB), QB)
        head_lanes = [slice((kv * gq + g) * HD, (kv * gq + g + 1) * HD) for g in gs]
        qs = []
        for lanes in head_lanes:
            q = q_ref[rows, lanes]
            if qk_norm:
                q = _rms(q, gq_ref[...])
            if latent:
                q = _rope(q, cos_ref[rows, :], sa_ref[rows, :], sb_ref[rows, :], pair)
            q = q * q_scale
            if kind == "diff":
                qs += [jnp.where(first_map, q, 0.0).astype(BF16), jnp.where(first_map, 0.0, q).astype(BF16)]
            else:
                qs.append(q.astype(BF16))
        sink = None
        if use_sink:
            sink = [jnp.broadcast_to(sink_ref[kv * gq + g] * LOG2E, (QB, HD)) for g in gs]
            sink = jnp.concatenate(sink, axis=0) if len(sink) > 1 else sink[0]
        if stack or len(qs) == 1:
            parts = [scores(jnp.concatenate(qs, axis=0) if len(qs) > 1 else qs[0], qb, kv, sink)]
        else:
            parts = [scores(q, qb, kv, sink) for q in qs]
        return rows, head_lanes, kv, parts

    def unit_values(state):
        rows, head_lanes, kv, parts = state
        outs = [(values(es, segs, kv), den) for es, den, segs in parts]
        if kind == "diff":
            if len(outs) == 1:
                (o, den), = outs
                o1, d1, o2, d2 = o[:QB], den[:QB], o[QB:], den[QB:]
            else:
                (o1, d1), (o2, d2) = outs
            o = o1 * (1.0 / d1) - o2 * (lam / d2)
            o_ref[rows, head_lanes[0]] = (_rms(o, gsub_ref[...]) * (1.0 - lambda_init)).astype(BF16)
            return
        (o, den), = outs
        o = o * (1.0 / den)
        for n, lanes in enumerate(head_lanes):
            o_ref[rows, lanes] = o[n * QB:(n + 1) * QB].astype(BF16)

    n_blk = t_len // QB
    head_groups = (tuple(range(gq)),) if stack else tuple((g,) for g in range(gq))
    per_iter = min(n_blk, 4 if stack and gq > 1 else 8)

    def blocks(qbs, kv, gs):
        if latent:
            for st in [unit_scores(qb, kv, gs) for qb in qbs]:
                unit_values(st)
        else:
            for qb in qbs:
                unit_values(unit_scores(qb, kv, gs))

    for kv in range(n_kv):
        for gs in head_groups:
            if n_blk == per_iter:
                blocks(range(n_blk), kv, gs)
            else:
                def some_blocks(i, carry, kv=kv, gs=gs):
                    blocks([per_iter * i + n for n in range(per_iter)], kv, gs)
                    return carry

                lax.fori_loop(0, n_blk // per_iter, some_blocks, 0)


def _attn_call(kind, proj, mix, layer, latent, cols, mix_col, n_kv, caches=None, tabs=None, lam_vecs=None,
               g_subln=None, qk_gains=None, sink=None, kv_out=None):
    t_len, nb, rb0 = (DEC_SEQ, DEC_BATCH, R_CTX // DEC_SEQ) if latent else (SEQ, BATCH, 0)
    gq = 1 if kind == "diff" else GQ
    kv_total = N_HEADS // gq
    c_q, c_k, c_v = cols
    qw, kw = n_kv * gq * HD, n_kv * HD
    grid = (nb, kv_total // n_kv)

    def tok(width, c0):
        return pl.BlockSpec((t_len, width), lambda b, h: (rb0 + b, c0 // width + h))

    in_specs = [tok(qw, c_q), tok(kw, c_k), tok(kw, c_v)]
    args = [proj, proj, proj]
    if latent:
        cache = pl.BlockSpec((None, None, PAST, kw), lambda b, h: (b, layer, 0, h))
        tab = pl.BlockSpec((t_len, HD), lambda b, h: (0, 0))
        in_specs += [cache, cache, tab, tab, tab]
        args += [c.reshape(DEC_BATCH, DEPTH, PAST, kv_total * HD) for c in caches] + list(tabs)
    if kind == "diff":
        vec = pl.BlockSpec((None, 1, HD // 2), lambda b, h: (layer, 0, 0))
        in_specs += [vec] * 4 + [pl.BlockSpec((None, 1, HD), lambda b, h: (layer, 0, 0))]
        args += [v.reshape(DEPTH, 1, HD // 2) for v in lam_vecs] + [g_subln.reshape(DEPTH, 1, HD)]
    if qk_gains is not None:
        gain = pl.BlockSpec((None, 1, HD), lambda b, h: (layer, 0, 0))
        in_specs += [gain, gain]
        args += [g.reshape(DEPTH, 1, HD) for g in qk_gains]
    if sink is not None:
        in_specs.append(pl.BlockSpec((None, n_kv * GQ, 1, HD), lambda b, h: (layer, h, 0, 0)))
        args.append(jnp.pad(sink.reshape(DEPTH, N_HEADS, 1, 1), ((0, 0), (0, 0), (0, 0), (0, HD - 1)),
                            constant_values=-1e30))
    aliases = {len(args): 0}
    in_specs.append(pl.BlockSpec(memory_space=pl.ANY))
    args.append(mix)
    out_specs = [tok(qw, mix_col)]
    out_shape = [_sds((R, D), BF16)]
    emit_kv = kv_out is not None
    if emit_kv:
        for n, prev in enumerate(kv_out):
            aliases[len(args)] = 1 + n
            in_specs.append(pl.BlockSpec(memory_space=pl.ANY))
            args.append(prev)
        kv_spec = pl.BlockSpec((None, None, SEQ, kw), lambda b, h: (b, layer, 0, h))
        out_specs += [kv_spec, kv_spec]
        out_shape += [_sds((BATCH, DEPTH, SEQ, kv_total * HD), F32)] * 2
    s_len = t_len + (PAST if latent else 0)
    return pl.pallas_call(
        functools.partial(_attn_kernel, kind=kind, t_len=t_len, latent=latent, n_kv=n_kv, gq=gq,
                          qk_norm=qk_gains is not None, use_sink=sink is not None, emit_kv=emit_kv,
                          n_alias=len(aliases), lambda_init=0.8 - 0.6 * math.exp(-0.3 * layer),
                          stack=not latent or kind == "win"),
        grid=grid,
        in_specs=in_specs, out_specs=out_specs, out_shape=out_shape,
        scratch_shapes=[pltpu.VMEM((n_kv, s_len, HD), BF16), pltpu.VMEM((n_kv, s_len, HD), BF16)],
        input_output_aliases=aliases,
        compiler_params=_params(2),
        name=kind + ("_lat" if latent else "_ctx"),
    )(*args)


def _conv_kernel(x_ref, w_ref, b_ref, o_ref, *, t_len):
    u = x_ref[...]
    w = w_ref[...]
    n_rows = u.shape[0]
    t = lax.broadcasted_iota(jnp.int32, (n_rows, 1), 0) % t_len
    half = D_CONV // 2
    acc = u * w[half:half + 1, :] + b_ref[...]
    for s in range(1, half + 1):
        before = jnp.where(t >= s, pltpu.roll(u, s, 0), 0.0)
        after = jnp.where(t < t_len - s, pltpu.roll(u, n_rows - s, 0), 0.0)
        acc = acc + before * w[half - s:half - s + 1, :] + after * w[half + s:half + s + 1, :]
    o_ref[...] = acc * _sigmoid(acc)


def _conv_call(proj, conv_w, conv_b, layer, latent):
    t_len, n_rows, row0 = (DEC_SEQ, R_LAT, R_CTX) if latent else (SEQ, R_CTX, 0)
    tc, tr = 512, 1024
    rb0 = row0 // tr
    return pl.pallas_call(
        functools.partial(_conv_kernel, t_len=t_len),
        grid=(n_rows // tr, CONV_DIM // tc),
        in_specs=[pl.BlockSpec((tr, tc), lambda b, j: (rb0 + b, C_XBC // tc + j)),
                  pl.BlockSpec((None, D_CONV, tc), lambda b, j: (layer, 0, j)),
                  pl.BlockSpec((None, 1, tc), lambda b, j: (layer, 0, j))],
        out_specs=pl.BlockSpec((tr, tc), lambda b, j: (b, j)),
        out_shape=_sds((n_rows, CONV_DIM), F32),
        compiler_params=_params(2),
        name="conv_lat" if latent else "conv_ctx",
    )(proj, conv_w, conv_b.reshape(DEPTH, 1, CONV_DIM))


def _split3(v):
    hi = v.astype(BF16).astype(F32)
    r1 = v - hi
    mid = r1.astype(BF16).astype(F32)
    lo = (r1 - mid).astype(BF16).astype(F32)
    return hi, mid, lo


def _exact_dot(tri, v, tri_first):
    parts = [p.astype(BF16) for p in _split3(v)]
    if tri_first:
        return sum(jnp.dot(tri, p, preferred_element_type=F32) for p in parts)
    return sum(jnp.dot(p, tri, preferred_element_type=F32) for p in parts)


def _lane_spread(v, width):
    hi, mid, lo = _split3(v)
    packed = (hi + pltpu.roll(mid, SSM_H, 1) + pltpu.roll(lo, 2 * SSM_H, 1)).astype(BF16)
    n = SSM_H * width
    src = lax.broadcasted_iota(jnp.int32, (HD, n), 0)
    dst = lax.broadcasted_iota(jnp.int32, (HD, n), 1)
    sel = jnp.where((src < 3 * SSM_H) & (src % SSM_H == dst // width), 1.0, 0.0).astype(BF16)
    return jnp.dot(packed, sel, preferred_element_type=F32)


def _softplus(x):
    return jnp.maximum(x, 0.0) + jnp.log1p(jnp.exp(-jnp.abs(x)))


def _scan_chunk(direction, xa_ref, dt_ref, bias_ref, alog_ref, ht_scr, y_ref):
    ii = lax.broadcasted_iota(jnp.int32, (CHUNK, CHUNK), 0)
    jj = lax.broadcasted_iota(jnp.int32, (CHUNK, CHUNK), 1)
    if direction == 0:
        keep, keep_t = jj <= ii, ii <= jj
        edge = CHUNK - 1
    else:
        keep, keep_t = jj >= ii, ii >= jj
        edge = 0
    tri_col = jnp.where(keep, 1.0, 0.0).astype(BF16)
    tri_row = jnp.where(keep_t, 1.0, 0.0).astype(BF16)

    raw = dt_ref[...]
    if direction == 1:
        raw = pltpu.roll(raw, HD - SSM_H, 1)
    head_lane = lax.broadcasted_iota(jnp.int32, (1, HD), 1) < SSM_H
    dt_c = jnp.where(head_lane, _softplus(raw + bias_ref[...]), 0.0)
    da_c = dt_c * (-jnp.exp(alog_ref[...]))
    a_col = _exact_dot(tri_col, da_c, True)
    a_row = _exact_dot(tri_row, da_c.T[0:SSM_H, :], False)

    a_sq = _lane_spread(a_col, CHUNK)
    a_hp = _lane_spread(a_col, SSM_P)
    dt_hp = _lane_spread(dt_c, SSM_P)

    xa = xa_ref[...]
    xdt = xa[:, :SSM_INNER] * dt_hp
    a_tot = a_hp[edge:edge + 1, :]
    grow = jnp.exp(a_hp)
    w_end = (xdt * jnp.exp(a_tot - a_hp)).astype(BF16)
    carry = jnp.exp(a_tot)
    lower_half = lax.broadcasted_iota(jnp.int32, (1, 2 * SSM_P), 1) < SSM_P

    hg = SSM_H // SSM_G
    gw = hg * SSM_P
    for g in range(SSM_G):
        bm = xa[:, SSM_INNER + g * SSM_N:SSM_INNER + (g + 1) * SSM_N].astype(BF16)
        cm = xa[:, SSM_INNER + (SSM_G + g) * SSM_N:SSM_INNER + (SSM_G + g + 1) * SSM_N].astype(BF16)
        cb = _nt_dot(cm, bm)
        gl = slice(g * gw, (g + 1) * gw)
        ht = ht_scr[:, gl]
        y_off = jnp.dot(cm, ht.astype(BF16), preferred_element_type=F32) * grow[:, gl]
        for pr in range(hg // 2):
            pl_ = slice(g * gw + pr * 2 * SSM_P, g * gw + (pr + 1) * 2 * SSM_P)
            x_pair = xdt[:, pl_]
            y_pair = y_off[:, pr * 2 * SSM_P:(pr + 1) * 2 * SSM_P]
            for half in range(2):
                h = g * hg + pr * 2 + half
                decay = jnp.exp(jnp.where(keep, a_sq[:, h * CHUNK:(h + 1) * CHUNK] - a_row[h:h + 1, :], -1e30))
                rhs = jnp.where(lower_half if half == 0 else ~lower_half, x_pair, 0.0).astype(BF16)
                y_pair = y_pair + jnp.dot((cb * decay).astype(BF16), rhs, preferred_element_type=F32)
            y_ref[:, pl_] = y_pair
        st = lax.dot_general(bm, w_end[:, gl], (((0,), (0,)), ((), ())), preferred_element_type=F32)
        ht_scr[:, gl] = ht * carry[:, gl] + st


def _scan_kernel(*refs, has_h0, emit_state, n_chunks):
    it = iter(refs)
    ins = [[next(it) for _ in range(4)] for _ in range(2)]
    if has_h0:
        h0_ref = next(it)
    if emit_state:
        next(it)
    y_refs = [next(it), next(it)]
    if emit_state:
        hend_ref = next(it)
    ht_scr = next(it)
    c = pl.program_id(1)

    @pl.when(c == 0)
    def _():
        for d in range(2):
            if has_h0:
                ht_scr[d] = h0_ref[d].reshape(SSM_INNER, SSM_N).T
            else:
                ht_scr[d] = jnp.zeros(ht_scr.shape[1:], F32)

    for d in range(2):
        _scan_chunk(d, *ins[d], ht_scr.at[d], y_refs[d])

    if emit_state:
        @pl.when(c == n_chunks - 1)
        def _():
            for d in range(2):
                hend_ref[d] = ht_scr[d].T.reshape(SSM_H, SSM_P, SSM_N)


def _pad_lanes(v):
    return jnp.pad(v, ((0, 0), (0, 0), (0, HD - SSM_H))).reshape(DEPTH, 2, 1, HD)


def _scan_call(xact, proj, dt_bias, a_log, state, layer, latent, state_out=None):
    t_len, nb, row0 = (DEC_SEQ, DEC_BATCH, R_CTX) if latent else (SEQ, BATCH, 0)
    nc = t_len // CHUNK
    rb0 = row0 // CHUNK
    chunk_of = (lambda c: c, lambda c: nc - 1 - c)
    in_specs, args = [], []
    for d in range(2):
        in_specs += [pl.BlockSpec((CHUNK, CONV_DIM), lambda b, c, d=d: (b * nc + chunk_of[d](c), 0)),
                     pl.BlockSpec((CHUNK, HD), lambda b, c, d=d: (rb0 + b * nc + chunk_of[d](c), C_DT // HD)),
                     pl.BlockSpec((None, None, 1, HD), lambda b, c, d=d: (layer, d, 0, 0)),
                     pl.BlockSpec((None, None, 1, HD), lambda b, c, d=d: (layer, d, 0, 0))]
        args += [xact, proj, _pad_lanes(dt_bias), _pad_lanes(a_log)]
    state_spec = pl.BlockSpec((None, None, 2, SSM_H, SSM_P, SSM_N), lambda b, c: (b, layer, 0, 0, 0, 0))
    if latent:
        in_specs.append(state_spec)
        args.append(state)
    out_specs = [pl.BlockSpec((CHUNK, SSM_INNER), lambda b, c, d=d: (b * nc + chunk_of[d](c), 0)) for d in range(2)]
    out_shape = [_sds((nb * t_len, SSM_INNER), F32)] * 2
    aliases = {}
    emit_state = state_out is not None
    if emit_state:
        aliases[len(args)] = 2
        in_specs.append(pl.BlockSpec(memory_space=pl.ANY))
        args.append(state_out)
        out_specs.append(state_spec)
        out_shape.append(_sds((nb, DEPTH, 2, SSM_H, SSM_P, SSM_N), F32))
    return pl.pallas_call(
        functools.partial(_scan_kernel, has_h0=latent, emit_state=emit_state, n_chunks=nc),
        grid=(nb, nc),
        in_specs=in_specs, out_specs=out_specs, out_shape=out_shape,
        scratch_shapes=[pltpu.VMEM((2, SSM_N, SSM_INNER), F32)],
        input_output_aliases=aliases,
        compiler_params=_params(2),
        name="ssd_lat" if latent else "ssd_ctx",
    )(*args)


def _ssd_out_kernel(yf_ref, yb_ref, xs_ref, z_ref, dskip_ref, g_ref, mix_ref, o_ref):
    del mix_ref
    dsk = dskip_ref[0:1, :] + dskip_ref[1:2, :]
    z = z_ref[...]
    y = (yf_ref[...] + yb_ref[...] + dsk * xs_ref[...]) * (z * _sigmoid(z))
    gw = SSM_INNER // SSM_G
    g = g_ref[...]
    for k in range(SSM_G):
        lanes = slice(k * gw, (k + 1) * gw)
        yk = y[:, lanes]
        yk = yk * lax.rsqrt(jnp.mean(yk * yk, axis=-1, keepdims=True) + EPS)
        o_ref[:, lanes] = (yk * g[:, lanes]).astype(BF16)


def _ssd_out_call(yf, yb, xact, proj, d_skip, g_norm, mix, layer, latent):
    tm = 256
    rb0 = (R_CTX if latent else 0) // tm
    n_rows = yf.shape[0]
    local = pl.BlockSpec((tm, SSM_INNER), lambda i: (i, 0))
    return pl.pallas_call(
        _ssd_out_kernel,
        grid=(n_rows // tm,),
        in_specs=[local, local, local,
                  pl.BlockSpec((tm, SSM_INNER), lambda i: (rb0 + i, C_Z // SSM_INNER)),
                  pl.BlockSpec((None, 2, SSM_INNER), lambda i: (layer, 0, 0)),
                  pl.BlockSpec((None, 1, SSM_INNER), lambda i: (layer, 0, 0)),
                  pl.BlockSpec(memory_space=pl.ANY)],
        out_specs=pl.BlockSpec((tm, SSM_INNER), lambda i: (rb0 + i, 3)),
        out_shape=_sds((R, D), BF16),
        input_output_aliases={6: 0},
        compiler_params=_params(1),
        name="ssd_out_lat" if latent else "ssd_out_ctx",
    )(yf, yb, xact, proj, jnp.repeat(d_skip, SSM_P, axis=-1), g_norm.reshape(DEPTH, 1, SSM_INNER), mix)


def kernel(x_prompt, x_sample, c, cache_a_k, cache_a_v, cache_b_k, cache_b_v, cache_c_k, cache_c_v, state_ssm, c_ctx, w_mod, b_mod, g_pre_mix, g_post_mix, g_pre_ffn, g_post_ffn, w_in, w_out, lam_q1, lam_k1, lam_q2, lam_k2, g_subln, g_qnorm, g_knorm, sink, conv_w, conv_b, dt_bias, a_log, d_skip, g_ssm_norm, w_up, w_down):
    x = (x_prompt.reshape(R_CTX, D), x_sample.reshape(R_LAT, D))
    cvec = jnp.concatenate([c_ctx[None, :], c, jnp.zeros((8 - 1 - DEC_BATCH, D), F32)], axis=0)
    w_in_t = jnp.swapaxes(w_in, 1, 2)
    mods, w_in_b = _mod_call(cvec, w_mod, b_mod, (w_in_t, 0, IN_WP))
    mods4 = mods.reshape(DEPTH, 8, 1, N_MOD * D)

    tabs_a = _rope_tables(HD // 8)
    tabs_bc = _rope_tables(HD // 4)
    lam_vecs = (lam_q1, lam_k1, lam_q2, lam_k2)

    h, w_out_b = _norm_call(x, mods4, nxt=(g_pre_mix, 0, 1, 0), casts=((w_out, 0, D),))
    kv_a = [jnp.zeros((BATCH, DEPTH, SEQ, N_HEADS * HD), F32) for _ in range(2)]
    kv_b = [jnp.zeros((BATCH, DEPTH, SEQ, N_KV * HD), F32) for _ in range(2)]
    kv_c = [jnp.zeros((BATCH, DEPTH, SEQ, N_KV * HD), F32) for _ in range(2)]
    ssm_new = jnp.zeros((BATCH, DEPTH, 2, SSM_H, SSM_P, SSM_N), F32)
    for layer in range(DEPTH):
        last = layer + 1 == DEPTH
        proj, w_up_b, mix = _mm_call(h, w_in_b, F32, name="in_proj", cast=(w_up, layer, D), w_rows_out=True,
                                     zero_fill=((R, D), BF16))

        mix, *kv_a = _attn_call("diff", proj, mix, layer, False, (C_QA, C_KA, C_VA), 0, N_HEADS,
                                lam_vecs=lam_vecs, g_subln=g_subln, kv_out=kv_a)
        mix, *kv_b = _attn_call("gqa", proj, mix, layer, False, (C_QB, C_KB, C_VB), 1024, N_KV,
                                qk_gains=(g_qnorm, g_knorm), kv_out=kv_b)
        mix, *kv_c = _attn_call("gqa", proj, mix, layer, False, (C_QC, C_KC, C_VC), 2048, 1,
                                sink=sink, kv_out=kv_c)
        mix, = _attn_call("diff", proj, mix, layer, True, (C_QA, C_KA, C_VA), 0, 1,
                          caches=(cache_a_k, cache_a_v), tabs=tabs_a, lam_vecs=lam_vecs, g_subln=g_subln)
        mix, = _attn_call("gqa", proj, mix, layer, True, (C_QB, C_KB, C_VB), 1024, 1,
                          caches=(cache_b_k, cache_b_v), tabs=tabs_bc, qk_gains=(g_qnorm, g_knorm))
        mix, = _attn_call("win", proj, mix, layer, True, (C_QC, C_KC, C_VC), 2048, 1,
                          caches=(cache_c_k, cache_c_v), tabs=tabs_bc, sink=sink)

        for latent in (False, True):
            xact = _conv_call(proj, conv_w, conv_b, layer, latent)
            y_f, y_b, *st = _scan_call(xact, proj, dt_bias, a_log, state_ssm, layer, latent,
                                       state_out=None if latent else ssm_new)
            if not latent:
                (ssm_new,) = st
            mix = _ssd_out_call(y_f, y_b, xact, proj, d_skip, g_ssm_norm, mix, layer, latent)

        if last:
            (mm,) = _mm_call(mix, w_out_b, F32, name="out_proj")
        else:
            mm, w_out_b = _mm_call(mix, w_out_b, F32, name="out_proj", cast=(w_out, layer + 1, D))
        x, h2 = _norm_call(x, mods4, mm=mm, post=(g_post_mix, layer, 2), nxt=(g_pre_ffn, layer, 4, 3))
        u, w_down_b = _mm_call(h2, w_up_b, BF16, relu2=True, name="ffn_up", cast=(w_down, layer, D_FF))
        if last:
            (mm,) = _mm_call(u, w_down_b, F32, name="ffn_down")
            x_ctx, x_lat = _norm_call(x, mods4, mm=mm, post=(g_post_ffn, layer, 5), split_out=True)
        else:
            mm, w_in_b = _mm_call(u, w_down_b, F32, name="ffn_down", cast=(w_in_t, layer + 1, IN_WP))
            x, h = _norm_call(x, mods4, mm=mm, post=(g_post_ffn, layer, 5), nxt=(g_pre_mix, layer + 1, 1, 0))

    new_kv = [a.reshape(BATCH, DEPTH, SEQ, -1, HD) for a in (*kv_a, *kv_b, *kv_c)]
    return (x_ctx.reshape(BATCH, SEQ, D), x_lat.reshape(DEC_BATCH, DEC_SEQ, D), *new_kv, ssm_new)
```

```python
import functools
import math

import numpy as np
import jax
import jax.numpy as jnp
from jax import lax
from jax.experimental import pallas as pl
from jax.experimental.pallas import tpu as pltpu

F32 = jnp.float32
BF16 = jnp.bfloat16

D = 4096
BATCH, SEQ = 16, 256
DEC_BATCH, DEC_SEQ = 4, 1024
PAST = 256
DEPTH = 2
GRID_W = 64
HD = 128
N_HEADS = 8
N_KV = 2
GQ = N_HEADS // N_KV
WINDOW = 128
QB = 128
SSM_H, SSM_P, SSM_N, SSM_G = 16, 64, 128, 2
SSM_INNER = SSM_H * SSM_P
CHUNK = 128
D_CONV = 5
CONV_DIM = SSM_INNER + 2 * SSM_G * SSM_N
D_FF = 4 * D
EPS = 1e-6
ROPE_BASE = 10000.0
N_MOD = 6
LOG2E = 1.4426950408889634

R_CTX = BATCH * SEQ
R_LAT = DEC_BATCH * DEC_SEQ
R = R_CTX + R_LAT

C_QA, C_KA, C_VA = 0, 1024, 2048
C_QB, C_KB, C_VB = 3072, 4096, 4352
C_QC, C_KC, C_VC = 4608, 5632, 5888
C_Z, C_XBC, C_DT = 6144, 7168, 8704
IN_W = C_DT + 2 * SSM_H
IN_WP = 9216

VMEM_LIMIT = 56 * 1024 * 1024


def _sds(shape, dtype):
    return jax.ShapeDtypeStruct(shape, dtype)


def _params(n_grid, vmem=VMEM_LIMIT):
    return pltpu.CompilerParams(dimension_semantics=("arbitrary",) * n_grid, vmem_limit_bytes=vmem)


def _rms(x, g):
    return x * lax.rsqrt(jnp.mean(x * x, axis=-1, keepdims=True) + EPS) * g


def _sigmoid(x):
    return 1.0 / (1.0 + jnp.exp(-x))


def _mod_row(i, tm):
    r = i * tm
    return jnp.where(r >= R_CTX, 1 + (r - R_CTX) // DEC_SEQ, 0)


def _linear_step(grid):
    step = pl.program_id(0)
    for axis in range(1, len(grid)):
        step = step * grid[axis] + pl.program_id(axis)
    return step


def _cast_plan(cast, grid):
    cw, layer, out_rows = cast
    rows, cols = cw.shape[1:]
    n_steps = math.prod(grid)
    common = math.gcd(rows, out_rows)
    rb = min(d for d in range(16, common + 1, 16) if common % d == 0 and out_rows // d <= n_steps)
    n_live, n_bands = rows // rb, out_rows // rb

    def step_of(*idx):
        step = idx[0]
        for axis in range(1, len(grid)):
            step = step * grid[axis] + idx[axis]
        return step

    return (pl.BlockSpec((None, rb, cols), lambda *idx: (layer, jnp.minimum(step_of(*idx), n_live - 1), 0)),
            pl.BlockSpec((rb, cols), lambda *idx: (jnp.minimum(step_of(*idx), n_bands - 1), 0)),
            _sds((out_rows, cols), BF16), (n_live, n_bands))


def _cast_band(cw_ref, cwo_ref, grid, bands):
    n_live, n_bands = bands
    if n_live == n_bands:
        cwo_ref[...] = cw_ref[...].astype(BF16)
        return
    step = _linear_step(grid)

    @pl.when(step < n_live)
    def _():
        cwo_ref[...] = cw_ref[...].astype(BF16)

    @pl.when(step >= n_live)
    def _():
        cwo_ref[...] = jnp.zeros(cwo_ref.shape, BF16)


def _mod_kernel(c_ref, w_ref, b_ref, cw_ref, o_ref, cwo_ref, *, grid, bands):
    c = c_ref[...]
    s = (c * _sigmoid(c)).astype(BF16)
    o_ref[...] = jnp.dot(s, w_ref[...].astype(BF16), preferred_element_type=F32) + b_ref[...]
    _cast_band(cw_ref, cwo_ref, grid, bands)


def _mod_call(cvec, w_mod, b_mod, cast):
    tn = 512
    grid = (DEPTH, N_MOD * D // tn)
    c_in, c_out, c_shape, bands = _cast_plan(cast, grid)
    return pl.pallas_call(
        functools.partial(_mod_kernel, grid=grid, bands=bands),
        grid=grid,
        in_specs=[pl.BlockSpec((8, D), lambda l, j: (0, 0)),
                  pl.BlockSpec((None, D, tn), lambda l, j: (l, 0, j)),
                  pl.BlockSpec((None, 1, tn), lambda l, j: (l, 0, j)), c_in],
        out_specs=[pl.BlockSpec((None, 8, tn), lambda l, j: (l, 0, j)), c_out],
        out_shape=[_sds((DEPTH, 8, N_MOD * D), F32), c_shape],
        compiler_params=_params(2),
        name="modulation",
    )(cvec, w_mod, b_mod.reshape(DEPTH, 1, N_MOD * D), cast[0])


TM_ROW = 256
N_CTX_TILES = R_CTX // TM_ROW


def _mod_spec(layer, chunk, tm):
    return pl.BlockSpec((None, None, 1, D), lambda i: (layer, _mod_row(i, tm), 0, chunk))


def _gain_spec(layer):
    return pl.BlockSpec((None, 1, D), lambda i: (layer, 0, 0))


_ROW = pl.BlockSpec((TM_ROW, D), lambda i: (i, 0))
_ROW_CTX = pl.BlockSpec((TM_ROW, D), lambda i: (jnp.minimum(i, N_CTX_TILES - 1), 0))
_ROW_LAT = pl.BlockSpec((TM_ROW, D), lambda i: (jnp.maximum(i - N_CTX_TILES, 0), 0))


def _norm_kernel(*refs, has_mm, split_in, with_next, split_out, cast_bands):
    it = iter(refs)
    if has_mm:
        mm_ref = next(it)
    x_refs = (next(it), next(it)) if split_in else (next(it),)
    if has_mm:
        gpost_ref, gt_ref = next(it), next(it)
    if with_next:
        gpre_ref, sc_ref, sh_ref = next(it), next(it), next(it)
    cw_refs = [next(it) for _ in cast_bands]
    if has_mm:
        xo_refs = (next(it), next(it)) if split_out else (next(it),)
    if with_next:
        h_ref = next(it)
    for cw_ref, bands in zip(cw_refs, cast_bands):
        _cast_band(cw_ref, next(it), (R // TM_ROW,), bands)

    def body(x_ref, xo_ref):
        x = x_ref[...]
        if has_mm:
            x = x + gt_ref[...] * _rms(mm_ref[...], gpost_ref[...])
            xo_ref[...] = x
        if with_next:
            h_ref[...] = (_rms(x, gpre_ref[...]) * (1.0 + sc_ref[...]) + sh_ref[...]).astype(BF16)

    if split_in or split_out:
        i = pl.program_id(0)

        @pl.when(i < N_CTX_TILES)
        def _():
            body(x_refs[0], xo_refs[0] if has_mm else None)

        @pl.when(i >= N_CTX_TILES)
        def _():
            body(x_refs[-1], xo_refs[-1] if has_mm else None)
    else:
        body(x_refs[0], xo_refs[0] if has_mm else None)


def _norm_call(x, mods4, mm=None, post=None, nxt=None, split_out=False, casts=()):
    split_in = isinstance(x, tuple)
    in_specs, args = [], []
    if mm is not None:
        in_specs.append(_ROW)
        args.append(mm)
    in_specs += [_ROW_CTX, _ROW_LAT] if split_in else [_ROW]
    args += list(x) if split_in else [x]
    out_specs, out_shape = [], []
    if mm is not None:
        g_post, pl_layer, gate_chunk = post
        in_specs += [_gain_spec(pl_layer), _mod_spec(pl_layer, gate_chunk, TM_ROW)]
        args += [g_post.reshape(DEPTH, 1, D), mods4]
        if split_out:
            out_specs += [_ROW_CTX, _ROW_LAT]
            out_shape += [_sds((R_CTX, D), F32), _sds((R_LAT, D), F32)]
        else:
            out_specs.append(_ROW)
            out_shape.append(_sds((R, D), F32))
    if nxt is not None:
        g_pre, nl, sc_chunk, sh_chunk = nxt
        in_specs += [_gain_spec(nl), _mod_spec(nl, sc_chunk, TM_ROW), _mod_spec(nl, sh_chunk, TM_ROW)]
        args += [g_pre.reshape(DEPTH, 1, D), mods4, mods4]
        out_specs.append(_ROW)
        out_shape.append(_sds((R, D), BF16))
    cast_bands = []
    for cast in casts:
        c_in, c_out, c_shape, bands = _cast_plan(cast, (R // TM_ROW,))
        in_specs.append(c_in)
        args.append(cast[0])
        out_specs.append(c_out)
        out_shape.append(c_shape)
        cast_bands.append(bands)
    return pl.pallas_call(
        functools.partial(_norm_kernel, has_mm=mm is not None, split_in=split_in, with_next=nxt is not None,
                          split_out=split_out, cast_bands=tuple(cast_bands)),
        grid=(R // TM_ROW,),
        in_specs=in_specs, out_specs=out_specs, out_shape=out_shape,
        compiler_params=_params(1),
        name="post_norm" if mm is not None else "pre_norm",
    )(*args)


def _mm_kernel(*refs, grid, relu2, w_rows_out, bands, zero_fill):
    it = iter(refs)
    a_ref, w_ref = next(it), next(it)
    cw_ref = next(it) if bands else None
    o_ref = next(it)
    if bands:
        _cast_band(cw_ref, next(it), grid, bands)
    for _ in range(zero_fill):
        z_ref = next(it)
        z_ref[...] = jnp.zeros(z_ref.shape, z_ref.dtype)
    nk = grid[2]
    if w_rows_out:
        acc = _nt_dot(a_ref[...], w_ref[...])
    else:
        acc = jnp.dot(a_ref[...], w_ref[...], preferred_element_type=F32)
    if nk == 1:
        if relu2:
            u = jnp.maximum(acc, 0.0)
            acc = u * u
        o_ref[...] = acc.astype(o_ref.dtype)
    else:
        k = pl.program_id(2)

        @pl.when(k == 0)
        def _():
            o_ref[...] = acc

        @pl.when(k > 0)
        def _():
            o_ref[...] += acc


CAST_BAND_BYTES = 2 * 1024 * 1024


def _mm_tiles(m, n, kdim, cast):
    tm, tn, tk = 1024, 1024, 4096
    if cast is not None:
        steps = (m // tm) * (n // tn) * (kdim // tk)
        if cast[0].shape[1] * cast[0].shape[2] * 4 // steps > CAST_BAND_BYTES:
            tn //= 2
    return tm, tn, tk


def _mm_call(a, w, out_dtype, relu2=False, name="proj", cast=None, w_rows_out=False, zero_fill=()):
    m, kdim = a.shape
    n = w.shape[0] if w_rows_out else w.shape[1]
    tm, tn, tk = _mm_tiles(m, n, kdim, cast)
    nk = kdim // tk
    assert nk == 1 or (out_dtype == F32 and not relu2)
    grid = (m // tm, n // tn, nk)
    in_specs = [pl.BlockSpec((tm, tk), lambda i, j, k: (i, k)),
                pl.BlockSpec((tn, tk), lambda i, j, k: (j, k)) if w_rows_out else
                pl.BlockSpec((tk, tn), lambda i, j, k: (k, j))]
    args = [a, w]
    out_specs = [pl.BlockSpec((tm, tn), lambda i, j, k: (i, j))]
    out_shape = [_sds((m, n), out_dtype)]
    bands = None
    if cast is not None:
        c_in, c_out, c_shape, bands = _cast_plan(cast, grid)
        in_specs.append(c_in)
        args.append(cast[0])
        out_specs.append(c_out)
        out_shape.append(c_shape)
    for (z_rows, z_cols), z_dtype in zero_fill:
        n_steps = math.prod(grid)
        zb = min(d for d in range(16, z_rows + 1, 16) if z_rows % d == 0 and z_rows // d <= n_steps)
        out_specs.append(pl.BlockSpec(
            (zb, z_cols),
            lambda i, j, k, nzb=z_rows // zb: (jnp.minimum((i * grid[1] + j) * grid[2] + k, nzb - 1), 0)))
        out_shape.append(_sds((z_rows, z_cols), z_dtype))
    return pl.pallas_call(
        functools.partial(_mm_kernel, grid=grid, relu2=relu2, w_rows_out=w_rows_out, bands=bands,
                          zero_fill=len(zero_fill)),
        grid=grid,
        in_specs=in_specs, out_specs=out_specs, out_shape=out_shape,
        compiler_params=_params(3),
        name=name,
    )(*args)


def _rope_tables(pair):
    t = np.arange(DEC_SEQ)
    row, col = t // GRID_W, t % GRID_W
    lane = np.arange(HD)
    da = 2 * pair
    inv_freq = 1.0 / (ROPE_BASE ** (np.arange(0, da, 2, dtype=np.float64) / da))
    pos = np.where((lane // da) % 2 == 0, row[:, None], col[:, None]).astype(np.float64)
    ang = pos * inv_freq[lane % pair][None, :]
    first = (lane % da) < pair
    cos = np.cos(ang)
    sin = np.sin(ang)
    sin_a = np.where(first[None, :], -sin, 0.0)
    sin_b = np.where(first[None, :], 0.0, sin)
    return tuple(jnp.asarray(a, F32) for a in (cos, sin_a, sin_b))


def _rope(x, cos, sin_a, sin_b, pair):
    return x * cos + pltpu.roll(x, HD - pair, 1) * sin_a + pltpu.roll(x, pair, 1) * sin_b


def _nt_dot(a, b):
    return lax.dot_general(a, b, (((1,), (1,)), ((), ())), preferred_element_type=F32)


def _exp2_rows(segs, sink_row=None):
    m = None
    for n, s in enumerate(segs):
        if n == 0 and sink_row is not None:
            mn = jnp.maximum(jnp.max(jnp.maximum(s[:, :HD], sink_row), axis=-1, keepdims=True),
                             jnp.max(s[:, HD:], axis=-1, keepdims=True))
        else:
            mn = jnp.max(s, axis=-1, keepdims=True)
        m = mn if m is None else jnp.maximum(m, mn)
    es = [jnp.exp2(s - m) for s in segs]
    den = sum(jnp.sum(e, axis=-1, keepdims=True) for e in es)
    if sink_row is not None:
        den = den + jnp.sum(jnp.exp2(sink_row - m), axis=-1, keepdims=True)
    return es, den


def _attn_kernel(*refs, kind, t_len, latent, n_kv, gq, qk_norm, use_sink, emit_kv, n_alias, lambda_init, stack):
    it = iter(refs)
    q_ref, k_ref, v_ref = next(it), next(it), next(it)
    if latent:
        ck_ref, cv_ref, cos_ref, sa_ref, sb_ref = next(it), next(it), next(it), next(it), next(it)
    if kind == "diff":
        lq1_ref, lk1_ref, lq2_ref, lk2_ref, gsub_ref = next(it), next(it), next(it), next(it), next(it)
    if qk_norm:
        gq_ref, gk_ref = next(it), next(it)
    if use_sink:
        sink_ref = next(it)
    for _ in range(n_alias):
        next(it)
    o_ref = next(it)
    if emit_kv:
        ko_ref, vo_ref = next(it), next(it)
    kb_scr, vb_scr = next(it), next(it)

    past = PAST if latent else 0
    pair = HD // 8 if kind == "diff" else HD // 4
    dh = HD // 2 if kind == "diff" else HD
    q_scale = dh ** -0.5 * LOG2E

    for kv in range(n_kv):
        kl = slice(kv * HD, (kv + 1) * HD)
        k = k_ref[:, kl]
        v = v_ref[:, kl]
        if qk_norm:
            k = _rms(k, gk_ref[...])
        if emit_kv:
            ko_ref[:, kl] = k
            vo_ref[:, kl] = v
        if latent:
            k = _rope(k, cos_ref[...], sa_ref[...], sb_ref[...], pair)
            kb_scr[kv, 0:past, :] = ck_ref[:, kl].astype(BF16)
            vb_scr[kv, 0:past, :] = cv_ref[:, kl].astype(BF16)
        kb_scr[kv, past:past + t_len, :] = k.astype(BF16)
        vb_scr[kv, past:past + t_len, :] = v.astype(BF16)

    if kind == "diff":
        lam = (jnp.exp(jnp.sum(lq1_ref[...] * lk1_ref[...], axis=-1, keepdims=True))
               - jnp.exp(jnp.sum(lq2_ref[...] * lk2_ref[...], axis=-1, keepdims=True)) + lambda_init)
        first_map = lax.broadcasted_iota(jnp.int32, (1, HD), 1) < HD // 2

    def scores(qh, qb, kv, sink):
        if kind == "win":
            w_len = 3 * QB
            start = jnp.clip((qb - 1) * QB, 0, t_len - w_len)
            loc = pl.ds(pl.multiple_of(past + start, QB), w_len)
            s_ctx = _nt_dot(qh, kb_scr[kv, 0:past, :])
            s_loc = _nt_dot(qh, kb_scr[kv, loc, :])
            qpos = qb * QB + lax.broadcasted_iota(jnp.int32, (qh.shape[0], 1), 0) % QB
            kpos = start + lax.broadcasted_iota(jnp.int32, (1, w_len), 1)
            s_loc = jnp.where(jnp.abs(qpos - kpos) <= WINDOW, s_loc, -1e30)
            es, den = _exp2_rows([s_ctx, s_loc], sink)
            return es, den, (slice(0, past), loc)
        es, den = _exp2_rows([_nt_dot(qh, kb_scr[kv])], sink)
        return es, den, (slice(None),)

    def values(es, segs, kv):
        return sum(jnp.dot(e.astype(BF16), vb_scr[kv, seg, :], preferred_element_type=F32)
                   for e, seg in zip(es, segs))

    def unit_scores(qb, kv, gs):
        rows = pl.ds(qb * QB, QB) if isinstance(qb, int) else pl.ds(pl.multiple_of(qb * QB, QB), QB)
        head_lanes = [slice((kv * gq + g) * HD, (kv * gq + g + 1) * HD) for g in gs]
        qs = []
        for lanes in head_lanes:
            q = q_ref[rows, lanes]
            if qk_norm:
                q = _rms(q, gq_ref[...])
            if latent:
                q = _rope(q, cos_ref[rows, :], sa_ref[rows, :], sb_ref[rows, :], pair)
            q = q * q_scale
            if kind == "diff":
                qs += [jnp.where(first_map, q, 0.0).astype(BF16), jnp.where(first_map, 0.0, q).astype(BF16)]
            else:
                qs.append(q.astype(BF16))
        sink = None
        if use_sink:
            sink = [jnp.broadcast_to(sink_ref[kv * gq + g] * LOG2E, (QB, HD)) for g in gs]
            sink = jnp.concatenate(sink, axis=0) if len(sink) > 1 else sink[0]
        if stack or len(qs) == 1:
            parts = [scores(jnp.concatenate(qs, axis=0) if len(qs) > 1 else qs[0], qb, kv, sink)]
        else:
            parts = [scores(q, qb, kv, sink) for q in qs]
        return rows, head_lanes, kv, parts

    def unit_values(state):
        rows, head_lanes, kv, parts = state
        outs = [(values(es, segs, kv), den) for es, den, segs in parts]
        if kind == "diff":
            if len(outs) == 1:
                (o, den), = outs
                o1, d1, o2, d2 = o[:QB], den[:QB], o[QB:], den[QB:]
            else:
                (o1, d1), (o2, d2) = outs
            o = o1 * (1.0 / d1) - o2 * (lam / d2)
            o_ref[rows, head_lanes[0]] = (_rms(o, gsub_ref[...]) * (1.0 - lambda_init)).astype(BF16)
            return
        (o, den), = outs
        o = o * (1.0 / den)
        for n, lanes in enumerate(head_lanes):
            o_ref[rows, lanes] = o[n * QB:(n + 1) * QB].astype(BF16)

    n_blk = t_len // QB
    head_groups = (tuple(range(gq)),) if stack else tuple((g,) for g in range(gq))
    per_iter = min(n_blk, 4 if stack and gq > 1 else 8)

    def blocks(qbs, kv, gs):
        if latent:
            for st in [unit_scores(qb, kv, gs) for qb in qbs]:
                unit_values(st)
        else:
            for qb in qbs:
                unit_values(unit_scores(qb, kv, gs))

    for kv in range(n_kv):
        for gs in head_groups:
            if n_blk == per_iter:
                blocks(range(n_blk), kv, gs)
            else:
                def some_blocks(i, carry, kv=kv, gs=gs):
                    blocks([per_iter * i + n for n in range(per_iter)], kv, gs)
                    return carry

                lax.fori_loop(0, n_blk // per_iter, some_blocks, 0)


def _attn_call(kind, proj, mix, layer, latent, cols, mix_col, n_kv, caches=None, tabs=None, lam_vecs=None,
               g_subln=None, qk_gains=None, sink=None, kv_out=None):
    t_len, nb, rb0 = (DEC_SEQ, DEC_BATCH, R_CTX // DEC_SEQ) if latent else (SEQ, BATCH, 0)
    gq = 1 if kind == "diff" else GQ
    kv_total = N_HEADS // gq
    c_q, c_k, c_v = cols
    qw, kw = n_kv * gq * HD, n_kv * HD
    grid = (nb, kv_total // n_kv)

    def tok(width, c0):
        return pl.BlockSpec((t_len, width), lambda b, h: (rb0 + b, c0 // width + h))

    in_specs = [tok(qw, c_q), tok(kw, c_k), tok(kw, c_v)]
    args = [proj, proj, proj]
    if latent:
        cache = pl.BlockSpec((None, None, PAST, kw), lambda b, h: (b, layer, 0, h))
        tab = pl.BlockSpec((t_len, HD), lambda b, h: (0, 0))
        in_specs += [cache, cache, tab, tab, tab]
        args += [c.reshape(DEC_BATCH, DEPTH, PAST, kv_total * HD) for c in caches] + list(tabs)
    if kind == "diff":
        vec = pl.BlockSpec((None, 1, HD // 2), lambda b, h: (layer, 0, 0))
        in_specs += [vec] * 4 + [pl.BlockSpec((None, 1, HD), lambda b, h: (layer, 0, 0))]
        args += [v.reshape(DEPTH, 1, HD // 2) for v in lam_vecs] + [g_subln.reshape(DEPTH, 1, HD)]
    if qk_gains is not None:
        gain = pl.BlockSpec((None, 1, HD), lambda b, h: (layer, 0, 0))
        in_specs += [gain, gain]
        args += [g.reshape(DEPTH, 1, HD) for g in qk_gains]
    if sink is not None:
        in_specs.append(pl.BlockSpec((None, n_kv * GQ, 1, HD), lambda b, h: (layer, h, 0, 0)))
        args.append(jnp.pad(sink.reshape(DEPTH, N_HEADS, 1, 1), ((0, 0), (0, 0), (0, 0), (0, HD - 1)),
                            constant_values=-1e30))
    aliases = {len(args): 0}
    in_specs.append(pl.BlockSpec(memory_space=pl.ANY))
    args.append(mix)
    out_specs = [tok(qw, mix_col)]
    out_shape = [_sds((R, D), BF16)]
    emit_kv = kv_out is not None
    if emit_kv:
        for n, prev in enumerate(kv_out):
            aliases[len(args)] = 1 + n
            in_specs.append(pl.BlockSpec(memory_space=pl.ANY))
            args.append(prev)
        kv_spec = pl.BlockSpec((None, None, SEQ, kw), lambda b, h: (b, layer, 0, h))
        out_specs += [kv_spec, kv_spec]
        out_shape += [_sds((BATCH, DEPTH, SEQ, kv_total * HD), F32)] * 2
    s_len = t_len + (PAST if latent else 0)
    return pl.pallas_call(
        functools.partial(_attn_kernel, kind=kind, t_len=t_len, latent=latent, n_kv=n_kv, gq=gq,
                          qk_norm=qk_gains is not None, use_sink=sink is not None, emit_kv=emit_kv,
                          n_alias=len(aliases), lambda_init=0.8 - 0.6 * math.exp(-0.3 * layer),
                          stack=not latent or kind == "win"),
        grid=grid,
        in_specs=in_specs, out_specs=out_specs, out_shape=out_shape,
        scratch_shapes=[pltpu.VMEM((n_kv, s_len, HD), BF16), pltpu.VMEM((n_kv, s_len, HD), BF16)],
        input_output_aliases=aliases,
        compiler_params=_params(2),
        name=kind + ("_lat" if latent else "_ctx"),
    )(*args)


def _conv_kernel(x_ref, w_ref, b_ref, o_ref, *, t_len):
    u = x_ref[...]
    w = w_ref[...]
    n_rows = u.shape[0]
    t = lax.broadcasted_iota(jnp.int32, (n_rows, 1), 0) % t_len
    half = D_CONV // 2
    acc = u * w[half:half + 1, :] + b_ref[...]
    for s in range(1, half + 1):
        before = jnp.where(t >= s, pltpu.roll(u, s, 0), 0.0)
        after = jnp.where(t < t_len - s, pltpu.roll(u, n_rows - s, 0), 0.0)
        acc = acc + before * w[half - s:half - s + 1, :] + after * w[half + s:half + s + 1, :]
    o_ref[...] = acc * _sigmoid(acc)


def _conv_call(proj, conv_w, conv_b, layer, latent):
    t_len, n_rows, row0 = (DEC_SEQ, R_LAT, R_CTX) if latent else (SEQ, R_CTX, 0)
    tc, tr = 512, 1024
    rb0 = row0 // tr
    return pl.pallas_call(
        functools.partial(_conv_kernel, t_len=t_len),
        grid=(n_rows // tr, CONV_DIM // tc),
        in_specs=[pl.BlockSpec((tr, tc), lambda b, j: (rb0 + b, C_XBC // tc + j)),
                  pl.BlockSpec((None, D_CONV, tc), lambda b, j: (layer, 0, j)),
                  pl.BlockSpec((None, 1, tc), lambda b, j: (layer, 0, j))],
        out_specs=pl.BlockSpec((tr, tc), lambda b, j: (b, j)),
        out_shape=_sds((n_rows, CONV_DIM), F32),
        compiler_params=_params(2),
        name="conv_lat" if latent else "conv_ctx",
    )(proj, conv_w, conv_b.reshape(DEPTH, 1, CONV_DIM))


def _split3(v):
    hi = v.astype(BF16).astype(F32)
    r1 = v - hi
    mid = r1.astype(BF16).astype(F32)
    lo = (r1 - mid).astype(BF16).astype(F32)
    return hi, mid, lo


def _exact_dot(tri, v, tri_first):
    parts = [p.astype(BF16) for p in _split3(v)]
    if tri_first:
        return sum(jnp.dot(tri, p, preferred_element_type=F32) for p in parts)
    return sum(jnp.dot(p, tri, preferred_element_type=F32) for p in parts)


def _lane_spread(v, width):
    hi, mid, lo = _split3(v)
    packed = (hi + pltpu.roll(mid, SSM_H, 1) + pltpu.roll(lo, 2 * SSM_H, 1)).astype(BF16)
    n = SSM_H * width
    src = lax.broadcasted_iota(jnp.int32, (HD, n), 0)
    dst = lax.broadcasted_iota(jnp.int32, (HD, n), 1)
    sel = jnp.where((src < 3 * SSM_H) & (src % SSM_H == dst // width), 1.0, 0.0).astype(BF16)
    return jnp.dot(packed, sel, preferred_element_type=F32)


def _softplus(x):
    return jnp.maximum(x, 0.0) + jnp.log1p(jnp.exp(-jnp.abs(x)))


def _scan_chunk(direction, xa_ref, dt_ref, bias_ref, alog_ref, ht_scr, y_ref):
    ii = lax.broadcasted_iota(jnp.int32, (CHUNK, CHUNK), 0)
    jj = lax.broadcasted_iota(jnp.int32, (CHUNK, CHUNK), 1)
    if direction == 0:
        keep, keep_t = jj <= ii, ii <= jj
        edge = CHUNK - 1
    else:
        keep, keep_t = jj >= ii, ii >= jj
        edge = 0
    tri_col = jnp.where(keep, 1.0, 0.0).astype(BF16)
    tri_row = jnp.where(keep_t, 1.0, 0.0).astype(BF16)

    raw = dt_ref[...]
    if direction == 1:
        raw = pltpu.roll(raw, HD - SSM_H, 1)
    head_lane = lax.broadcasted_iota(jnp.int32, (1, HD), 1) < SSM_H
    dt_c = jnp.where(head_lane, _softplus(raw + bias_ref[...]), 0.0)
    da_c = dt_c * (-jnp.exp(alog_ref[...]))
    a_col = _exact_dot(tri_col, da_c, True)
    a_row = _exact_dot(tri_row, da_c.T[0:SSM_H, :], False)

    a_sq = _lane_spread(a_col, CHUNK)
    a_hp = _lane_spread(a_col, SSM_P)
    dt_hp = _lane_spread(dt_c, SSM_P)

    xa = xa_ref[...]
    xdt = xa[:, :SSM_INNER] * dt_hp
    a_tot = a_hp[edge:edge + 1, :]
    grow = jnp.exp(a_hp)
    w_end = (xdt * jnp.exp(a_tot - a_hp)).astype(BF16)
    carry = jnp.exp(a_tot)
    lower_half = lax.broadcasted_iota(jnp.int32, (1, 2 * SSM_P), 1) < SSM_P

    hg = SSM_H // SSM_G
    gw = hg * SSM_P
    for g in range(SSM_G):
        bm = xa[:, SSM_INNER + g * SSM_N:SSM_INNER + (g + 1) * SSM_N].astype(BF16)
        cm = xa[:, SSM_INNER + (SSM_G + g) * SSM_N:SSM_INNER + (SSM_G + g + 1) * SSM_N].astype(BF16)
        cb = _nt_dot(cm, bm)
        gl = slice(g * gw, (g + 1) * gw)
        ht = ht_scr[:, gl]
        y_off = jnp.dot(cm, ht.astype(BF16), preferred_element_type=F32) * grow[:, gl]
        for pr in range(hg // 2):
            pl_ = slice(g * gw + pr * 2 * SSM_P, g * gw + (pr + 1) * 2 * SSM_P)
            x_pair = xdt[:, pl_]
            y_pair = y_off[:, pr * 2 * SSM_P:(pr + 1) * 2 * SSM_P]
            for half in range(2):
                h = g * hg + pr * 2 + half
                decay = jnp.exp(jnp.where(keep, a_sq[:, h * CHUNK:(h + 1) * CHUNK] - a_row[h:h + 1, :], -1e30))
                rhs = jnp.where(lower_half if half == 0 else ~lower_half, x_pair, 0.0).astype(BF16)
                y_pair = y_pair + jnp.dot((cb * decay).astype(BF16), rhs, preferred_element_type=F32)
            y_ref[:, pl_] = y_pair
        st = lax.dot_general(bm, w_end[:, gl], (((0,), (0,)), ((), ())), preferred_element_type=F32)
        ht_scr[:, gl] = ht * carry[:, gl] + st


def _scan_kernel(*refs, has_h0, emit_state, n_chunks):
    it = iter(refs)
    ins = [[next(it) for _ in range(4)] for _ in range(2)]
    if has_h0:
        h0_ref = next(it)
    if emit_state:
        next(it)
    y_refs = [next(it), next(it)]
    if emit_state:
        hend_ref = next(it)
    ht_scr = next(it)
    c = pl.program_id(1)

    @pl.when(c == 0)
    def _():
        for d in range(2):
            if has_h0:
                ht_scr[d] = h0_ref[d].reshape(SSM_INNER, SSM_N).T
            else:
                ht_scr[d] = jnp.zeros(ht_scr.shape[1:], F32)

    for d in range(2):
        _scan_chunk(d, *ins[d], ht_scr.at[d], y_refs[d])

    if emit_state:
        @pl.when(c == n_chunks - 1)
        def _():
            for d in range(2):
                hend_ref[d] = ht_scr[d].T.reshape(SSM_H, SSM_P, SSM_N)


def _pad_lanes(v):
    return jnp.pad(v, ((0, 0), (0, 0), (0, HD - SSM_H))).reshape(DEPTH, 2, 1, HD)


def _scan_call(xact, proj, dt_bias, a_log, state, layer, latent, state_out=None):
    t_len, nb, row0 = (DEC_SEQ, DEC_BATCH, R_CTX) if latent else (SEQ, BATCH, 0)
    nc = t_len // CHUNK
    rb0 = row0 // CHUNK
    chunk_of = (lambda c: c, lambda c: nc - 1 - c)
    in_specs, args = [], []
    for d in range(2):
        in_specs += [pl.BlockSpec((CHUNK, CONV_DIM), lambda b, c, d=d: (b * nc + chunk_of[d](c), 0)),
                     pl.BlockSpec((CHUNK, HD), lambda b, c, d=d: (rb0 + b * nc + chunk_of[d](c), C_DT // HD)),
                     pl.BlockSpec((None, None, 1, HD), lambda b, c, d=d: (layer, d, 0, 0)),
                     pl.BlockSpec((None, None, 1, HD), lambda b, c, d=d: (layer, d, 0, 0))]
        args += [xact, proj, _pad_lanes(dt_bias), _pad_lanes(a_log)]
    state_spec = pl.BlockSpec((None, None, 2, SSM_H, SSM_P, SSM_N), lambda b, c: (b, layer, 0, 0, 0, 0))
    if latent:
        in_specs.append(state_spec)
        args.append(state)
    out_specs = [pl.BlockSpec((CHUNK, SSM_INNER), lambda b, c, d=d: (b * nc + chunk_of[d](c), 0)) for d in range(2)]
    out_shape = [_sds((nb * t_len, SSM_INNER), F32)] * 2
    aliases = {}
    emit_state = state_out is not None
    if emit_state:
        aliases[len(args)] = 2
        in_specs.append(pl.BlockSpec(memory_space=pl.ANY))
        args.append(state_out)
        out_specs.append(state_spec)
        out_shape.append(_sds((nb, DEPTH, 2, SSM_H, SSM_P, SSM_N), F32))
    return pl.pallas_call(
        functools.partial(_scan_kernel, has_h0=latent, emit_state=emit_state, n_chunks=nc),
        grid=(nb, nc),
        in_specs=in_specs, out_specs=out_specs, out_shape=out_shape,
        scratch_shapes=[pltpu.VMEM((2, SSM_N, SSM_INNER), F32)],
        input_output_aliases=aliases,
        compiler_params=_params(2),
        name="ssd_lat" if latent else "ssd_ctx",
    )(*args)


def _ssd_out_kernel(yf_ref, yb_ref, xs_ref, z_ref, dskip_ref, g_ref, mix_ref, o_ref):
    del mix_ref
    dsk = dskip_ref[0:1, :] + dskip_ref[1:2, :]
    z = z_ref[...]
    y = (yf_ref[...] + yb_ref[...] + dsk * xs_ref[...]) * (z * _sigmoid(z))
    gw = SSM_INNER // SSM_G
    g = g_ref[...]
    for k in range(SSM_G):
        lanes = slice(k * gw, (k + 1) * gw)
        yk = y[:, lanes]
        yk = yk * lax.rsqrt(jnp.mean(yk * yk, axis=-1, keepdims=True) + EPS)
        o_ref[:, lanes] = (yk * g[:, lanes]).astype(BF16)


def _ssd_out_call(yf, yb, xact, proj, d_skip, g_norm, mix, layer, latent):
    tm = 256
    rb0 = (R_CTX if latent else 0) // tm
    n_rows = yf.shape[0]
    local = pl.BlockSpec((tm, SSM_INNER), lambda i: (i, 0))
    return pl.pallas_call(
        _ssd_out_kernel,
        grid=(n_rows // tm,),
        in_specs=[local, local, local,
                  pl.BlockSpec((tm, SSM_INNER), lambda i: (rb0 + i, C_Z // SSM_INNER)),
                  pl.BlockSpec((None, 2, SSM_INNER), lambda i: (layer, 0, 0)),
                  pl.BlockSpec((None, 1, SSM_INNER), lambda i: (layer, 0, 0)),
                  pl.BlockSpec(memory_space=pl.ANY)],
        out_specs=pl.BlockSpec((tm, SSM_INNER), lambda i: (rb0 + i, 3)),
        out_shape=_sds((R, D), BF16),
        input_output_aliases={6: 0},
        compiler_params=_params(1),
        name="ssd_out_lat" if latent else "ssd_out_ctx",
    )(yf, yb, xact, proj, jnp.repeat(d_skip, SSM_P, axis=-1), g_norm.reshape(DEPTH, 1, SSM_INNER), mix)


def kernel(x_prompt, x_sample, c, cache_a_k, cache_a_v, cache_b_k, cache_b_v, cache_c_k, cache_c_v, state_ssm, c_ctx, w_mod, b_mod, g_pre_mix, g_post_mix, g_pre_ffn, g_post_ffn, w_in, w_out, lam_q1, lam_k1, lam_q2, lam_k2, g_subln, g_qnorm, g_knorm, sink, conv_w, conv_b, dt_bias, a_log, d_skip, g_ssm_norm, w_up, w_down):
    x = (x_prompt.reshape(R_CTX, D), x_sample.reshape(R_LAT, D))
    cvec = jnp.concatenate([c_ctx[None, :], c, jnp.zeros((8 - 1 - DEC_BATCH, D), F32)], axis=0)
    w_in_t = jnp.swapaxes(w_in, 1, 2)
    mods, w_in_b = _mod_call(cvec, w_mod, b_mod, (w_in_t, 0, IN_WP))
    mods4 = mods.reshape(DEPTH, 8, 1, N_MOD * D)

    tabs_a = _rope_tables(HD // 8)
    tabs_bc = _rope_tables(HD // 4)
    lam_vecs = (lam_q1, lam_k1, lam_q2, lam_k2)

    h, w_out_b = _norm_call(x, mods4, nxt=(g_pre_mix, 0, 1, 0), casts=((w_out, 0, D),))
    kv_rows = BATCH * DEPTH * SEQ
    zero_bufs = [((R, D), BF16)]
    zero_bufs += [((kv_rows, N_HEADS * HD), F32)] * 2 + [((kv_rows, N_KV * HD), F32)] * 4
    zero_bufs += [((BATCH * DEPTH * 2 * SSM_H * SSM_P, SSM_N), F32)]
    for layer in range(DEPTH):
        last = layer + 1 == DEPTH
        proj, w_up_b, mix, *zeros = _mm_call(h, w_in_b, F32, name="in_proj", cast=(w_up, layer, D),
                                             w_rows_out=True, zero_fill=zero_bufs if layer == 0 else zero_bufs[:1])
        if layer == 0:
            kv_a = [z.reshape(BATCH, DEPTH, SEQ, -1) for z in zeros[0:2]]
            kv_b = [z.reshape(BATCH, DEPTH, SEQ, -1) for z in zeros[2:4]]
            kv_c = [z.reshape(BATCH, DEPTH, SEQ, -1) for z in zeros[4:6]]
            ssm_new = zeros[6].reshape(BATCH, DEPTH, 2, SSM_H, SSM_P, SSM_N)

        mix, *kv_a = _attn_call("diff", proj, mix, layer, False, (C_QA, C_KA, C_VA), 0, N_HEADS,
                                lam_vecs=lam_vecs, g_subln=g_subln, kv_out=kv_a)
        mix, *kv_b = _attn_call("gqa", proj, mix, layer, False, (C_QB, C_KB, C_VB), 1024, N_KV,
                                qk_gains=(g_qnorm, g_knorm), kv_out=kv_b)
        mix, *kv_c = _attn_call("gqa", proj, mix, layer, False, (C_QC, C_KC, C_VC), 2048, 1,
                                sink=sink, kv_out=kv_c)
        mix, = _attn_call("diff", proj, mix, layer, True, (C_QA, C_KA, C_VA), 0, 1,
                          caches=(cache_a_k, cache_a_v), tabs=tabs_a, lam_vecs=lam_vecs, g_subln=g_subln)
        mix, = _attn_call("gqa", proj, mix, layer, True, (C_QB, C_KB, C_VB), 1024, 1,
                          caches=(cache_b_k, cache_b_v), tabs=tabs_bc, qk_gains=(g_qnorm, g_knorm))
        mix, = _attn_call("win", proj, mix, layer, True, (C_QC, C_KC, C_VC), 2048, 1,
                          caches=(cache_c_k, cache_c_v), tabs=tabs_bc, sink=sink)

        for latent in (False, True):
            xact = _conv_call(proj, conv_w, conv_b, layer, latent)
            y_f, y_b, *st = _scan_call(xact, proj, dt_bias, a_log, state_ssm, layer, latent,
                                       state_out=None if latent else ssm_new)
            if not latent:
                (ssm_new,) = st
            mix = _ssd_out_call(y_f, y_b, xact, proj, d_skip, g_ssm_norm, mix, layer, latent)

        if last:
            (mm,) = _mm_call(mix, w_out_b, F32, name="out_proj")
        else:
            mm, w_out_b = _mm_call(mix, w_out_b, F32, name="out_proj", cast=(w_out, layer + 1, D))
        x, h2 = _norm_call(x, mods4, mm=mm, post=(g_post_mix, layer, 2), nxt=(g_pre_ffn, layer, 4, 3))
        u, w_down_b = _mm_call(h2, w_up_b, BF16, relu2=True, name="ffn_up", cast=(w_down, layer, D_FF))
        if last:
            (mm,) = _mm_call(u, w_down_b, F32, name="ffn_down")
            x_ctx, x_lat = _norm_call(x, mods4, mm=mm, post=(g_post_ffn, layer, 5), split_out=True)
        else:
            mm, w_in_b = _mm_call(u, w_down_b, F32, name="ffn_down", cast=(w_in_t, layer + 1, IN_WP))
            x, h = _norm_call(x, mods4, mm=mm, post=(g_post_ffn, layer, 5), nxt=(g_pre_mix, layer + 1, 1, 0))

    new_kv = [a.reshape(BATCH, DEPTH, SEQ, -1, HD) for a in (*kv_a, *kv_b, *kv_c)]
    return (x_ctx.reshape(BATCH, SEQ, D), x_lat.reshape(DEC_BATCH, DEC_SEQ, D), *new_kv, ssm_new)
```

```python
import functools
import math

import numpy as np
import jax
import jax.numpy as jnp
from jax import lax
from jax.experimental import pallas as pl
from jax.experimental.pallas import tpu as pltpu

F32 = jnp.float32
BF16 = jnp.bfloat16

D = 4096
BATCH, SEQ = 16, 256
DEC_BATCH, DEC_SEQ = 4, 1024
PAST = 256
DEPTH = 2
GRID_W = 64
HD = 128
N_HEADS = 8
N_KV = 2
GQ = N_HEADS // N_KV
WINDOW = 128
QB = 128
SSM_H, SSM_P, SSM_N, SSM_G = 16, 64, 128, 2
SSM_INNER = SSM_H * SSM_P
CHUNK = 128
D_CONV = 5
CONV_DIM = SSM_INNER + 2 * SSM_G * SSM_N
D_FF = 4 * D
EPS = 1e-6
ROPE_BASE = 10000.0
N_MOD = 6
LOG2E = 1.4426950408889634

R_CTX = BATCH * SEQ
R_LAT = DEC_BATCH * DEC_SEQ
R = R_CTX + R_LAT

C_QA, C_KA, C_VA = 0, 1024, 2048
C_QB, C_KB, C_VB = 3072, 4096, 4352
C_QC, C_KC, C_VC = 4608, 5632, 5888
C_Z, C_XBC, C_DT = 6144, 7168, 8704
IN_W = C_DT + 2 * SSM_H
IN_WP = 9216

VMEM_LIMIT = 56 * 1024 * 1024


def _sds(shape, dtype):
    return jax.ShapeDtypeStruct(shape, dtype)


def _params(n_grid, vmem=VMEM_LIMIT):
    return pltpu.CompilerParams(dimension_semantics=("arbitrary",) * n_grid, vmem_limit_bytes=vmem)


def _rms(x, g):
    return x * lax.rsqrt(jnp.mean(x * x, axis=-1, keepdims=True) + EPS) * g


def _sigmoid(x):
    return 1.0 / (1.0 + jnp.exp(-x))


def _mod_row(i, tm):
    r = i * tm
    return jnp.where(r >= R_CTX, 1 + (r - R_CTX) // DEC_SEQ, 0)


def _linear_step(grid):
    step = pl.program_id(0)
    for axis in range(1, len(grid)):
        step = step * grid[axis] + pl.program_id(axis)
    return step


def _cast_plan(cast, grid):
    cw, layer, out_rows = cast
    rows, cols = cw.shape[1:]
    n_steps = math.prod(grid)
    common = math.gcd(rows, out_rows)
    rb = min(d for d in range(16, common + 1, 16) if common % d == 0 and out_rows // d <= n_steps)
    n_live, n_bands = rows // rb, out_rows // rb

    def step_of(*idx):
        step = idx[0]
        for axis in range(1, len(grid)):
            step = step * grid[axis] + idx[axis]
        return step

    return (pl.BlockSpec((None, rb, cols), lambda *idx: (layer, jnp.minimum(step_of(*idx), n_live - 1), 0)),
            pl.BlockSpec((rb, cols), lambda *idx: (jnp.minimum(step_of(*idx), n_bands - 1), 0)),
            _sds((out_rows, cols), BF16), (n_live, n_bands))


def _cast_band(cw_ref, cwo_ref, grid, bands):
    n_live, n_bands = bands
    if n_live == n_bands:
        cwo_ref[...] = cw_ref[...].astype(BF16)
        return
    step = _linear_step(grid)

    @pl.when(step < n_live)
    def _():
        cwo_ref[...] = cw_ref[...].astype(BF16)

    @pl.when(step >= n_live)
    def _():
        cwo_ref[...] = jnp.zeros(cwo_ref.shape, BF16)


def _mod_kernel(c_ref, w_ref, b_ref, cw_ref, o_ref, cwo_ref, *, grid, bands):
    c = c_ref[...]
    s = (c * _sigmoid(c)).astype(BF16)
    o_ref[...] = jnp.dot(s, w_ref[...].astype(BF16), preferred_element_type=F32) + b_ref[...]
    _cast_band(cw_ref, cwo_ref, grid, bands)


def _mod_call(cvec, w_mod, b_mod, cast):
    tn = 512
    grid = (DEPTH, N_MOD * D // tn)
    c_in, c_out, c_shape, bands = _cast_plan(cast, grid)
    return pl.pallas_call(
        functools.partial(_mod_kernel, grid=grid, bands=bands),
        grid=grid,
        in_specs=[pl.BlockSpec((8, D), lambda l, j: (0, 0)),
                  pl.BlockSpec((None, D, tn), lambda l, j: (l, 0, j)),
                  pl.BlockSpec((None, 1, tn), lambda l, j: (l, 0, j)), c_in],
        out_specs=[pl.BlockSpec((None, 8, tn), lambda l, j: (l, 0, j)), c_out],
        out_shape=[_sds((DEPTH, 8, N_MOD * D), F32), c_shape],
        compiler_params=_params(2),
        name="modulation",
    )(cvec, w_mod, b_mod.reshape(DEPTH, 1, N_MOD * D), cast[0])


TM_ROW = 256
N_CTX_TILES = R_CTX // TM_ROW
ROW_CHUNK = 64


def _mod_spec(layer, chunk, tm):
    return pl.BlockSpec((None, None, 1, D), lambda i: (layer, _mod_row(i, tm), 0, chunk))


def _gain_spec(layer):
    return pl.BlockSpec((None, 1, D), lambda i: (layer, 0, 0))


_ROW = pl.BlockSpec((TM_ROW, D), lambda i: (i, 0))
_ROW_CTX = pl.BlockSpec((TM_ROW, D), lambda i: (jnp.minimum(i, N_CTX_TILES - 1), 0))
_ROW_LAT = pl.BlockSpec((TM_ROW, D), lambda i: (jnp.maximum(i - N_CTX_TILES, 0), 0))


def _norm_kernel(*refs, has_mm, split_in, with_next, split_out, cast_bands):
    it = iter(refs)
    if has_mm:
        mm_ref = next(it)
    x_refs = (next(it), next(it)) if split_in else (next(it),)
    if has_mm:
        gpost_ref, gt_ref = next(it), next(it)
    if with_next:
        gpre_ref, sc_ref, sh_ref = next(it), next(it), next(it)
    cw_refs = [next(it) for _ in cast_bands]
    if has_mm:
        xo_refs = (next(it), next(it)) if split_out else (next(it),)
    if with_next:
        h_ref = next(it)
    for cw_ref, bands in zip(cw_refs, cast_bands):
        _cast_band(cw_ref, next(it), (R // TM_ROW,), bands)

    def body(x_ref, xo_ref):
        for r in range(0, TM_ROW, ROW_CHUNK):
            rows = slice(r, r + ROW_CHUNK)
            x = x_ref[rows, :]
            if has_mm:
                x = x + gt_ref[...] * _rms(mm_ref[rows, :], gpost_ref[...])
                xo_ref[rows, :] = x
            if with_next:
                h_ref[rows, :] = (_rms(x, gpre_ref[...]) * (1.0 + sc_ref[...]) + sh_ref[...]).astype(BF16)

    if split_in or split_out:
        i = pl.program_id(0)

        @pl.when(i < N_CTX_TILES)
        def _():
            body(x_refs[0], xo_refs[0] if has_mm else None)

        @pl.when(i >= N_CTX_TILES)
        def _():
            body(x_refs[-1], xo_refs[-1] if has_mm else None)
    else:
        body(x_refs[0], xo_refs[0] if has_mm else None)


def _norm_call(x, mods4, mm=None, post=None, nxt=None, split_out=False, casts=()):
    split_in = isinstance(x, tuple)
    in_specs, args = [], []
    if mm is not None:
        in_specs.append(_ROW)
        args.append(mm)
    in_specs += [_ROW_CTX, _ROW_LAT] if split_in else [_ROW]
    args += list(x) if split_in else [x]
    out_specs, out_shape = [], []
    if mm is not None:
        g_post, pl_layer, gate_chunk = post
        in_specs += [_gain_spec(pl_layer), _mod_spec(pl_layer, gate_chunk, TM_ROW)]
        args += [g_post.reshape(DEPTH, 1, D), mods4]
        if split_out:
            out_specs += [_ROW_CTX, _ROW_LAT]
            out_shape += [_sds((R_CTX, D), F32), _sds((R_LAT, D), F32)]
        else:
            out_specs.append(_ROW)
            out_shape.append(_sds((R, D), F32))
    if nxt is not None:
        g_pre, nl, sc_chunk, sh_chunk = nxt
        in_specs += [_gain_spec(nl), _mod_spec(nl, sc_chunk, TM_ROW), _mod_spec(nl, sh_chunk, TM_ROW)]
        args += [g_pre.reshape(DEPTH, 1, D), mods4, mods4]
        out_specs.append(_ROW)
        out_shape.append(_sds((R, D), BF16))
    cast_bands = []
    for cast in casts:
        c_in, c_out, c_shape, bands = _cast_plan(cast, (R // TM_ROW,))
        in_specs.append(c_in)
        args.append(cast[0])
        out_specs.append(c_out)
        out_shape.append(c_shape)
        cast_bands.append(bands)
    return pl.pallas_call(
        functools.partial(_norm_kernel, has_mm=mm is not None, split_in=split_in, with_next=nxt is not None,
                          split_out=split_out, cast_bands=tuple(cast_bands)),
        grid=(R // TM_ROW,),
        in_specs=in_specs, out_specs=out_specs, out_shape=out_shape,
        compiler_params=_params(1),
        name="post_norm" if mm is not None else "pre_norm",
    )(*args)


def _mm_kernel(*refs, grid, relu2, w_rows_out, bands, zero_fill):
    it = iter(refs)
    a_ref, w_ref = next(it), next(it)
    cw_ref = next(it) if bands else None
    o_ref = next(it)
    if bands:
        _cast_band(cw_ref, next(it), grid, bands)
    for _ in range(zero_fill):
        z_ref = next(it)
        z_ref[...] = jnp.zeros(z_ref.shape, z_ref.dtype)
    nk = grid[2]
    if w_rows_out:
        acc = _nt_dot(a_ref[...], w_ref[...])
    else:
        acc = jnp.dot(a_ref[...], w_ref[...], preferred_element_type=F32)
    if nk == 1:
        if relu2:
            u = jnp.maximum(acc, 0.0)
            acc = u * u
        o_ref[...] = acc.astype(o_ref.dtype)
    else:
        k = pl.program_id(2)

        @pl.when(k == 0)
        def _():
            o_ref[...] = acc

        @pl.when(k > 0)
        def _():
            o_ref[...] += acc


CAST_BAND_BYTES = 2 * 1024 * 1024


def _mm_tiles(m, n, kdim, cast):
    tm, tn, tk = 1024, 1024, 4096
    if cast is not None:
        steps = (m // tm) * (n // tn) * (kdim // tk)
        if cast[0].shape[1] * cast[0].shape[2] * 4 // steps > CAST_BAND_BYTES:
            tn //= 2
    return tm, tn, tk


def _mm_call(a, w, out_dtype, relu2=False, name="proj", cast=None, w_rows_out=False, zero_fill=()):
    m, kdim = a.shape
    n = w.shape[0] if w_rows_out else w.shape[1]
    tm, tn, tk = _mm_tiles(m, n, kdim, cast)
    nk = kdim // tk
    assert nk == 1 or (out_dtype == F32 and not relu2)
    grid = (m // tm, n // tn, nk)
    in_specs = [pl.BlockSpec((tm, tk), lambda i, j, k: (i, k)),
                pl.BlockSpec((tn, tk), lambda i, j, k: (j, k)) if w_rows_out else
                pl.BlockSpec((tk, tn), lambda i, j, k: (k, j))]
    args = [a, w]
    out_specs = [pl.BlockSpec((tm, tn), lambda i, j, k: (i, j))]
    out_shape = [_sds((m, n), out_dtype)]
    bands = None
    if cast is not None:
        c_in, c_out, c_shape, bands = _cast_plan(cast, grid)
        in_specs.append(c_in)
        args.append(cast[0])
        out_specs.append(c_out)
        out_shape.append(c_shape)
    for (z_rows, z_cols), z_dtype in zero_fill:
        n_steps = math.prod(grid)
        zb = min(d for d in range(16, z_rows + 1, 16) if z_rows % d == 0 and z_rows // d <= n_steps)
        out_specs.append(pl.BlockSpec(
            (zb, z_cols),
            lambda i, j, k, nzb=z_rows // zb: (jnp.minimum((i * grid[1] + j) * grid[2] + k, nzb - 1), 0)))
        out_shape.append(_sds((z_rows, z_cols), z_dtype))
    return pl.pallas_call(
        functools.partial(_mm_kernel, grid=grid, relu2=relu2, w_rows_out=w_rows_out, bands=bands,
                          zero_fill=len(zero_fill)),
        grid=grid,
        in_specs=in_specs, out_specs=out_specs, out_shape=out_shape,
        compiler_params=_params(3),
        name=name,
    )(*args)


def _rope_tables(pair):
    t = np.arange(DEC_SEQ)
    row, col = t // GRID_W, t % GRID_W
    lane = np.arange(HD)
    da = 2 * pair
    inv_freq = 1.0 / (ROPE_BASE ** (np.arange(0, da, 2, dtype=np.float64) / da))
    pos = np.where((lane // da) % 2 == 0, row[:, None], col[:, None]).astype(np.float64)
    ang = pos * inv_freq[lane % pair][None, :]
    first = (lane % da) < pair
    cos = np.cos(ang)
    sin = np.sin(ang)
    sin_a = np.where(first[None, :], -sin, 0.0)
    sin_b = np.where(first[None, :], 0.0, sin)
    return tuple(jnp.asarray(a, F32) for a in (cos, sin_a, sin_b))


def _rope(x, cos, sin_a, sin_b, pair):
    return x * cos + pltpu.roll(x, HD - pair, 1) * sin_a + pltpu.roll(x, pair, 1) * sin_b


def _nt_dot(a, b):
    return lax.dot_general(a, b, (((1,), (1,)), ((), ())), preferred_element_type=F32)


def _exp2_rows(segs, sink_row=None):
    m = None
    for n, s in enumerate(segs):
        if n == 0 and sink_row is not None:
            mn = jnp.maximum(jnp.max(jnp.maximum(s[:, :HD], sink_row), axis=-1, keepdims=True),
                             jnp.max(s[:, HD:], axis=-1, keepdims=True))
        else:
            mn = jnp.max(s, axis=-1, keepdims=True)
        m = mn if m is None else jnp.maximum(m, mn)
    es = [jnp.exp2(s - m) for s in segs]
    den = sum(jnp.sum(e, axis=-1, keepdims=True) for e in es)
    if sink_row is not None:
        den = den + jnp.sum(jnp.exp2(sink_row - m), axis=-1, keepdims=True)
    return es, den


def _attn_kernel(*refs, kind, t_len, latent, n_kv, gq, qk_norm, use_sink, emit_kv, n_alias, lambda_init, stack):
    it = iter(refs)
    q_ref, k_ref, v_ref = next(it), next(it), next(it)
    if latent:
        ck_ref, cv_ref, cos_ref, sa_ref, sb_ref = next(it), next(it), next(it), next(it), next(it)
    if kind == "diff":
        lq1_ref, lk1_ref, lq2_ref, lk2_ref, gsub_ref = next(it), next(it), next(it), next(it), next(it)
    if qk_norm:
        gq_ref, gk_ref = next(it), next(it)
    if use_sink:
        sink_ref = next(it)
    for _ in range(n_alias):
        next(it)
    o_ref = next(it)
    if emit_kv:
        ko_ref, vo_ref = next(it), next(it)
    kb_scr, vb_scr = next(it), next(it)

    past = PAST if latent else 0
    pair = HD // 8 if kind == "diff" else HD // 4
    dh = HD // 2 if kind == "diff" else HD
    q_scale = dh ** -0.5 * LOG2E

    for kv in range(n_kv):
        kl = slice(kv * HD, (kv + 1) * HD)
        k = k_ref[:, kl]
        v = v_ref[:, kl]
        if qk_norm:
            k = _rms(k, gk_ref[...])
        if emit_kv:
            ko_ref[:, kl] = k
            vo_ref[:, kl] = v
        if latent:
            k = _rope(k, cos_ref[...], sa_ref[...], sb_ref[...], pair)
            kb_scr[kv, 0:past, :] = ck_ref[:, kl].astype(BF16)
            vb_scr[kv, 0:past, :] = cv_ref[:, kl].astype(BF16)
        kb_scr[kv, past:past + t_len, :] = k.astype(BF16)
        vb_scr[kv, past:past + t_len, :] = v.astype(BF16)

    if kind == "diff":
        lam = (jnp.exp(jnp.sum(lq1_ref[...] * lk1_ref[...], axis=-1, keepdims=True))
               - jnp.exp(jnp.sum(lq2_ref[...] * lk2_ref[...], axis=-1, keepdims=True)) + lambda_init)
        first_map = lax.broadcasted_iota(jnp.int32, (1, HD), 1) < HD // 2

    def scores(qh, qb, kv, sink):
        if kind == "win":
            w_len = 3 * QB
            start = jnp.clip((qb - 1) * QB, 0, t_len - w_len)
            loc = pl.ds(pl.multiple_of(past + start, QB), w_len)
            s_ctx = _nt_dot(qh, kb_scr[kv, 0:past, :])
            s_loc = _nt_dot(qh, kb_scr[kv, loc, :])
            qpos = qb * QB + lax.broadcasted_iota(jnp.int32, (qh.shape[0], 1), 0) % QB
            kpos = start + lax.broadcasted_iota(jnp.int32, (1, w_len), 1)
            s_loc = jnp.where(jnp.abs(qpos - kpos) <= WINDOW, s_loc, -1e30)
            es, den = _exp2_rows([s_ctx, s_loc], sink)
            return es, den, (slice(0, past), loc)
        es, den = _exp2_rows([_nt_dot(qh, kb_scr[kv])], sink)
        return es, den, (slice(None),)

    def values(es, segs, kv):
        return sum(jnp.dot(e.astype(BF16), vb_scr[kv, seg, :], preferred_element_type=F32)
                   for e, seg in zip(es, segs))

    def unit_scores(qb, kv, gs):
        rows = pl.ds(qb * QB, QB) if isinstance(qb, int) else pl.ds(pl.multiple_of(qb * QB, QB), QB)
        head_lanes = [slice((kv * gq + g) * HD, (kv * gq + g + 1) * HD) for g in gs]
        qs = []
        for lanes in head_lanes:
            q = q_ref[rows, lanes]
            if qk_norm:
                q = _rms(q, gq_ref[...])
            if latent:
                q = _rope(q, cos_ref[rows, :], sa_ref[rows, :], sb_ref[rows, :], pair)
            q = q * q_scale
            if kind == "diff":
                qs += [jnp.where(first_map, q, 0.0).astype(BF16), jnp.where(first_map, 0.0, q).astype(BF16)]
            else:
                qs.append(q.astype(BF16))
        sink = None
        if use_sink:
            sink = [jnp.broadcast_to(sink_ref[kv * gq + g] * LOG2E, (QB, HD)) for g in gs]
            sink = jnp.concatenate(sink, axis=0) if len(sink) > 1 else sink[0]
        if stack or len(qs) == 1:
            parts = [scores(jnp.concatenate(qs, axis=0) if len(qs) > 1 else qs[0], qb, kv, sink)]
        else:
            parts = [scores(q, qb, kv, sink) for q in qs]
        return rows, head_lanes, kv, parts

    def unit_values(state):
        rows, head_lanes, kv, parts = state
        outs = [(values(es, segs, kv), den) for es, den, segs in parts]
        if kind == "diff":
            if len(outs) == 1:
                (o, den), = outs
                o1, d1, o2, d2 = o[:QB], den[:QB], o[QB:], den[QB:]
            else:
                (o1, d1), (o2, d2) = outs
            o = o1 * (1.0 / d1) - o2 * (lam / d2)
            o_ref[rows, head_lanes[0]] = (_rms(o, gsub_ref[...]) * (1.0 - lambda_init)).astype(BF16)
            return
        (o, den), = outs
        o = o * (1.0 / den)
        for n, lanes in enumerate(head_lanes):
            o_ref[rows, lanes] = o[n * QB:(n + 1) * QB].astype(BF16)

    n_blk = t_len // QB
    head_groups = (tuple(range(gq)),) if stack else tuple((g,) for g in range(gq))
    per_iter = min(n_blk, 4 if stack and gq > 1 else 8)

    def blocks(qbs, kv, gs):
        if latent:
            for st in [unit_scores(qb, kv, gs) for qb in qbs]:
                unit_values(st)
        else:
            for qb in qbs:
                unit_values(unit_scores(qb, kv, gs))

    for kv in range(n_kv):
        for gs in head_groups:
            if n_blk == per_iter:
                blocks(range(n_blk), kv, gs)
            else:
                def some_blocks(i, carry, kv=kv, gs=gs):
                    blocks([per_iter * i + n for n in range(per_iter)], kv, gs)
                    return carry

                lax.fori_loop(0, n_blk // per_iter, some_blocks, 0)


def _attn_call(kind, proj, mix, layer, latent, cols, mix_col, n_kv, caches=None, tabs=None, lam_vecs=None,
               g_subln=None, qk_gains=None, sink=None, kv_out=None):
    t_len, nb, rb0 = (DEC_SEQ, DEC_BATCH, R_CTX // DEC_SEQ) if latent else (SEQ, BATCH, 0)
    gq = 1 if kind == "diff" else GQ
    kv_total = N_HEADS // gq
    c_q, c_k, c_v = cols
    qw, kw = n_kv * gq * HD, n_kv * HD
    grid = (nb, kv_total // n_kv)

    def tok(width, c0):
        return pl.BlockSpec((t_len, width), lambda b, h: (rb0 + b, c0 // width + h))

    in_specs = [tok(qw, c_q), tok(kw, c_k), tok(kw, c_v)]
    args = [proj, proj, proj]
    if latent:
        cache = pl.BlockSpec((None, None, PAST, kw), lambda b, h: (b, layer, 0, h))
        tab = pl.BlockSpec((t_len, HD), lambda b, h: (0, 0))
        in_specs += [cache, cache, tab, tab, tab]
        args += [c.reshape(DEC_BATCH, DEPTH, PAST, kv_total * HD) for c in caches] + list(tabs)
    if kind == "diff":
        vec = pl.BlockSpec((None, 1, HD // 2), lambda b, h: (layer, 0, 0))
        in_specs += [vec] * 4 + [pl.BlockSpec((None, 1, HD), lambda b, h: (layer, 0, 0))]
        args += [v.reshape(DEPTH, 1, HD // 2) for v in lam_vecs] + [g_subln.reshape(DEPTH, 1, HD)]
    if qk_gains is not None:
        gain = pl.BlockSpec((None, 1, HD), lambda b, h: (layer, 0, 0))
        in_specs += [gain, gain]
        args += [g.reshape(DEPTH, 1, HD) for g in qk_gains]
    if sink is not None:
        in_specs.append(pl.BlockSpec((None, n_kv * GQ, 1, HD), lambda b, h: (layer, h, 0, 0)))
        args.append(jnp.pad(sink.reshape(DEPTH, N_HEADS, 1, 1), ((0, 0), (0, 0), (0, 0), (0, HD - 1)),
                            constant_values=-1e30))
    aliases = {len(args): 0}
    in_specs.append(pl.BlockSpec(memory_space=pl.ANY))
    args.append(mix)
    out_specs = [tok(qw, mix_col)]
    out_shape = [_sds((R, D), BF16)]
    emit_kv = kv_out is not None
    if emit_kv:
        for n, prev in enumerate(kv_out):
            aliases[len(args)] = 1 + n
            in_specs.append(pl.BlockSpec(memory_space=pl.ANY))
            args.append(prev)
        kv_spec = pl.BlockSpec((None, None, SEQ, kw), lambda b, h: (b, layer, 0, h))
        out_specs += [kv_spec, kv_spec]
        out_shape += [_sds((BATCH, DEPTH, SEQ, kv_total * HD), F32)] * 2
    s_len = t_len + (PAST if latent else 0)
    return pl.pallas_call(
        functools.partial(_attn_kernel, kind=kind, t_len=t_len, latent=latent, n_kv=n_kv, gq=gq,
                          qk_norm=qk_gains is not None, use_sink=sink is not None, emit_kv=emit_kv,
                          n_alias=len(aliases), lambda_init=0.8 - 0.6 * math.exp(-0.3 * layer),
                          stack=not latent or kind == "win"),
        grid=grid,
        in_specs=in_specs, out_specs=out_specs, out_shape=out_shape,
        scratch_shapes=[pltpu.VMEM((n_kv, s_len, HD), BF16), pltpu.VMEM((n_kv, s_len, HD), BF16)],
        input_output_aliases=aliases,
        compiler_params=_params(2),
        name=kind + ("_lat" if latent else "_ctx"),
    )(*args)


def _conv_kernel(x_ref, w_ref, b_ref, o_ref, *, t_len):
    u = x_ref[...]
    w = w_ref[...]
    n_rows = u.shape[0]
    t = lax.broadcasted_iota(jnp.int32, (n_rows, 1), 0) % t_len
    half = D_CONV // 2
    acc = u * w[half:half + 1, :] + b_ref[...]
    for s in range(1, half + 1):
        before = jnp.where(t >= s, pltpu.roll(u, s, 0), 0.0)
        after = jnp.where(t < t_len - s, pltpu.roll(u, n_rows - s, 0), 0.0)
        acc = acc + before * w[half - s:half - s + 1, :] + after * w[half + s:half + s + 1, :]
    o_ref[...] = acc * _sigmoid(acc)


def _conv_call(proj, conv_w, conv_b, layer, latent):
    t_len, n_rows, row0 = (DEC_SEQ, R_LAT, R_CTX) if latent else (SEQ, R_CTX, 0)
    tc, tr = 512, 1024
    rb0 = row0 // tr
    return pl.pallas_call(
        functools.partial(_conv_kernel, t_len=t_len),
        grid=(n_rows // tr, CONV_DIM // tc),
        in_specs=[pl.BlockSpec((tr, tc), lambda b, j: (rb0 + b, C_XBC // tc + j)),
                  pl.BlockSpec((None, D_CONV, tc), lambda b, j: (layer, 0, j)),
                  pl.BlockSpec((None, 1, tc), lambda b, j: (layer, 0, j))],
        out_specs=pl.BlockSpec((tr, tc), lambda b, j: (b, j)),
        out_shape=_sds((n_rows, CONV_DIM), F32),
        compiler_params=_params(2),
        name="conv_lat" if latent else "conv_ctx",
    )(proj, conv_w, conv_b.reshape(DEPTH, 1, CONV_DIM))


def _split3(v):
    hi = v.astype(BF16).astype(F32)
    r1 = v - hi
    mid = r1.astype(BF16).astype(F32)
    lo = (r1 - mid).astype(BF16).astype(F32)
    return hi, mid, lo


def _exact_dot(tri, v, tri_first):
    parts = [p.astype(BF16) for p in _split3(v)]
    if tri_first:
        return sum(jnp.dot(tri, p, preferred_element_type=F32) for p in parts)
    return sum(jnp.dot(p, tri, preferred_element_type=F32) for p in parts)


def _lane_spread(v, width):
    hi, mid, lo = _split3(v)
    packed = (hi + pltpu.roll(mid, SSM_H, 1) + pltpu.roll(lo, 2 * SSM_H, 1)).astype(BF16)
    n = SSM_H * width
    src = lax.broadcasted_iota(jnp.int32, (HD, n), 0)
    dst = lax.broadcasted_iota(jnp.int32, (HD, n), 1)
    sel = jnp.where((src < 3 * SSM_H) & (src % SSM_H == dst // width), 1.0, 0.0).astype(BF16)
    return jnp.dot(packed, sel, preferred_element_type=F32)


def _softplus(x):
    return jnp.maximum(x, 0.0) + jnp.log1p(jnp.exp(-jnp.abs(x)))


def _scan_chunk(direction, xa_ref, dt_ref, bias_ref, alog_ref, ht_scr, y_ref):
    ii = lax.broadcasted_iota(jnp.int32, (CHUNK, CHUNK), 0)
    jj = lax.broadcasted_iota(jnp.int32, (CHUNK, CHUNK), 1)
    if direction == 0:
        keep, keep_t = jj <= ii, ii <= jj
        edge = CHUNK - 1
    else:
        keep, keep_t = jj >= ii, ii >= jj
        edge = 0
    tri_col = jnp.where(keep, 1.0, 0.0).astype(BF16)
    tri_row = jnp.where(keep_t, 1.0, 0.0).astype(BF16)

    raw = dt_ref[...]
    if direction == 1:
        raw = pltpu.roll(raw, HD - SSM_H, 1)
    head_lane = lax.broadcasted_iota(jnp.int32, (1, HD), 1) < SSM_H
    dt_c = jnp.where(head_lane, _softplus(raw + bias_ref[...]), 0.0)
    da_c = dt_c * (-jnp.exp(alog_ref[...]))
    a_col = _exact_dot(tri_col, da_c, True)
    a_row = _exact_dot(tri_row, da_c.T[0:SSM_H, :], False)

    a_sq = _lane_spread(a_col, CHUNK)
    a_hp = _lane_spread(a_col, SSM_P)
    dt_hp = _lane_spread(dt_c, SSM_P)

    xa = xa_ref[...]
    xdt = xa[:, :SSM_INNER] * dt_hp
    a_tot = a_hp[edge:edge + 1, :]
    grow = jnp.exp(a_hp)
    w_end = (xdt * jnp.exp(a_tot - a_hp)).astype(BF16)
    carry = jnp.exp(a_tot)
    lower_half = lax.broadcasted_iota(jnp.int32, (1, 2 * SSM_P), 1) < SSM_P

    hg = SSM_H // SSM_G
    gw = hg * SSM_P
    for g in range(SSM_G):
        bm = xa[:, SSM_INNER + g * SSM_N:SSM_INNER + (g + 1) * SSM_N].astype(BF16)
        cm = xa[:, SSM_INNER + (SSM_G + g) * SSM_N:SSM_INNER + (SSM_G + g + 1) * SSM_N].astype(BF16)
        cb = _nt_dot(cm, bm)
        gl = slice(g * gw, (g + 1) * gw)
        ht = ht_scr[:, gl]
        y_off = jnp.dot(cm, ht.astype(BF16), preferred_element_type=F32) * grow[:, gl]
        for pr in range(hg // 2):
            pl_ = slice(g * gw + pr * 2 * SSM_P, g * gw + (pr + 1) * 2 * SSM_P)
            x_pair = xdt[:, pl_]
            y_pair = y_off[:, pr * 2 * SSM_P:(pr + 1) * 2 * SSM_P]
            for half in range(2):
                h = g * hg + pr * 2 + half
                decay = jnp.exp(jnp.where(keep, a_sq[:, h * CHUNK:(h + 1) * CHUNK] - a_row[h:h + 1, :], -1e30))
                rhs = jnp.where(lower_half if half == 0 else ~lower_half, x_pair, 0.0).astype(BF16)
                y_pair = y_pair + jnp.dot((cb * decay).astype(BF16), rhs, preferred_element_type=F32)
            y_ref[:, pl_] = y_pair
        st = lax.dot_general(bm, w_end[:, gl], (((0,), (0,)), ((), ())), preferred_element_type=F32)
        ht_scr[:, gl] = ht * carry[:, gl] + st


def _scan_kernel(*refs, has_h0, emit_state, n_chunks):
    it = iter(refs)
    ins = [[next(it) for _ in range(4)] for _ in range(2)]
    if has_h0:
        h0_ref = next(it)
    if emit_state:
        next(it)
    y_refs = [next(it), next(it)]
    if emit_state:
        hend_ref = next(it)
    ht_scr = next(it)
    c = pl.program_id(1)

    @pl.when(c == 0)
    def _():
        for d in range(2):
            if has_h0:
                ht_scr[d] = h0_ref[d].reshape(SSM_INNER, SSM_N).T
            else:
                ht_scr[d] = jnp.zeros(ht_scr.shape[1:], F32)

    for d in range(2):
        _scan_chunk(d, *ins[d], ht_scr.at[d], y_refs[d])

    if emit_state:
        @pl.when(c == n_chunks - 1)
        def _():
            for d in range(2):
                hend_ref[d] = ht_scr[d].T.reshape(SSM_H, SSM_P, SSM_N)


def _pad_lanes(v):
    return jnp.pad(v, ((0, 0), (0, 0), (0, HD - SSM_H))).reshape(DEPTH, 2, 1, HD)


def _scan_call(xact, proj, dt_bias, a_log, state, layer, latent, state_out=None):
    t_len, nb, row0 = (DEC_SEQ, DEC_BATCH, R_CTX) if latent else (SEQ, BATCH, 0)
    nc = t_len // CHUNK
    rb0 = row0 // CHUNK
    chunk_of = (lambda c: c, lambda c: nc - 1 - c)
    in_specs, args = [], []
    for d in range(2):
        in_specs += [pl.BlockSpec((CHUNK, CONV_DIM), lambda b, c, d=d: (b * nc + chunk_of[d](c), 0)),
                     pl.BlockSpec((CHUNK, HD), lambda b, c, d=d: (rb0 + b * nc + chunk_of[d](c), C_DT // HD)),
                     pl.BlockSpec((None, None, 1, HD), lambda b, c, d=d: (layer, d, 0, 0)),
                     pl.BlockSpec((None, None, 1, HD), lambda b, c, d=d: (layer, d, 0, 0))]
        args += [xact, proj, _pad_lanes(dt_bias), _pad_lanes(a_log)]
    state_spec = pl.BlockSpec((None, None, 2, SSM_H, SSM_P, SSM_N), lambda b, c: (b, layer, 0, 0, 0, 0))
    if latent:
        in_specs.append(state_spec)
        args.append(state)
    out_specs = [pl.BlockSpec((CHUNK, SSM_INNER), lambda b, c, d=d: (b * nc + chunk_of[d](c), 0)) for d in range(2)]
    out_shape = [_sds((nb * t_len, SSM_INNER), F32)] * 2
    aliases = {}
    emit_state = state_out is not None
    if emit_state:
        aliases[len(args)] = 2
        in_specs.append(pl.BlockSpec(memory_space=pl.ANY))
        args.append(state_out)
        out_specs.append(state_spec)
        out_shape.append(_sds((nb, DEPTH, 2, SSM_H, SSM_P, SSM_N), F32))
    return pl.pallas_call(
        functools.partial(_scan_kernel, has_h0=latent, emit_state=emit_state, n_chunks=nc),
        grid=(nb, nc),
        in_specs=in_specs, out_specs=out_specs, out_shape=out_shape,
        scratch_shapes=[pltpu.VMEM((2, SSM_N, SSM_INNER), F32)],
        input_output_aliases=aliases,
        compiler_params=_params(2),
        name="ssd_lat" if latent else "ssd_ctx",
    )(*args)


def _ssd_out_kernel(yf_ref, yb_ref, xs_ref, z_ref, dskip_ref, g_ref, mix_ref, o_ref):
    del mix_ref
    dsk = dskip_ref[0:1, :] + dskip_ref[1:2, :]
    z = z_ref[...]
    y = (yf_ref[...] + yb_ref[...] + dsk * xs_ref[...]) * (z * _sigmoid(z))
    gw = SSM_INNER // SSM_G
    g = g_ref[...]
    for k in range(SSM_G):
        lanes = slice(k * gw, (k + 1) * gw)
        yk = y[:, lanes]
        yk = yk * lax.rsqrt(jnp.mean(yk * yk, axis=-1, keepdims=True) + EPS)
        o_ref[:, lanes] = (yk * g[:, lanes]).astype(BF16)


def _ssd_out_call(yf, yb, xact, proj, d_skip, g_norm, mix, layer, latent):
    tm = 256
    rb0 = (R_CTX if latent else 0) // tm
    n_rows = yf.shape[0]
    local = pl.BlockSpec((tm, SSM_INNER), lambda i: (i, 0))
    return pl.pallas_call(
        _ssd_out_kernel,
        grid=(n_rows // tm,),
        in_specs=[local, local, local,
                  pl.BlockSpec((tm, SSM_INNER), lambda i: (rb0 + i, C_Z // SSM_INNER)),
                  pl.BlockSpec((None, 2, SSM_INNER), lambda i: (layer, 0, 0)),
                  pl.BlockSpec((None, 1, SSM_INNER), lambda i: (layer, 0, 0)),
                  pl.BlockSpec(memory_space=pl.ANY)],
        out_specs=pl.BlockSpec((tm, SSM_INNER), lambda i: (rb0 + i, 3)),
        out_shape=_sds((R, D), BF16),
        input_output_aliases={6: 0},
        compiler_params=_params(1),
        name="ssd_out_lat" if latent else "ssd_out_ctx",
    )(yf, yb, xact, proj, jnp.repeat(d_skip, SSM_P, axis=-1), g_norm.reshape(DEPTH, 1, SSM_INNER), mix)


def kernel(x_prompt, x_sample, c, cache_a_k, cache_a_v, cache_b_k, cache_b_v, cache_c_k, cache_c_v, state_ssm, c_ctx, w_mod, b_mod, g_pre_mix, g_post_mix, g_pre_ffn, g_post_ffn, w_in, w_out, lam_q1, lam_k1, lam_q2, lam_k2, g_subln, g_qnorm, g_knorm, sink, conv_w, conv_b, dt_bias, a_log, d_skip, g_ssm_norm, w_up, w_down):
    x = (x_prompt.reshape(R_CTX, D), x_sample.reshape(R_LAT, D))
    cvec = jnp.concatenate([c_ctx[None, :], c, jnp.zeros((8 - 1 - DEC_BATCH, D), F32)], axis=0)
    w_in_t = jnp.swapaxes(w_in, 1, 2)
    mods, w_in_b = _mod_call(cvec, w_mod, b_mod, (w_in_t, 0, IN_WP))
    mods4 = mods.reshape(DEPTH, 8, 1, N_MOD * D)

    tabs_a = _rope_tables(HD // 8)
    tabs_bc = _rope_tables(HD // 4)
    lam_vecs = (lam_q1, lam_k1, lam_q2, lam_k2)

    h, w_out_b = _norm_call(x, mods4, nxt=(g_pre_mix, 0, 1, 0), casts=((w_out, 0, D),))
    kv_rows = BATCH * DEPTH * SEQ
    zero_bufs = [((R, D), BF16)]
    zero_bufs += [((kv_rows, N_HEADS * HD), F32)] * 2 + [((kv_rows, N_KV * HD), F32)] * 4
    zero_bufs += [((BATCH * DEPTH * 2 * SSM_H * SSM_P, SSM_N), F32)]
    for layer in range(DEPTH):
        last = layer + 1 == DEPTH
        proj, w_up_b, mix, *zeros = _mm_call(h, w_in_b, F32, name="in_proj", cast=(w_up, layer, D),
                                             w_rows_out=True, zero_fill=zero_bufs if layer == 0 else zero_bufs[:1])
        if layer == 0:
            kv_a = [z.reshape(BATCH, DEPTH, SEQ, -1) for z in zeros[0:2]]
            kv_b = [z.reshape(BATCH, DEPTH, SEQ, -1) for z in zeros[2:4]]
            kv_c = [z.reshape(BATCH, DEPTH, SEQ, -1) for z in zeros[4:6]]
            ssm_new = zeros[6].reshape(BATCH, DEPTH, 2, SSM_H, SSM_P, SSM_N)

        mix, *kv_a = _attn_call("diff", proj, mix, layer, False, (C_QA, C_KA, C_VA), 0, N_HEADS,
                                lam_vecs=lam_vecs, g_subln=g_subln, kv_out=kv_a)
        mix, *kv_b = _attn_call("gqa", proj, mix, layer, False, (C_QB, C_KB, C_VB), 1024, N_KV,
                                qk_gains=(g_qnorm, g_knorm), kv_out=kv_b)
        mix, *kv_c = _attn_call("gqa", proj, mix, layer, False, (C_QC, C_KC, C_VC), 2048, 1,
                                sink=sink, kv_out=kv_c)
        mix, = _attn_call("diff", proj, mix, layer, True, (C_QA, C_KA, C_VA), 0, 1,
                          caches=(cache_a_k, cache_a_v), tabs=tabs_a, lam_vecs=lam_vecs, g_subln=g_subln)
        mix, = _attn_call("gqa", proj, mix, layer, True, (C_QB, C_KB, C_VB), 1024, 1,
                          caches=(cache_b_k, cache_b_v), tabs=tabs_bc, qk_gains=(g_qnorm, g_knorm))
        mix, = _attn_call("win", proj, mix, layer, True, (C_QC, C_KC, C_VC), 2048, 1,
                          caches=(cache_c_k, cache_c_v), tabs=tabs_bc, sink=sink)

        for latent in (False, True):
            xact = _conv_call(proj, conv_w, conv_b, layer, latent)
            y_f, y_b, *st = _scan_call(xact, proj, dt_bias, a_log, state_ssm, layer, latent,
                                       state_out=None if latent else ssm_new)
            if not latent:
                (ssm_new,) = st
            mix = _ssd_out_call(y_f, y_b, xact, proj, d_skip, g_ssm_norm, mix, layer, latent)

        if last:
            (mm,) = _mm_call(mix, w_out_b, F32, name="out_proj")
        else:
            mm, w_out_b = _mm_call(mix, w_out_b, F32, name="out_proj", cast=(w_out, layer + 1, D))
        x, h2 = _norm_call(x, mods4, mm=mm, post=(g_post_mix, layer, 2), nxt=(g_pre_ffn, layer, 4, 3))
        u, w_down_b = _mm_call(h2, w_up_b, BF16, relu2=True, name="ffn_up", cast=(w_down, layer, D_FF))
        if last:
            (mm,) = _mm_call(u, w_down_b, F32, name="ffn_down")
            x_ctx, x_lat = _norm_call(x, mods4, mm=mm, post=(g_post_ffn, layer, 5), split_out=True)
        else:
            mm, w_in_b = _mm_call(u, w_down_b, F32, name="ffn_down", cast=(w_in_t, layer + 1, IN_WP))
            x, h = _norm_call(x, mods4, mm=mm, post=(g_post_ffn, layer, 5), nxt=(g_pre_mix, layer + 1, 1, 0))

    new_kv = [a.reshape(BATCH, DEPTH, SEQ, -1, HD) for a in (*kv_a, *kv_b, *kv_c)]
    return (x_ctx.reshape(BATCH, SEQ, D), x_lat.reshape(DEC_BATCH, DEC_SEQ, D), *new_kv, ssm_new)
```

```python
import functools
import math

import numpy as np
import jax
import jax.numpy as jnp
from jax import lax
from jax.experimental import pallas as pl
from jax.experimental.pallas import tpu as pltpu

F32 = jnp.float32
BF16 = jnp.bfloat16

D = 4096
BATCH, SEQ = 16, 256
DEC_BATCH, DEC_SEQ = 4, 1024
PAST = 256
DEPTH = 2
GRID_W = 64
HD = 128
N_HEADS = 8
N_KV = 2
GQ = N_HEADS // N_KV
WINDOW = 128
QB = 128
SSM_H, SSM_P, SSM_N, SSM_G = 16, 64, 128, 2
SSM_INNER = SSM_H * SSM_P
CHUNK = 128
D_CONV = 5
CONV_DIM = SSM_INNER + 2 * SSM_G * SSM_N
D_FF = 4 * D
EPS = 1e-6
ROPE_BASE = 10000.0
N_MOD = 6
LOG2E = 1.4426950408889634

R_CTX = BATCH * SEQ
R_LAT = DEC_BATCH * DEC_SEQ
R = R_CTX + R_LAT

C_QA, C_KA, C_VA = 0, 1024, 2048
C_QB, C_KB, C_VB = 3072, 4096, 4352
C_QC, C_KC, C_VC = 4608, 5632, 5888
C_Z, C_XBC, C_DT = 6144, 7168, 8704
IN_W = C_DT + 2 * SSM_H
IN_WP = 9216

VMEM_LIMIT = 56 * 1024 * 1024


def _sds(shape, dtype):
    return jax.ShapeDtypeStruct(shape, dtype)


def _params(n_grid, vmem=VMEM_LIMIT):
    return pltpu.CompilerParams(dimension_semantics=("arbitrary",) * n_grid, vmem_limit_bytes=vmem)


def _rms(x, g):
    return x * lax.rsqrt(jnp.mean(x * x, axis=-1, keepdims=True) + EPS) * g


def _sigmoid(x):
    return 1.0 / (1.0 + jnp.exp(-x))


def _mod_row(i, tm):
    r = i * tm
    return jnp.where(r >= R_CTX, 1 + (r - R_CTX) // DEC_SEQ, 0)


def _linear_step(grid):
    step = pl.program_id(0)
    for axis in range(1, len(grid)):
        step = step * grid[axis] + pl.program_id(axis)
    return step


def _cast_plan(cast, grid):
    cw, layer, out_rows = cast
    rows, cols = cw.shape[1:]
    n_steps = math.prod(grid)
    common = math.gcd(rows, out_rows)
    rb = min(d for d in range(16, common + 1, 16) if common % d == 0 and out_rows // d <= n_steps)
    n_live, n_bands = rows // rb, out_rows // rb

    def step_of(*idx):
        step = idx[0]
        for axis in range(1, len(grid)):
            step = step * grid[axis] + idx[axis]
        return step

    return (pl.BlockSpec((None, rb, cols), lambda *idx: (layer, jnp.minimum(step_of(*idx), n_live - 1), 0)),
            pl.BlockSpec((rb, cols), lambda *idx: (jnp.minimum(step_of(*idx), n_bands - 1), 0)),
            _sds((out_rows, cols), BF16), (n_live, n_bands))


def _cast_band(cw_ref, cwo_ref, grid, bands):
    n_live, n_bands = bands
    if n_live == n_bands:
        cwo_ref[...] = cw_ref[...].astype(BF16)
        return
    step = _linear_step(grid)

    @pl.when(step < n_live)
    def _():
        cwo_ref[...] = cw_ref[...].astype(BF16)

    @pl.when(step >= n_live)
    def _():
        cwo_ref[...] = jnp.zeros(cwo_ref.shape, BF16)


def _mod_kernel(c_ref, w_ref, b_ref, cw_ref, o_ref, cwo_ref, *, grid, bands):
    c = c_ref[...]
    s = (c * _sigmoid(c)).astype(BF16)
    o_ref[...] = jnp.dot(s, w_ref[...].astype(BF16), preferred_element_type=F32) + b_ref[...]
    _cast_band(cw_ref, cwo_ref, grid, bands)


def _mod_call(cvec, w_mod, b_mod, cast):
    tn = 512
    grid = (DEPTH, N_MOD * D // tn)
    c_in, c_out, c_shape, bands = _cast_plan(cast, grid)
    return pl.pallas_call(
        functools.partial(_mod_kernel, grid=grid, bands=bands),
        grid=grid,
        in_specs=[pl.BlockSpec((8, D), lambda l, j: (0, 0)),
                  pl.BlockSpec((None, D, tn), lambda l, j: (l, 0, j)),
                  pl.BlockSpec((None, 1, tn), lambda l, j: (l, 0, j)), c_in],
        out_specs=[pl.BlockSpec((None, 8, tn), lambda l, j: (l, 0, j)), c_out],
        out_shape=[_sds((DEPTH, 8, N_MOD * D), F32), c_shape],
        compiler_params=_params(2),
        name="modulation",
    )(cvec, w_mod, b_mod.reshape(DEPTH, 1, N_MOD * D), cast[0])


TM_ROW = 256
N_CTX_TILES = R_CTX // TM_ROW


def _mod_spec(layer, chunk, tm):
    return pl.BlockSpec((None, None, 1, D), lambda i: (layer, _mod_row(i, tm), 0, chunk))


def _gain_spec(layer):
    return pl.BlockSpec((None, 1, D), lambda i: (layer, 0, 0))


_ROW = pl.BlockSpec((TM_ROW, D), lambda i: (i, 0))
_ROW_CTX = pl.BlockSpec((TM_ROW, D), lambda i: (jnp.minimum(i, N_CTX_TILES - 1), 0))
_ROW_LAT = pl.BlockSpec((TM_ROW, D), lambda i: (jnp.maximum(i - N_CTX_TILES, 0), 0))


def _norm_kernel(*refs, has_mm, split_in, with_next, split_out, cast_bands):
    it = iter(refs)
    if has_mm:
        mm_ref = next(it)
    x_refs = (next(it), next(it)) if split_in else (next(it),)
    if has_mm:
        gpost_ref, gt_ref = next(it), next(it)
    if with_next:
        gpre_ref, sc_ref, sh_ref = next(it), next(it), next(it)
    cw_refs = [next(it) for _ in cast_bands]
    if has_mm:
        xo_refs = (next(it), next(it)) if split_out else (next(it),)
    if with_next:
        h_ref = next(it)
    for cw_ref, bands in zip(cw_refs, cast_bands):
        _cast_band(cw_ref, next(it), (R // TM_ROW,), bands)

    def body(x_ref, xo_ref):
        x = x_ref[...]
        if has_mm:
            x = x + gt_ref[...] * _rms(mm_ref[...], gpost_ref[...])
            xo_ref[...] = x
        if with_next:
            h_ref[...] = (_rms(x, gpre_ref[...]) * (1.0 + sc_ref[...]) + sh_ref[...]).astype(BF16)

    if split_in or split_out:
        i = pl.program_id(0)

        @pl.when(i < N_CTX_TILES)
        def _():
            body(x_refs[0], xo_refs[0] if has_mm else None)

        @pl.when(i >= N_CTX_TILES)
        def _():
            body(x_refs[-1], xo_refs[-1] if has_mm else None)
    else:
        body(x_refs[0], xo_refs[0] if has_mm else None)


def _norm_call(x, mods4, mm=None, post=None, nxt=None, split_out=False, casts=()):
    split_in = isinstance(x, tuple)
    in_specs, args = [], []
    if mm is not None:
        in_specs.append(_ROW)
        args.append(mm)
    in_specs += [_ROW_CTX, _ROW_LAT] if split_in else [_ROW]
    args += list(x) if split_in else [x]
    out_specs, out_shape = [], []
    if mm is not None:
        g_post, pl_layer, gate_chunk = post
        in_specs += [_gain_spec(pl_layer), _mod_spec(pl_layer, gate_chunk, TM_ROW)]
        args += [g_post.reshape(DEPTH, 1, D), mods4]
        if split_out:
            out_specs += [_ROW_CTX, _ROW_LAT]
            out_shape += [_sds((R_CTX, D), F32), _sds((R_LAT, D), F32)]
        else:
            out_specs.append(_ROW)
            out_shape.append(_sds((R, D), F32))
    if nxt is not None:
        g_pre, nl, sc_chunk, sh_chunk = nxt
        in_specs += [_gain_spec(nl), _mod_spec(nl, sc_chunk, TM_ROW), _mod_spec(nl, sh_chunk, TM_ROW)]
        args += [g_pre.reshape(DEPTH, 1, D), mods4, mods4]
        out_specs.append(_ROW)
        out_shape.append(_sds((R, D), BF16))
    cast_bands = []
    for cast in casts:
        c_in, c_out, c_shape, bands = _cast_plan(cast, (R // TM_ROW,))
        in_specs.append(c_in)
        args.append(cast[0])
        out_specs.append(c_out)
        out_shape.append(c_shape)
        cast_bands.append(bands)
    return pl.pallas_call(
        functools.partial(_norm_kernel, has_mm=mm is not None, split_in=split_in, with_next=nxt is not None,
                          split_out=split_out, cast_bands=tuple(cast_bands)),
        grid=(R // TM_ROW,),
        in_specs=in_specs, out_specs=out_specs, out_shape=out_shape,
        compiler_params=_params(1),
        name="post_norm" if mm is not None else "pre_norm",
    )(*args)


def _mm_kernel(*refs, grid, relu2, w_rows_out, bands, zero_fill):
    it = iter(refs)
    a_ref, w_ref = next(it), next(it)
    cw_ref = next(it) if bands else None
    o_ref = next(it)
    if bands:
        _cast_band(cw_ref, next(it), grid, bands)
    for _ in range(zero_fill):
        z_ref = next(it)
        z_ref[...] = jnp.zeros(z_ref.shape, z_ref.dtype)
    nk = grid[2]
    if w_rows_out:
        acc = _nt_dot(a_ref[...], w_ref[...])
    else:
        acc = jnp.dot(a_ref[...], w_ref[...], preferred_element_type=F32)
    if nk == 1:
        if relu2:
            u = jnp.maximum(acc, 0.0)
            acc = u * u
        o_ref[...] = acc.astype(o_ref.dtype)
    else:
        k = pl.program_id(2)

        @pl.when(k == 0)
        def _():
            o_ref[...] = acc

        @pl.when(k > 0)
        def _():
            o_ref[...] += acc


CAST_BAND_BYTES = 2 * 1024 * 1024


def _mm_tiles(m, n, kdim, cast):
    tm, tn, tk = 1024, 1024, 4096
    if cast is not None:
        steps = (m // tm) * (n // tn) * (kdim // tk)
        if cast[0].shape[1] * cast[0].shape[2] * 4 // steps > CAST_BAND_BYTES:
            tn //= 2
    return tm, tn, tk


def _mm_call(a, w, out_dtype, relu2=False, name="proj", cast=None, w_rows_out=False, zero_fill=()):
    m, kdim = a.shape
    n = w.shape[0] if w_rows_out else w.shape[1]
    tm, tn, tk = _mm_tiles(m, n, kdim, cast)
    nk = kdim // tk
    assert nk == 1 or (out_dtype == F32 and not relu2)
    grid = (m // tm, n // tn, nk)
    in_specs = [pl.BlockSpec((tm, tk), lambda i, j, k: (i, k)),
                pl.BlockSpec((tn, tk), lambda i, j, k: (j, k)) if w_rows_out else
                pl.BlockSpec((tk, tn), lambda i, j, k: (k, j))]
    args = [a, w]
    out_specs = [pl.BlockSpec((tm, tn), lambda i, j, k: (i, j))]
    out_shape = [_sds((m, n), out_dtype)]
    bands = None
    if cast is not None:
        c_in, c_out, c_shape, bands = _cast_plan(cast, grid)
        in_specs.append(c_in)
        args.append(cast[0])
        out_specs.append(c_out)
        out_shape.append(c_shape)
    for (z_rows, z_cols), z_dtype in zero_fill:
        n_steps = math.prod(grid)
        zb = min(d for d in range(16, z_rows + 1, 16) if z_rows % d == 0 and z_rows // d <= n_steps)
        out_specs.append(pl.BlockSpec(
            (zb, z_cols),
            lambda i, j, k, nzb=z_rows // zb: (jnp.minimum((i * grid[1] + j) * grid[2] + k, nzb - 1), 0)))
        out_shape.append(_sds((z_rows, z_cols), z_dtype))
    return pl.pallas_call(
        functools.partial(_mm_kernel, grid=grid, relu2=relu2, w_rows_out=w_rows_out, bands=bands,
                          zero_fill=len(zero_fill)),
        grid=grid,
        in_specs=in_specs, out_specs=out_specs, out_shape=out_shape,
        compiler_params=_params(3),
        name=name,
    )(*args)


def _rope_tables(pair):
    t = np.arange(DEC_SEQ)
    row, col = t // GRID_W, t % GRID_W
    lane = np.arange(HD)
    da = 2 * pair
    inv_freq = 1.0 / (ROPE_BASE ** (np.arange(0, da, 2, dtype=np.float64) / da))
    pos = np.where((lane // da) % 2 == 0, row[:, None], col[:, None]).astype(np.float64)
    ang = pos * inv_freq[lane % pair][None, :]
    first = (lane % da) < pair
    cos = np.cos(ang)
    sin = np.sin(ang)
    sin_a = np.where(first[None, :], -sin, 0.0)
    sin_b = np.where(first[None, :], 0.0, sin)
    return tuple(jnp.asarray(a, F32) for a in (cos, sin_a, sin_b))


def _rope(x, cos, sin_a, sin_b, pair):
    return x * cos + pltpu.roll(x, HD - pair, 1) * sin_a + pltpu.roll(x, pair, 1) * sin_b


def _nt_dot(a, b):
    return lax.dot_general(a, b, (((1,), (1,)), ((), ())), preferred_element_type=F32)


def _exp2_rows(segs, sink_row=None):
    m = None
    for n, s in enumerate(segs):
        if n == 0 and sink_row is not None:
            mn = jnp.maximum(jnp.max(jnp.maximum(s[:, :HD], sink_row), axis=-1, keepdims=True),
                             jnp.max(s[:, HD:], axis=-1, keepdims=True))
        else:
            mn = jnp.max(s, axis=-1, keepdims=True)
        m = mn if m is None else jnp.maximum(m, mn)
    es = [jnp.exp2(s - m) for s in segs]
    den = sum(jnp.sum(e, axis=-1, keepdims=True) for e in es)
    if sink_row is not None:
        den = den + jnp.sum(jnp.exp2(sink_row - m), axis=-1, keepdims=True)
    return es, den


def _attn_kernel(*refs, kind, t_len, latent, n_kv, gq, qk_norm, use_sink, emit_kv, n_alias, lambda_init, stack):
    it = iter(refs)
    q_ref, k_ref, v_ref = next(it), next(it), next(it)
    if latent:
        ck_ref, cv_ref, cos_ref, sa_ref, sb_ref = next(it), next(it), next(it), next(it), next(it)
    if kind == "diff":
        lq1_ref, lk1_ref, lq2_ref, lk2_ref, gsub_ref = next(it), next(it), next(it), next(it), next(it)
    if qk_norm:
        gq_ref, gk_ref = next(it), next(it)
    if use_sink:
        sink_ref = next(it)
    for _ in range(n_alias):
        next(it)
    o_ref = next(it)
    if emit_kv:
        ko_ref, vo_ref = next(it), next(it)
    kb_scr, vb_scr = next(it), next(it)

    past = PAST if latent else 0
    pair = HD // 8 if kind == "diff" else HD // 4
    dh = HD // 2 if kind == "diff" else HD
    q_scale = dh ** -0.5 * LOG2E

    for kv in range(n_kv):
        kl = slice(kv * HD, (kv + 1) * HD)
        k = k_ref[:, kl]
        v = v_ref[:, kl]
        if qk_norm:
            k = _rms(k, gk_ref[...])
        if emit_kv:
            ko_ref[:, kl] = k
            vo_ref[:, kl] = v
        if latent:
            k = _rope(k, cos_ref[...], sa_ref[...], sb_ref[...], pair)
            kb_scr[kv, 0:past, :] = ck_ref[:, kl].astype(BF16)
            vb_scr[kv, 0:past, :] = cv_ref[:, kl].astype(BF16)
        kb_scr[kv, past:past + t_len, :] = k.astype(BF16)
        vb_scr[kv, past:past + t_len, :] = v.astype(BF16)

    if kind == "diff":
        lam = (jnp.exp(jnp.sum(lq1_ref[...] * lk1_ref[...], axis=-1, keepdims=True))
               - jnp.exp(jnp.sum(lq2_ref[...] * lk2_ref[...], axis=-1, keepdims=True)) + lambda_init)
        first_map = lax.broadcasted_iota(jnp.int32, (1, HD), 1) < HD // 2

    def scores(qh, qb, kv, sink):
        if kind == "win":
            w_len = 3 * QB
            start = jnp.clip((qb - 1) * QB, 0, t_len - w_len)
            loc = pl.ds(pl.multiple_of(past + start, QB), w_len)
            s_ctx = _nt_dot(qh, kb_scr[kv, 0:past, :])
            s_loc = _nt_dot(qh, kb_scr[kv, loc, :])
            qpos = qb * QB + lax.broadcasted_iota(jnp.int32, (qh.shape[0], 1), 0) % QB
            kpos = start + lax.broadcasted_iota(jnp.int32, (1, w_len), 1)
            s_loc = jnp.where(jnp.abs(qpos - kpos) <= WINDOW, s_loc, -1e30)
            es, den = _exp2_rows([s_ctx, s_loc], sink)
            return es, den, (slice(0, past), loc)
        es, den = _exp2_rows([_nt_dot(qh, kb_scr[kv])], sink)
        return es, den, (slice(None),)

    def values(es, segs, kv):
        return sum(jnp.dot(e.astype(BF16), vb_scr[kv, seg, :], preferred_element_type=F32)
                   for e, seg in zip(es, segs))

    def unit_scores(qb, kv, gs):
        rows = pl.ds(qb * QB, QB) if isinstance(qb, int) else pl.ds(pl.multiple_of(qb * QB, QB), QB)
        head_lanes = [slice((kv * gq + g) * HD, (kv * gq + g + 1) * HD) for g in gs]
        qs = []
        for lanes in head_lanes:
            q = q_ref[rows, lanes]
            if qk_norm:
                q = _rms(q, gq_ref[...])
            if latent:
                q = _rope(q, cos_ref[rows, :], sa_ref[rows, :], sb_ref[rows, :], pair)
            q = q * q_scale
            if kind == "diff":
                qs += [jnp.where(first_map, q, 0.0).astype(BF16), jnp.where(first_map, 0.0, q).astype(BF16)]
            else:
                qs.append(q.astype(BF16))
        sink = None
        if use_sink:
            sink = [jnp.broadcast_to(sink_ref[kv * gq + g] * LOG2E, (QB, HD)) for g in gs]
            sink = jnp.concatenate(sink, axis=0) if len(sink) > 1 else sink[0]
        if stack or len(qs) == 1:
            parts = [scores(jnp.concatenate(qs, axis=0) if len(qs) > 1 else qs[0], qb, kv, sink)]
        else:
            parts = [scores(q, qb, kv, sink) for q in qs]
        return rows, head_lanes, kv, parts

    def unit_values(state):
        rows, head_lanes, kv, parts = state
        if kind == "diff" and len(parts) == 2:
            (es1, d1, segs), (es2, d2, _) = parts
            p = [e1 * (1.0 / d1) - e2 * (lam / d2) for e1, e2 in zip(es1, es2)]
            o = values(p, segs, kv)
            o_ref[rows, head_lanes[0]] = (_rms(o, gsub_ref[...]) * (1.0 - lambda_init)).astype(BF16)
            return
        outs = [(values(es, segs, kv), den) for es, den, segs in parts]
        if kind == "diff":
            if len(outs) == 1:
                (o, den), = outs
                o1, d1, o2, d2 = o[:QB], den[:QB], o[QB:], den[QB:]
            else:
                (o1, d1), (o2, d2) = outs
            o = o1 * (1.0 / d1) - o2 * (lam / d2)
            o_ref[rows, head_lanes[0]] = (_rms(o, gsub_ref[...]) * (1.0 - lambda_init)).astype(BF16)
            return
        (o, den), = outs
        o = o * (1.0 / den)
        for n, lanes in enumerate(head_lanes):
            o_ref[rows, lanes] = o[n * QB:(n + 1) * QB].astype(BF16)

    n_blk = t_len // QB
    head_groups = (tuple(range(gq)),) if stack else tuple((g,) for g in range(gq))
    per_iter = min(n_blk, 4 if stack and gq > 1 else 8)

    def blocks(qbs, kv, gs):
        if latent:
            for st in [unit_scores(qb, kv, gs) for qb in qbs]:
                unit_values(st)
        else:
            for qb in qbs:
                unit_values(unit_scores(qb, kv, gs))

    for kv in range(n_kv):
        for gs in head_groups:
            if n_blk == per_iter:
                blocks(range(n_blk), kv, gs)
            else:
                def some_blocks(i, carry, kv=kv, gs=gs):
                    blocks([per_iter * i + n for n in range(per_iter)], kv, gs)
                    return carry

                lax.fori_loop(0, n_blk // per_iter, some_blocks, 0)


def _attn_call(kind, proj, mix, layer, latent, cols, mix_col, n_kv, caches=None, tabs=None, lam_vecs=None,
               g_subln=None, qk_gains=None, sink=None, kv_out=None):
    t_len, nb, rb0 = (DEC_SEQ, DEC_BATCH, R_CTX // DEC_SEQ) if latent else (SEQ, BATCH, 0)
    gq = 1 if kind == "diff" else GQ
    kv_total = N_HEADS // gq
    c_q, c_k, c_v = cols
    qw, kw = n_kv * gq * HD, n_kv * HD
    grid = (nb, kv_total // n_kv)

    def tok(width, c0):
        return pl.BlockSpec((t_len, width), lambda b, h: (rb0 + b, c0 // width + h))

    in_specs = [tok(qw, c_q), tok(kw, c_k), tok(kw, c_v)]
    args = [proj, proj, proj]
    if latent:
        cache = pl.BlockSpec((None, None, PAST, kw), lambda b, h: (b, layer, 0, h))
        tab = pl.BlockSpec((t_len, HD), lambda b, h: (0, 0))
        in_specs += [cache, cache, tab, tab, tab]
        args += [c.reshape(DEC_BATCH, DEPTH, PAST, kv_total * HD) for c in caches] + list(tabs)
    if kind == "diff":
        vec = pl.BlockSpec((None, 1, HD // 2), lambda b, h: (layer, 0, 0))
        in_specs += [vec] * 4 + [pl.BlockSpec((None, 1, HD), lambda b, h: (layer, 0, 0))]
        args += [v.reshape(DEPTH, 1, HD // 2) for v in lam_vecs] + [g_subln.reshape(DEPTH, 1, HD)]
    if qk_gains is not None:
        gain = pl.BlockSpec((None, 1, HD), lambda b, h: (layer, 0, 0))
        in_specs += [gain, gain]
        args += [g.reshape(DEPTH, 1, HD) for g in qk_gains]
    if sink is not None:
        in_specs.append(pl.BlockSpec((None, n_kv * GQ, 1, HD), lambda b, h: (layer, h, 0, 0)))
        args.append(jnp.pad(sink.reshape(DEPTH, N_HEADS, 1, 1), ((0, 0), (0, 0), (0, 0), (0, HD - 1)),
                            constant_values=-1e30))
    aliases = {len(args): 0}
    in_specs.append(pl.BlockSpec(memory_space=pl.ANY))
    args.append(mix)
    out_specs = [tok(qw, mix_col)]
    out_shape = [_sds((R, D), BF16)]
    emit_kv = kv_out is not None
    if emit_kv:
        for n, prev in enumerate(kv_out):
            aliases[len(args)] = 1 + n
            in_specs.append(pl.BlockSpec(memory_space=pl.ANY))
            args.append(prev)
        kv_spec = pl.BlockSpec((None, None, SEQ, kw), lambda b, h: (b, layer, 0, h))
        out_specs += [kv_spec, kv_spec]
        out_shape += [_sds((BATCH, DEPTH, SEQ, kv_total * HD), F32)] * 2
    s_len = t_len + (PAST if latent else 0)
    return pl.pallas_call(
        functools.partial(_attn_kernel, kind=kind, t_len=t_len, latent=latent, n_kv=n_kv, gq=gq,
                          qk_norm=qk_gains is not None, use_sink=sink is not None, emit_kv=emit_kv,
                          n_alias=len(aliases), lambda_init=0.8 - 0.6 * math.exp(-0.3 * layer),
                          stack=not latent or kind == "win"),
        grid=grid,
        in_specs=in_specs, out_specs=out_specs, out_shape=out_shape,
        scratch_shapes=[pltpu.VMEM((n_kv, s_len, HD), BF16), pltpu.VMEM((n_kv, s_len, HD), BF16)],
        input_output_aliases=aliases,
        compiler_params=_params(2),
        name=kind + ("_lat" if latent else "_ctx"),
    )(*args)


def _conv_kernel(x_ref, w_ref, b_ref, o_ref, *, t_len):
    u = x_ref[...]
    w = w_ref[...]
    n_rows = u.shape[0]
    t = lax.broadcasted_iota(jnp.int32, (n_rows, 1), 0) % t_len
    half = D_CONV // 2
    acc = u * w[half:half + 1, :] + b_ref[...]
    for s in range(1, half + 1):
        before = jnp.where(t >= s, pltpu.roll(u, s, 0), 0.0)
        after = jnp.where(t < t_len - s, pltpu.roll(u, n_rows - s, 0), 0.0)
        acc = acc + before * w[half - s:half - s + 1, :] + after * w[half + s:half + s + 1, :]
    o_ref[...] = acc * _sigmoid(acc)


def _conv_call(proj, conv_w, conv_b, layer, latent):
    t_len, n_rows, row0 = (DEC_SEQ, R_LAT, R_CTX) if latent else (SEQ, R_CTX, 0)
    tc, tr = 512, 1024
    rb0 = row0 // tr
    return pl.pallas_call(
        functools.partial(_conv_kernel, t_len=t_len),
        grid=(n_rows // tr, CONV_DIM // tc),
        in_specs=[pl.BlockSpec((tr, tc), lambda b, j: (rb0 + b, C_XBC // tc + j)),
                  pl.BlockSpec((None, D_CONV, tc), lambda b, j: (layer, 0, j)),
                  pl.BlockSpec((None, 1, tc), lambda b, j: (layer, 0, j))],
        out_specs=pl.BlockSpec((tr, tc), lambda b, j: (b, j)),
        out_shape=_sds((n_rows, CONV_DIM), F32),
        compiler_params=_params(2),
        name="conv_lat" if latent else "conv_ctx",
    )(proj, conv_w, conv_b.reshape(DEPTH, 1, CONV_DIM))


def _split3(v):
    hi = v.astype(BF16).astype(F32)
    r1 = v - hi
    mid = r1.astype(BF16).astype(F32)
    lo = (r1 - mid).astype(BF16).astype(F32)
    return hi, mid, lo


def _exact_dot(tri, v, tri_first):
    parts = [p.astype(BF16) for p in _split3(v)]
    if tri_first:
        return sum(jnp.dot(tri, p, preferred_element_type=F32) for p in parts)
    return sum(jnp.dot(p, tri, preferred_element_type=F32) for p in parts)


def _lane_spread(v, width):
    hi, mid, lo = _split3(v)
    packed = (hi + pltpu.roll(mid, SSM_H, 1) + pltpu.roll(lo, 2 * SSM_H, 1)).astype(BF16)
    n = SSM_H * width
    src = lax.broadcasted_iota(jnp.int32, (HD, n), 0)
    dst = lax.broadcasted_iota(jnp.int32, (HD, n), 1)
    sel = jnp.where((src < 3 * SSM_H) & (src % SSM_H == dst // width), 1.0, 0.0).astype(BF16)
    return jnp.dot(packed, sel, preferred_element_type=F32)


def _softplus(x):
    return jnp.maximum(x, 0.0) + jnp.log1p(jnp.exp(-jnp.abs(x)))


def _scan_chunk(direction, xa_ref, dt_ref, bias_ref, alog_ref, ht_scr, y_ref):
    ii = lax.broadcasted_iota(jnp.int32, (CHUNK, CHUNK), 0)
    jj = lax.broadcasted_iota(jnp.int32, (CHUNK, CHUNK), 1)
    if direction == 0:
        keep, keep_t = jj <= ii, ii <= jj
        edge = CHUNK - 1
    else:
        keep, keep_t = jj >= ii, ii >= jj
        edge = 0
    tri_col = jnp.where(keep, 1.0, 0.0).astype(BF16)
    tri_row = jnp.where(keep_t, 1.0, 0.0).astype(BF16)

    raw = dt_ref[...]
    if direction == 1:
        raw = pltpu.roll(raw, HD - SSM_H, 1)
    head_lane = lax.broadcasted_iota(jnp.int32, (1, HD), 1) < SSM_H
    dt_c = jnp.where(head_lane, _softplus(raw + bias_ref[...]), 0.0)
    da_c = dt_c * (-jnp.exp(alog_ref[...]))
    a_col = _exact_dot(tri_col, da_c, True)
    a_row = _exact_dot(tri_row, da_c.T[0:SSM_H, :], False)

    a_sq = _lane_spread(a_col, CHUNK)
    a_hp = _lane_spread(a_col, SSM_P)
    dt_hp = _lane_spread(dt_c, SSM_P)

    xa = xa_ref[...]
    xdt = xa[:, :SSM_INNER] * dt_hp
    a_tot = a_hp[edge:edge + 1, :]
    grow = jnp.exp(a_hp)
    w_end = (xdt * jnp.exp(a_tot - a_hp)).astype(BF16)
    carry = jnp.exp(a_tot)
    lower_half = lax.broadcasted_iota(jnp.int32, (1, 2 * SSM_P), 1) < SSM_P

    hg = SSM_H // SSM_G
    gw = hg * SSM_P
    for g in range(SSM_G):
        bm = xa[:, SSM_INNER + g * SSM_N:SSM_INNER + (g + 1) * SSM_N].astype(BF16)
        cm = xa[:, SSM_INNER + (SSM_G + g) * SSM_N:SSM_INNER + (SSM_G + g + 1) * SSM_N].astype(BF16)
        cb = _nt_dot(cm, bm)
        gl = slice(g * gw, (g + 1) * gw)
        ht = ht_scr[:, gl]
        y_off = jnp.dot(cm, ht.astype(BF16), preferred_element_type=F32) * grow[:, gl]
        for pr in range(hg // 2):
            pl_ = slice(g * gw + pr * 2 * SSM_P, g * gw + (pr + 1) * 2 * SSM_P)
            x_pair = xdt[:, pl_]
            y_pair = y_off[:, pr * 2 * SSM_P:(pr + 1) * 2 * SSM_P]
            for half in range(2):
                h = g * hg + pr * 2 + half
                decay = jnp.exp(jnp.where(keep, a_sq[:, h * CHUNK:(h + 1) * CHUNK] - a_row[h:h + 1, :], -1e30))
                rhs = jnp.where(lower_half if half == 0 else ~lower_half, x_pair, 0.0).astype(BF16)
                y_pair = y_pair + jnp.dot((cb * decay).astype(BF16), rhs, preferred_element_type=F32)
            y_ref[:, pl_] = y_pair
        st = lax.dot_general(bm, w_end[:, gl], (((0,), (0,)), ((), ())), preferred_element_type=F32)
        ht_scr[:, gl] = ht * carry[:, gl] + st


def _scan_kernel(*refs, has_h0, emit_state, n_chunks):
    it = iter(refs)
    ins = [[next(it) for _ in range(4)] for _ in range(2)]
    if has_h0:
        h0_ref = next(it)
    if emit_state:
        next(it)
    y_refs = [next(it), next(it)]
    if emit_state:
        hend_ref = next(it)
    ht_scr = next(it)
    c = pl.program_id(1)

    @pl.when(c == 0)
    def _():
        for d in range(2):
            if has_h0:
                ht_scr[d] = h0_ref[d].reshape(SSM_INNER, SSM_N).T
            else:
                ht_scr[d] = jnp.zeros(ht_scr.shape[1:], F32)

    for d in range(2):
        _scan_chunk(d, *ins[d], ht_scr.at[d], y_refs[d])

    if emit_state:
        @pl.when(c == n_chunks - 1)
        def _():
            for d in range(2):
                hend_ref[d] = ht_scr[d].T.reshape(SSM_H, SSM_P, SSM_N)


def _pad_lanes(v):
    return jnp.pad(v, ((0, 0), (0, 0), (0, HD - SSM_H))).reshape(DEPTH, 2, 1, HD)


def _scan_call(xact, proj, dt_bias, a_log, state, layer, latent, state_out=None):
    t_len, nb, row0 = (DEC_SEQ, DEC_BATCH, R_CTX) if latent else (SEQ, BATCH, 0)
    nc = t_len // CHUNK
    rb0 = row0 // CHUNK
    chunk_of = (lambda c: c, lambda c: nc - 1 - c)
    in_specs, args = [], []
    for d in range(2):
        in_specs += [pl.BlockSpec((CHUNK, CONV_DIM), lambda b, c, d=d: (b * nc + chunk_of[d](c), 0)),
                     pl.BlockSpec((CHUNK, HD), lambda b, c, d=d: (rb0 + b * nc + chunk_of[d](c), C_DT // HD)),
                     pl.BlockSpec((None, None, 1, HD), lambda b, c, d=d: (layer, d, 0, 0)),
                     pl.BlockSpec((None, None, 1, HD), lambda b, c, d=d: (layer, d, 0, 0))]
        args += [xact, proj, _pad_lanes(dt_bias), _pad_lanes(a_log)]
    state_spec = pl.BlockSpec((None, None, 2, SSM_H, SSM_P, SSM_N), lambda b, c: (b, layer, 0, 0, 0, 0))
    if latent:
        in_specs.append(state_spec)
        args.append(state)
    out_specs = [pl.BlockSpec((CHUNK, SSM_INNER), lambda b, c, d=d: (b * nc + chunk_of[d](c), 0)) for d in range(2)]
    out_shape = [_sds((nb * t_len, SSM_INNER), F32)] * 2
    aliases = {}
    emit_state = state_out is not None
    if emit_state:
        aliases[len(args)] = 2
        in_specs.append(pl.BlockSpec(memory_space=pl.ANY))
        args.append(state_out)
        out_specs.append(state_spec)
        out_shape.append(_sds((nb, DEPTH, 2, SSM_H, SSM_P, SSM_N), F32))
    return pl.pallas_call(
        functools.partial(_scan_kernel, has_h0=latent, emit_state=emit_state, n_chunks=nc),
        grid=(nb, nc),
        in_specs=in_specs, out_specs=out_specs, out_shape=out_shape,
        scratch_shapes=[pltpu.VMEM((2, SSM_N, SSM_INNER), F32)],
        input_output_aliases=aliases,
        compiler_params=_params(2),
        name="ssd_lat" if latent else "ssd_ctx",
    )(*args)


def _ssd_out_kernel(yf_ref, yb_ref, xs_ref, z_ref, dskip_ref, g_ref, mix_ref, o_ref):
    del mix_ref
    dsk = dskip_ref[0:1, :] + dskip_ref[1:2, :]
    z = z_ref[...]
    y = (yf_ref[...] + yb_ref[...] + dsk * xs_ref[...]) * (z * _sigmoid(z))
    gw = SSM_INNER // SSM_G
    g = g_ref[...]
    for k in range(SSM_G):
        lanes = slice(k * gw, (k + 1) * gw)
        yk = y[:, lanes]
        yk = yk * lax.rsqrt(jnp.mean(yk * yk, axis=-1, keepdims=True) + EPS)
        o_ref[:, lanes] = (yk * g[:, lanes]).astype(BF16)


def _ssd_out_call(yf, yb, xact, proj, d_skip, g_norm, mix, layer, latent):
    tm = 256
    rb0 = (R_CTX if latent else 0) // tm
    n_rows = yf.shape[0]
    local = pl.BlockSpec((tm, SSM_INNER), lambda i: (i, 0))
    return pl.pallas_call(
        _ssd_out_kernel,
        grid=(n_rows // tm,),
        in_specs=[local, local, local,
                  pl.BlockSpec((tm, SSM_INNER), lambda i: (rb0 + i, C_Z // SSM_INNER)),
                  pl.BlockSpec((None, 2, SSM_INNER), lambda i: (layer, 0, 0)),
                  pl.BlockSpec((None, 1, SSM_INNER), lambda i: (layer, 0, 0)),
                  pl.BlockSpec(memory_space=pl.ANY)],
        out_specs=pl.BlockSpec((tm, SSM_INNER), lambda i: (rb0 + i, 3)),
        out_shape=_sds((R, D), BF16),
        input_output_aliases={6: 0},
        compiler_params=_params(1),
        name="ssd_out_lat" if latent else "ssd_out_ctx",
    )(yf, yb, xact, proj, jnp.repeat(d_skip, SSM_P, axis=-1), g_norm.reshape(DEPTH, 1, SSM_INNER), mix)


def kernel(x_prompt, x_sample, c, cache_a_k, cache_a_v, cache_b_k, cache_b_v, cache_c_k, cache_c_v, state_ssm, c_ctx, w_mod, b_mod, g_pre_mix, g_post_mix, g_pre_ffn, g_post_ffn, w_in, w_out, lam_q1, lam_k1, lam_q2, lam_k2, g_subln, g_qnorm, g_knorm, sink, conv_w, conv_b, dt_bias, a_log, d_skip, g_ssm_norm, w_up, w_down):
    x = (x_prompt.reshape(R_CTX, D), x_sample.reshape(R_LAT, D))
    cvec = jnp.concatenate([c_ctx[None, :], c, jnp.zeros((8 - 1 - DEC_BATCH, D), F32)], axis=0)
    w_in_t = jnp.swapaxes(w_in, 1, 2)
    mods, w_in_b = _mod_call(cvec, w_mod, b_mod, (w_in_t, 0, IN_WP))
    mods4 = mods.reshape(DEPTH, 8, 1, N_MOD * D)

    tabs_a = _rope_tables(HD // 8)
    tabs_bc = _rope_tables(HD // 4)
    lam_vecs = (lam_q1, lam_k1, lam_q2, lam_k2)

    h, w_out_b = _norm_call(x, mods4, nxt=(g_pre_mix, 0, 1, 0), casts=((w_out, 0, D),))
    kv_rows = BATCH * DEPTH * SEQ
    zero_bufs = [((R, D), BF16)]
    zero_bufs += [((kv_rows, N_HEADS * HD), F32)] * 2 + [((kv_rows, N_KV * HD), F32)] * 4
    zero_bufs += [((BATCH * DEPTH * 2 * SSM_H * SSM_P, SSM_N), F32)]
    for layer in range(DEPTH):
        last = layer + 1 == DEPTH
        proj, w_up_b, mix, *zeros = _mm_call(h, w_in_b, F32, name="in_proj", cast=(w_up, layer, D),
                                             w_rows_out=True, zero_fill=zero_bufs if layer == 0 else zero_bufs[:1])
        if layer == 0:
            kv_a = [z.reshape(BATCH, DEPTH, SEQ, -1) for z in zeros[0:2]]
            kv_b = [z.reshape(BATCH, DEPTH, SEQ, -1) for z in zeros[2:4]]
            kv_c = [z.reshape(BATCH, DEPTH, SEQ, -1) for z in zeros[4:6]]
            ssm_new = zeros[6].reshape(BATCH, DEPTH, 2, SSM_H, SSM_P, SSM_N)

        mix, *kv_a = _attn_call("diff", proj, mix, layer, False, (C_QA, C_KA, C_VA), 0, N_HEADS,
                                lam_vecs=lam_vecs, g_subln=g_subln, kv_out=kv_a)
        mix, *kv_b = _attn_call("gqa", proj, mix, layer, False, (C_QB, C_KB, C_VB), 1024, N_KV,
                                qk_gains=(g_qnorm, g_knorm), kv_out=kv_b)
        mix, *kv_c = _attn_call("gqa", proj, mix, layer, False, (C_QC, C_KC, C_VC), 2048, 1,
                                sink=sink, kv_out=kv_c)
        mix, = _attn_call("diff", proj, mix, layer, True, (C_QA, C_KA, C_VA), 0, 1,
                          caches=(cache_a_k, cache_a_v), tabs=tabs_a, lam_vecs=lam_vecs, g_subln=g_subln)
        mix, = _attn_call("gqa", proj, mix, layer, True, (C_QB, C_KB, C_VB), 1024, 1,
                          caches=(cache_b_k, cache_b_v), tabs=tabs_bc, qk_gains=(g_qnorm, g_knorm))
        mix, = _attn_call("win", proj, mix, layer, True, (C_QC, C_KC, C_VC), 2048, 1,
                          caches=(cache_c_k, cache_c_v), tabs=tabs_bc, sink=sink)

        for latent in (False, True):
            xact = _conv_call(proj, conv_w, conv_b, layer, latent)
            y_f, y_b, *st = _scan_call(xact, proj, dt_bias, a_log, state_ssm, layer, latent,
                                       state_out=None if latent else ssm_new)
            if not latent:
                (ssm_new,) = st
            mix = _ssd_out_call(y_f, y_b, xact, proj, d_skip, g_ssm_norm, mix, layer, latent)

        if last:
            (mm,) = _mm_call(mix, w_out_b, F32, name="out_proj")
        else:
            mm, w_out_b = _mm_call(mix, w_out_b, F32, name="out_proj", cast=(w_out, layer + 1, D))
        x, h2 = _norm_call(x, mods4, mm=mm, post=(g_post_mix, layer, 2), nxt=(g_pre_ffn, layer, 4, 3))
        u, w_down_b = _mm_call(h2, w_up_b, BF16, relu2=True, name="ffn_up", cast=(w_down, layer, D_FF))
        if last:
            (mm,) = _mm_call(u, w_down_b, F32, name="ffn_down")
            x_ctx, x_lat = _norm_call(x, mods4, mm=mm, post=(g_post_ffn, layer, 5), split_out=True)
        else:
            mm, w_in_b = _mm_call(u, w_down_b, F32, name="ffn_down", cast=(w_in_t, layer + 1, IN_WP))
            x, h = _norm_call(x, mods4, mm=mm, post=(g_post_ffn, layer, 5), nxt=(g_pre_mix, layer + 1, 1, 0))

    new_kv = [a.reshape(BATCH, DEPTH, SEQ, -1, HD) for a in (*kv_a, *kv_b, *kv_c)]
    return (x_ctx.reshape(BATCH, SEQ, D), x_lat.reshape(DEC_BATCH, DEC_SEQ, D), *new_kv, ssm_new)
```
